```python
import math
import jax, jax.numpy as jnp
from jax import lax
import numpy as np

D_MODEL = 2048
BATCH = 16
SEQ = 256
DEPTH = 2
DEC_BATCH = 8
DEC_SEQ = 1024
PAST_LEN = 256

GRID_W = 64
DA_HEADS = 8
DA_HEAD_DIM = 64
DA_VDIM = 2 * DA_HEAD_DIM
DA_WIDTH = DA_HEADS * 2 * DA_HEAD_DIM
SC_WIDTH = 512
SC_K = 3
CF_WIDTH = 512
CF_K = 31
N_BRANCH = 3
FF_HIDDEN = -(-8 * D_MODEL // (3 * 256)) * 256
ROPE_THETA = 10000.0
Q_BLOCK = 128
EPS = 1e-6

OFF_K = DA_WIDTH
OFF_V = 2 * DA_WIDTH
OFF_SC = 3 * DA_WIDTH
OFF_CF = OFF_SC + 3 * SC_WIDTH
OFF_GATE = OFF_CF + 2 * CF_WIDTH
IN_COLS = OFF_GATE + N_BRANCH * D_MODEL

kernel_name = 'hybrid_diffattn_conv_prefix_dit_step'


def rmsnorm(x, g):
    xf = x.astype(jnp.float32)
    y = xf * lax.rsqrt(jnp.mean(xf * xf, axis=-1, keepdims=True) + EPS)
    return (y * g.astype(jnp.float32)).astype(x.dtype)


def layernorm(x, g, b):
    xf = x.astype(jnp.float32)
    mu = jnp.mean(xf, axis=-1, keepdims=True)
    xc = xf - mu
    y = xc * lax.rsqrt(jnp.mean(xc * xc, axis=-1, keepdims=True) + EPS)
    return (y * g.astype(jnp.float32) + b.astype(jnp.float32)).astype(x.dtype)


def dwconv(x, w, b=None):
    k = w.shape[0]
    y = lax.conv_general_dilated(x, w[:, None, :].astype(x.dtype), window_strides=(1,),
                                 padding=[(k // 2, k // 2)],
                                 dimension_numbers=('NWC', 'WIO', 'NWC'),
                                 feature_group_count=x.shape[-1])
    return y if b is None else y + b


def axial_angles(n_tok):
    rows = n_tok // GRID_W
    row_ids = jnp.repeat(jnp.arange(rows), GRID_W).astype(jnp.float32)
    col_ids = jnp.tile(jnp.arange(GRID_W), rows).astype(jnp.float32)
    n_freq = DA_HEAD_DIM // 4
    inv = ROPE_THETA ** (-jnp.arange(n_freq, dtype=jnp.float32) / n_freq)
    return row_ids[:, None] * inv, col_ids[:, None] * inv


def rope_half(x, ang):
    cos = jnp.cos(ang)[None, :, None, None, :].astype(x.dtype)
    sin = jnp.sin(ang)[None, :, None, None, :].astype(x.dtype)
    x1, x2 = jnp.split(x, 2, axis=-1)
    return jnp.concatenate([x1 * cos - x2 * sin, x2 * cos + x1 * sin], axis=-1)


def axial_rope(x, ang_row, ang_col):
    half = DA_HEAD_DIM // 2
    return jnp.concatenate([rope_half(x[..., :half], ang_row),
                            rope_half(x[..., half:], ang_col)], axis=-1)


def diff_attention(q, k, v, lam, lam_init, subln_g):
    b, lq = q.shape[0], q.shape[1]
    nb = lq // Q_BLOCK
    scale = DA_HEAD_DIM ** -0.5
    qb = q.reshape(b, nb, Q_BLOCK, DA_HEADS, 2, DA_HEAD_DIM).transpose(1, 0, 2, 3, 4, 5)

    def one_block(qblk):
        s = jnp.einsum('bqhcd,bkhcd->bchqk', qblk, k).astype(jnp.float32) * scale
        p = jax.nn.softmax(s, axis=-1)
        a = p[:, 0] - lam * p[:, 1]
        return jnp.einsum('bhqk,bkhe->bqhe', a.astype(v.dtype), v)

    o = lax.map(one_block, qb)
    o = o.transpose(1, 0, 2, 3, 4).reshape(b, lq, DA_HEADS, DA_VDIM)
    o = rmsnorm(o, subln_g) * (1.0 - lam_init)
    return o.reshape(b, lq, DA_HEADS * DA_VDIM)


def trunk_layer(x, mod, l, P, kv_ctx, angles):
    shift1, scale1, gate1, shift2, scale2, gate2 = jnp.split(mod, 6, axis=-1)
    b, n, _ = x.shape
    h = rmsnorm(x, P['g_norm1'][l]) * (1 + scale1) + shift1
    proj = h @ P['w_in'][l]
    q, k, v, sc, cf, gt = jnp.split(proj, [OFF_K, OFF_V, OFF_SC, OFF_CF, OFF_GATE], axis=-1)
    q = q.reshape(b, n, DA_HEADS, 2, DA_HEAD_DIM)
    k = k.reshape(b, n, DA_HEADS, 2, DA_HEAD_DIM)
    v = v.reshape(b, n, DA_HEADS, DA_VDIM)
    if angles is not None:
        q = axial_rope(q, *angles)
        k = axial_rope(k, *angles)
    if kv_ctx is None:
        k_all, v_all = k, v
    else:
        k_ctx, v_ctx = kv_ctx
        k_ctx = k_ctx.reshape(b, k_ctx.shape[1], DA_HEADS, 2, DA_HEAD_DIM).astype(k.dtype)
        k_all = jnp.concatenate([k_ctx, k], axis=1)
        v_all = jnp.concatenate([v_ctx.astype(v.dtype), v], axis=1)
    lam_init = 0.8 - 0.6 * math.exp(-0.3 * l)
    lq1, lk1, lq2, lk2 = [t.astype(jnp.float32) for t in P['da_lambda'][l]]
    lam = jnp.exp(jnp.sum(lq1 * lk1)) - jnp.exp(jnp.sum(lq2 * lk2)) + lam_init
    attn = diff_attention(q, k_all, v_all, lam, lam_init, P['da_subln'][l])
    branch_a = attn @ P['w_da_out'][l]
    g_b, g_c, sx = jnp.split(sc, 3, axis=-1)
    branch_b = (g_b * dwconv(g_c * sx, P['sc_conv'][l])) @ P['w_sc_out'][l]
    ca, cb = jnp.split(cf, 2, axis=-1)
    cy = dwconv(ca * jax.nn.sigmoid(cb), P['cf_conv'][l], P['cf_conv_b'][l])
    cy = jax.nn.silu(layernorm(cy, P['cf_ln_g'][l], P['cf_ln_b'][l]))
    branch_c = cy @ P['w_cf_out'][l]
    ga, gb, gc = jnp.split(jax.nn.sigmoid(gt + P['b_gate'][l]), 3, axis=-1)
    merged = ga * branch_a + gb * branch_b + gc * branch_c
    x = x + gate1 * (merged @ P['w_out'][l])
    h2 = rmsnorm(x, P['g_norm2'][l]) * (1 + scale2) + shift2
    u, w = jnp.split(h2 @ P['w_ffn_in'][l], 2, axis=-1)
    x = x + gate2 * ((jax.nn.silu(u) * w) @ P['w_ffn_out'][l])
    return x, (k.reshape(b, n, DA_HEADS, 2 * DA_HEAD_DIM), v)


def setup_inputs(seed: int = 0) -> dict:
    key = jax.random.key(seed)
    ks = jax.random.split(key, 32)
    f32 = jnp.float32

    def nrm(k, shape, scale=1.0):
        return jax.random.normal(k, shape, f32) * scale

    D = D_MODEL
    return {
        'x_prompt': nrm(ks[0], (BATCH, SEQ, D)),
        'x_sample': nrm(ks[1], (DEC_BATCH, DEC_SEQ, D)),
        'cache_k': nrm(ks[2], (DEC_BATCH, DEPTH, PAST_LEN, DA_HEADS, 2 * DA_HEAD_DIM)),
        'cache_v': nrm(ks[3], (DEC_BATCH, DEPTH, PAST_LEN, DA_HEADS, DA_VDIM)),
        'c': nrm(ks[4], (DEC_BATCH, D)),
        'c_ctx': nrm(ks[5], (D,)),
        'w_mod': nrm(ks[6], (DEPTH, D, 6 * D), 0.5 * D ** -0.5),
        'b_mod': nrm(ks[7], (DEPTH, 6 * D), 0.01),
        'g_norm1': 1.0 + nrm(ks[8], (DEPTH, D), 0.01),
        'w_in': nrm(ks[9], (DEPTH, D, IN_COLS), D ** -0.5),
        'da_lambda': nrm(ks[10], (DEPTH, 4, DA_HEAD_DIM), 0.1),
        'da_subln': 1.0 + nrm(ks[11], (DEPTH, DA_VDIM), 0.01),
        'w_da_out': nrm(ks[12], (DEPTH, DA_WIDTH, D), DA_WIDTH ** -0.5),
        'sc_conv': nrm(ks[13], (DEPTH, SC_K, SC_WIDTH), SC_K ** -0.5),
        'w_sc_out': nrm(ks[14], (DEPTH, SC_WIDTH, D), SC_WIDTH ** -0.5),
        'cf_conv': nrm(ks[15], (DEPTH, CF_K, CF_WIDTH), CF_K ** -0.5),
        'cf_conv_b': nrm(ks[16], (DEPTH, CF_WIDTH), 0.01),
        'cf_ln_g': 1.0 + nrm(ks[17], (DEPTH, CF_WIDTH), 0.01),
        'cf_ln_b': nrm(ks[18], (DEPTH, CF_WIDTH), 0.01),
        'w_cf_out': nrm(ks[19], (DEPTH, CF_WIDTH, D), CF_WIDTH ** -0.5),
        'b_gate': nrm(ks[20], (DEPTH, N_BRANCH * D), 0.01),
        'w_out': nrm(ks[21], (DEPTH, D, D), D ** -0.5),
        'g_norm2': 1.0 + nrm(ks[22], (DEPTH, D), 0.01),
        'w_ffn_in': nrm(ks[23], (DEPTH, D, 2 * FF_HIDDEN), D ** -0.5),
        'w_ffn_out': nrm(ks[24], (DEPTH, FF_HIDDEN, D), FF_HIDDEN ** -0.5),
        'g_final': 1.0 + nrm(ks[25], (D,), 0.01),
    }


def reference(x_prompt, x_sample, cache_k, cache_v, c, c_ctx, w_mod, b_mod, g_norm1, w_in,
              da_lambda, da_subln, w_da_out, sc_conv, w_sc_out, cf_conv, cf_conv_b, cf_ln_g,
              cf_ln_b, w_cf_out, b_gate, w_out, g_norm2, w_ffn_in, w_ffn_out, g_final):
    P = dict(w_mod=w_mod, b_mod=b_mod, g_norm1=g_norm1, w_in=w_in, da_lambda=da_lambda,
             da_subln=da_subln, w_da_out=w_da_out, sc_conv=sc_conv, w_sc_out=w_sc_out,
             cf_conv=cf_conv, cf_conv_b=cf_conv_b, cf_ln_g=cf_ln_g, cf_ln_b=cf_ln_b,
             w_cf_out=w_cf_out, b_gate=b_gate, w_out=w_out, g_norm2=g_norm2,
             w_ffn_in=w_ffn_in, w_ffn_out=w_ffn_out)

    xp = x_prompt
    ks_new, vs_new = [], []
    for l in range(DEPTH):
        mod_ctx = jax.nn.silu(c_ctx) @ w_mod[l] + b_mod[l]
        xp, (k_l, v_l) = trunk_layer(xp, mod_ctx, l, P, None, None)
        ks_new.append(k_l)
        vs_new.append(v_l)
    y_prompt = rmsnorm(xp, g_final)
    new_k = jnp.stack(ks_new, axis=1)
    new_v = jnp.stack(vs_new, axis=1)

    xs = x_sample
    angles = axial_angles(xs.shape[1])
    for l in range(DEPTH):
        mod_lat = (jax.nn.silu(c) @ w_mod[l] + b_mod[l])[:, None, :]
        xs, _ = trunk_layer(xs, mod_lat, l, P, (cache_k[:, l], cache_v[:, l]), angles)
    y_sample = rmsnorm(xs, g_final)

    return (y_prompt, y_sample, new_k, new_v)
```

```python
import functools
import math

import jax
import jax.numpy as jnp
from jax import lax
from jax.experimental import pallas as pl
from jax.experimental.pallas import tpu as pltpu

D_MODEL = 2048
BATCH = 16
SEQ = 256
DEPTH = 2
DEC_BATCH = 8
DEC_SEQ = 1024
PAST_LEN = 256
GRID_W = 64
DA_HEADS = 8
DA_HEAD_DIM = 64
DA_VDIM = 2 * DA_HEAD_DIM
DA_WIDTH = DA_HEADS * 2 * DA_HEAD_DIM
SC_WIDTH = 512
SC_K = 3
CF_WIDTH = 512
CF_K = 31
N_BRANCH = 3
FF_HIDDEN = -(-8 * D_MODEL // (3 * 256)) * 256
ROPE_THETA = 10000.0
EPS = 1e-6

OFF_K = DA_WIDTH
OFF_V = 2 * DA_WIDTH
OFF_SC = 3 * DA_WIDTH
OFF_CF = OFF_SC + 3 * SC_WIDTH
OFF_GATE = OFF_CF + 2 * CF_WIDTH
IN_COLS = OFF_GATE + N_BRANCH * D_MODEL

N_CTX = BATCH * SEQ
N_LAT = DEC_BATCH * DEC_SEQ
N_TOK = N_CTX + N_LAT
MOD_ROWS = 16

F32 = jnp.float32
BF16 = jnp.bfloat16

VMEM_BUDGET_V7X = 56 * 1024 * 1024
LANES = 128
SUBLANES = 8

TM = 1024
TN = 512
TM_FFN = 512
TH_FFN = 512
TQ_LAT = 512
CONV_ROWS = 32


def _nbytes(shape, dtype):
    return math.prod(shape) * jnp.dtype(dtype).itemsize


def _vmem_limit(blocks, scratch=(), temps=()):
    total = 2 * sum(_nbytes(s, d) for s, d in blocks)
    total += sum(_nbytes(s, d) for s, d in scratch)
    total += sum(_nbytes(s, d) for s, d in temps)
    return min(total, VMEM_BUDGET_V7X)


def _mod_group(i, tm):
    return jnp.maximum(i * tm - N_CTX + DEC_SEQ, 0) // DEC_SEQ


def _rms(x, g):
    return x * lax.rsqrt(jnp.mean(x * x, axis=-1, keepdims=True) + EPS) * g


def _mod_norm(x, g, shift, scale):
    return _rms(x, g) * (1.0 + scale) + shift


def _sigmoid(x):
    return 1.0 / (1.0 + jnp.exp(-x))


def _mod_kernel(c_ref, w_ref, b_ref, o_ref):
    c = c_ref[...]
    s = (c * _sigmoid(c)).astype(BF16)
    o_ref[...] = jnp.dot(s, w_ref[...].astype(BF16), preferred_element_type=F32) + b_ref[...]


def _modulation(cvec, w_mod, b_mod):
    bn = 1024
    return pl.pallas_call(
        _mod_kernel,
        grid=(DEPTH, 6 * D_MODEL // bn),
        in_specs=[
            pl.BlockSpec((MOD_ROWS, D_MODEL), lambda l, j: (0, 0)),
            pl.BlockSpec((None, D_MODEL, bn), lambda l, j: (l, 0, j)),
            pl.BlockSpec((None, 1, bn), lambda l, j: (l, 0, j)),
        ],
        out_specs=pl.BlockSpec((None, MOD_ROWS, bn), lambda l, j: (l, 0, j)),
        out_shape=jax.ShapeDtypeStruct((DEPTH, MOD_ROWS, 6 * D_MODEL), F32),
        compiler_params=pltpu.CompilerParams(
            dimension_semantics=("parallel", "parallel"),
            vmem_limit_bytes=_vmem_limit(
                [((D_MODEL, bn), F32), ((MOD_ROWS, D_MODEL), F32), ((MOD_ROWS, bn), F32)],
                temps=[((D_MODEL, bn), BF16), ((D_MODEL, bn), F32)])),
        name="modulation",
    )(cvec, w_mod, b_mod.reshape(DEPTH, 1, 6 * D_MODEL))


def _mod_spec(layer, chunk, tm):
    return pl.BlockSpec((None, None, 1, D_MODEL),
                        lambda i, j: (layer, _mod_group(i, tm), 0, chunk))


def _inproj_kernel(x_ref, g_ref, shift_ref, scale_ref, w_ref, o_ref, h_scr):
    @pl.when(pl.program_id(1) == 0)
    def _():
        h_scr[...] = _mod_norm(x_ref[...], g_ref[...], shift_ref[...], scale_ref[...]).astype(BF16)

    o_ref[...] = jnp.dot(h_scr[...], w_ref[...], preferred_element_type=F32)


def _in_projection(x, g, mod4, layer, w_in):
    return pl.pallas_call(
        _inproj_kernel,
        grid=(N_TOK // TM, IN_COLS // TN),
        in_specs=[
            pl.BlockSpec((TM, D_MODEL), lambda i, j: (i, 0)),
            pl.BlockSpec((1, D_MODEL), lambda i, j: (0, 0)),
            _mod_spec(layer, 0, TM),
            _mod_spec(layer, 1, TM),
            pl.BlockSpec((D_MODEL, TN), lambda i, j: (0, j)),
        ],
        out_specs=pl.BlockSpec((TM, TN), lambda i, j: (i, j)),
        out_shape=jax.ShapeDtypeStruct((N_TOK, IN_COLS), F32),
        scratch_shapes=[pltpu.VMEM((TM, D_MODEL), BF16)],
        compiler_params=pltpu.CompilerParams(
            dimension_semantics=("parallel", "arbitrary"),
            vmem_limit_bytes=_vmem_limit(
                [((TM, D_MODEL), F32), ((D_MODEL, TN), BF16), ((TM, TN), F32)],
                scratch=[((TM, D_MODEL), BF16)],
                temps=[((TM, D_MODEL), F32), ((TM, TN), F32)])),
        name="in_projection",
    )(x, g, mod4, mod4, w_in)


def _rope(x, cos, sin_lo, sin_hi):
    return (x * cos + pltpu.roll(x, LANES - 16, 1) * sin_lo + pltpu.roll(x, 16, 1) * sin_hi)


def _softmax_rows(s):
    e = jnp.exp(s - jnp.max(s, axis=-1, keepdims=True))
    return e * (1.0 / jnp.sum(e, axis=-1, keepdims=True))


def _make_attn_kernel(seq_len, tq, n_cache, use_rope, lam_init):
    def kernel(*refs):
        dl_ref, g_ref, q_ref, k_ref, v_ref = refs[:5]
        refs = refs[5:]
        if n_cache:
            ck_ref, cv_ref = refs[:2]
            refs = refs[2:]
        if use_rope:
            cos_ref, slo_ref, shi_ref = refs[:3]
            refs = refs[3:]
        o_ref, kk_scr, vv_scr = refs
        qb = pl.program_id(2)

        @pl.when(qb == 0)
        def _():
            k = k_ref[...]
            if use_rope:
                k = _rope(k, cos_ref[...], slo_ref[...], shi_ref[...])
            if n_cache:
                kk_scr[0:n_cache, :] = ck_ref[...].astype(BF16)
                vv_scr[0:n_cache, :] = cv_ref[...].astype(BF16)
            kk_scr[n_cache:n_cache + seq_len, :] = k.astype(BF16)
            vv_scr[n_cache:n_cache + seq_len, :] = v_ref[...].astype(BF16)

        q = q_ref[...]
        if use_rope:
            rows = pl.ds(pl.multiple_of(qb * tq, tq), tq)
            q = _rope(q, cos_ref[rows, :], slo_ref[rows, :], shi_ref[rows, :])
        q = q * (DA_HEAD_DIM ** -0.5)
        lane = lax.broadcasted_iota(jnp.int32, q.shape, 1)
        q1 = jnp.where(lane < DA_HEAD_DIM, q, 0.0).astype(BF16)
        q2 = jnp.where(lane >= DA_HEAD_DIM, q, 0.0).astype(BF16)
        kk = kk_scr[...]
        nt = (((1,), (1,)), ((), ()))
        p1 = _softmax_rows(lax.dot_general(q1, kk, nt, preferred_element_type=F32))
        p2 = _softmax_rows(lax.dot_general(q2, kk, nt, preferred_element_type=F32))

        dl = dl_ref[...]
        lam = (jnp.exp(jnp.sum(dl[0:1] * dl[1:2], axis=-1, keepdims=True))
               - jnp.exp(jnp.sum(dl[2:3] * dl[3:4], axis=-1, keepdims=True)) + lam_init)
        a = (p1 - lam * p2).astype(BF16)
        o = jnp.dot(a, vv_scr[...], preferred_element_type=F32)
        o_ref[...] = (_rms(o, g_ref[...]) * (1.0 - lam_init)).astype(BF16)

    return kernel


def _diff_attention(proj, da_lambda_l, subln_g, layer, *, latent, cache=None, rope=None):
    lam_init = 0.8 - 0.6 * math.exp(-0.3 * layer)
    if latent:
        n_seq, seq_len, tq, n_cache, row0 = DEC_BATCH, DEC_SEQ, TQ_LAT, PAST_LEN, N_CTX
    else:
        n_seq, seq_len, tq, n_cache, row0 = BATCH, SEQ, SEQ, 0, 0
    nq = seq_len // tq
    lk = n_cache + seq_len
    hk, hv = OFF_K // LANES, OFF_V // LANES

    in_specs = [
        pl.BlockSpec((4, DA_HEAD_DIM), lambda b, h, t: (0, 0)),
        pl.BlockSpec((1, DA_VDIM), lambda b, h, t: (0, 0)),
        pl.BlockSpec((tq, LANES), lambda b, h, t: (row0 // tq + b * nq + t, h)),
        pl.BlockSpec((seq_len, LANES), lambda b, h, t: (row0 // seq_len + b, hk + h)),
        pl.BlockSpec((seq_len, LANES), lambda b, h, t: (row0 // seq_len + b, hv + h)),
    ]
    args = [da_lambda_l, subln_g, proj, proj, proj]
    blocks = [((tq, LANES), F32), ((seq_len, LANES), F32), ((seq_len, LANES), F32), ((tq, LANES), BF16)]
    if latent:
        cache_k, cache_v = cache
        in_specs += [pl.BlockSpec((None, None, n_cache, LANES), lambda b, h, t: (b, layer, 0, h))] * 2
        args += [cache_k, cache_v]
        in_specs += [pl.BlockSpec((seq_len, LANES), lambda b, h, t: (0, 0))] * 3
        args += list(rope)
        blocks += [((n_cache, LANES), F32)] * 2 + [((seq_len, LANES), F32)] * 3

    return pl.pallas_call(
        _make_attn_kernel(seq_len, tq, n_cache, latent, lam_init),
        grid=(n_seq, DA_HEADS, nq),
        in_specs=in_specs,
        out_specs=pl.BlockSpec((tq, LANES), lambda b, h, t: (b * nq + t, h)),
        out_shape=jax.ShapeDtypeStruct((n_seq * seq_len, DA_WIDTH), BF16),
        scratch_shapes=[pltpu.VMEM((lk, LANES), BF16), pltpu.VMEM((lk, LANES), BF16)],
        compiler_params=pltpu.CompilerParams(
            dimension_semantics=("parallel", "parallel", "arbitrary"),
            vmem_limit_bytes=_vmem_limit(
                blocks, scratch=[((lk, LANES), BF16)] * 2,
                temps=[((tq, lk), F32)] * 8)),
        name="diff_attention_latent" if latent else "diff_attention_context",
    )(*args)


def _fill_shifted(scr, value, pad, seq_len, shifts):
    width = value.shape[-1]
    n = seq_len + 2 * pad - SUBLANES
    scr[0, 0:pad, :] = jnp.zeros((pad, width), F32)
    scr[0, pad + seq_len:, :] = jnp.zeros((pad, width), F32)
    scr[0, pad:pad + seq_len, :] = value
    for s, shift in enumerate(shifts, start=1):
        scr[s, 0:n, :] = scr[0, shift:shift + n, :]


def _tap(scr, shifts, r0, offset):
    aligned, rem = offset - offset % SUBLANES, offset % SUBLANES
    slot = 0 if rem == 0 else 1 + shifts.index(rem)
    return scr[slot, pl.ds(r0 + aligned, CONV_ROWS), :]


def _make_conv_kernel(seq_len):
    pad3, pad31 = SUBLANES, 2 * SUBLANES
    shifts3 = (1, SUBLANES - 1)
    shifts31 = tuple(range(1, SUBLANES))

    def kernel(gb_ref, gc_ref, sx_ref, ca_ref, cb_ref, w3_ref, w31_ref, b31_ref, lng_ref, lnb_ref,
               bo_ref, co_ref, t_scr, u_scr):
        _fill_shifted(t_scr, gc_ref[...] * sx_ref[...], pad3, seq_len, shifts3)
        _fill_shifted(u_scr, ca_ref[...] * _sigmoid(cb_ref[...]), pad31, seq_len, shifts31)

        def chunk(c, carry):
            r0 = pl.multiple_of(c * CONV_ROWS, CONV_ROWS)
            rows = pl.ds(r0, CONV_ROWS)
            y = _tap(t_scr, shifts3, r0, pad3 - SC_K // 2) * w3_ref[0:1, :]
            for k in range(1, SC_K):
                y = y + _tap(t_scr, shifts3, r0, pad3 - SC_K // 2 + k) * w3_ref[k:k + 1, :]
            bo_ref[rows, :] = (gb_ref[rows, :] * y).astype(BF16)

            acc = _tap(u_scr, shifts31, r0, pad31 - CF_K // 2) * w31_ref[0:1, :]
            for k in range(1, CF_K):
                acc = acc + _tap(u_scr, shifts31, r0, pad31 - CF_K // 2 + k) * w31_ref[k:k + 1, :]
            acc = acc + b31_ref[...]
            xc = acc - jnp.mean(acc, axis=-1, keepdims=True)
            yn = xc * lax.rsqrt(jnp.mean(xc * xc, axis=-1, keepdims=True) + EPS) * lng_ref[...] + lnb_ref[...]
            co_ref[rows, :] = (yn * _sigmoid(yn)).astype(BF16)
            return carry

        lax.fori_loop(0, seq_len // CONV_ROWS, chunk, 0)

    return kernel


def _conv_branches(proj, sc_conv_l, cf_conv_l, cf_b, ln_g, ln_b, *, latent):
    if latent:
        n_seq, seq_len, row0 = DEC_BATCH, DEC_SEQ, N_CTX
    else:
        n_seq, seq_len, row0 = BATCH, SEQ, 0
    rb = row0 // seq_len
    c0 = OFF_SC // SC_WIDTH

    def col(cidx):
        return pl.BlockSpec((seq_len, SC_WIDTH), lambda b: (rb + b, cidx))

    def whole(shape):
        return pl.BlockSpec(shape, lambda b: (0, 0))

    out_spec = pl.BlockSpec((seq_len, SC_WIDTH), lambda b: (b, 0))
    out_sds = jax.ShapeDtypeStruct((n_seq * seq_len, SC_WIDTH), BF16)
    t_shape = (3, seq_len + 2 * SUBLANES, SC_WIDTH)
    u_shape = (SUBLANES, seq_len + 4 * SUBLANES, CF_WIDTH)
    return pl.pallas_call(
        _make_conv_kernel(seq_len),
        grid=(n_seq,),
        in_specs=[col(c0), col(c0 + 1), col(c0 + 2), col(c0 + 3), col(c0 + 4),
                  whole((SC_K, SC_WIDTH)), whole((CF_K, CF_WIDTH)),
                  whole((1, CF_WIDTH)), whole((1, CF_WIDTH)), whole((1, CF_WIDTH))],
        out_specs=[out_spec, out_spec],
        out_shape=[out_sds, out_sds],
        scratch_shapes=[pltpu.VMEM(t_shape, F32), pltpu.VMEM(u_shape, F32)],
        compiler_params=pltpu.CompilerParams(
            dimension_semantics=("parallel",),
            vmem_limit_bytes=_vmem_limit(
                [((seq_len, SC_WIDTH), F32)] * 5 + [((seq_len, SC_WIDTH), BF16)] * 2,
                scratch=[(t_shape, F32), (u_shape, F32)],
                temps=[((seq_len, SC_WIDTH), F32)] * 4)),
        name="conv_branches_latent" if latent else "conv_branches_context",
    )(proj, proj, proj, proj, proj, sc_conv_l, cf_conv_l, cf_b, ln_g, ln_b)


def _merge_kernel(a_ref, b_ref, c_ref, ga_ref, gb_ref, gc_ref, bga_ref, bgb_ref, bgc_ref,
                  wa_ref, wb_ref, wc_ref, o_ref):
    br_a = jnp.dot(a_ref[...], wa_ref[...], preferred_element_type=F32)
    br_b = jnp.dot(b_ref[...], wb_ref[...], preferred_element_type=F32)
    br_c = jnp.dot(c_ref[...], wc_ref[...], preferred_element_type=F32)
    merged = (_sigmoid(ga_ref[...] + bga_ref[...]) * br_a
              + _sigmoid(gb_ref[...] + bgb_ref[...]) * br_b
              + _sigmoid(gc_ref[...] + bgc_ref[...]) * br_c)
    o_ref[...] = merged.astype(BF16)


def _gated_merge(attn, b_pre, c_pre, proj, b_gate_l, w_da, w_sc, w_cf):
    g0 = OFF_GATE // TN
    gstep = D_MODEL // TN

    def gate(k):
        return pl.BlockSpec((TM, TN), lambda i, j: (i, g0 + k * gstep + j))

    def gbias(k):
        return pl.BlockSpec((1, TN), lambda i, j: (0, k * gstep + j))

    def wcol(rows):
        return pl.BlockSpec((rows, TN), lambda i, j: (0, j))

    return pl.pallas_call(
        _merge_kernel,
        grid=(N_TOK // TM, D_MODEL // TN),
        in_specs=[
            pl.BlockSpec((TM, DA_WIDTH), lambda i, j: (i, 0)),
            pl.BlockSpec((TM, SC_WIDTH), lambda i, j: (i, 0)),
            pl.BlockSpec((TM, CF_WIDTH), lambda i, j: (i, 0)),
            gate(0), gate(1), gate(2), gbias(0), gbias(1), gbias(2),
            wcol(DA_WIDTH), wcol(SC_WIDTH), wcol(CF_WIDTH),
        ],
        out_specs=pl.BlockSpec((TM, TN), lambda i, j: (i, j)),
        out_shape=jax.ShapeDtypeStruct((N_TOK, D_MODEL), BF16),
        compiler_params=pltpu.CompilerParams(
            dimension_semantics=("parallel", "parallel"),
            vmem_limit_bytes=_vmem_limit(
                [((TM, DA_WIDTH), BF16), ((TM, SC_WIDTH), BF16), ((TM, CF_WIDTH), BF16)]
                + [((TM, TN), F32)] * 3 + [((DA_WIDTH, TN), BF16), ((SC_WIDTH, TN), BF16),
                                           ((CF_WIDTH, TN), BF16), ((TM, TN), BF16)],
                temps=[((TM, TN), F32)] * 8)),
        name="gated_merge",
    )(attn, b_pre, c_pre, proj, proj, proj, b_gate_l, b_gate_l, b_gate_l, w_da, w_sc, w_cf)


def _outproj_kernel(m_ref, w_ref, x_ref, gate_ref, o_ref):
    o_ref[...] = x_ref[...] + gate_ref[...] * jnp.dot(m_ref[...], w_ref[...], preferred_element_type=F32)


def _out_projection(merged, w_out, x, mod4, layer):
    gate_spec = pl.BlockSpec((None, None, 1, TN),
                             lambda i, j: (layer, _mod_group(i, TM), 0, 2 * (D_MODEL // TN) + j))
    return pl.pallas_call(
        _outproj_kernel,
        grid=(N_TOK // TM, D_MODEL // TN),
        in_specs=[
            pl.BlockSpec((TM, D_MODEL), lambda i, j: (i, 0)),
            pl.BlockSpec((D_MODEL, TN), lambda i, j: (0, j)),
            pl.BlockSpec((TM, TN), lambda i, j: (i, j)),
            gate_spec,
        ],
        out_specs=pl.BlockSpec((TM, TN), lambda i, j: (i, j)),
        out_shape=jax.ShapeDtypeStruct((N_TOK, D_MODEL), F32),
        compiler_params=pltpu.CompilerParams(
            dimension_semantics=("parallel", "parallel"),
            vmem_limit_bytes=_vmem_limit(
                [((TM, D_MODEL), BF16), ((D_MODEL, TN), BF16), ((TM, TN), F32), ((TM, TN), F32)],
                temps=[((TM, TN), F32)] * 2)),
        name="out_projection",
    )(merged, w_out, x, mod4)


def _make_ffn_kernel(final_norm):
    def kernel(*refs):
        x_ref, g_ref, shift_ref, scale_ref, gate_ref, wu_ref, ww_ref, wo_ref = refs[:8]
        refs = refs[8:]
        if final_norm:
            gf_ref = refs[0]
            refs = refs[1:]
        o_ref, h_scr = refs
        j = pl.program_id(1)

        @pl.when(j == 0)
        def _():
            h_scr[...] = _mod_norm(x_ref[...], g_ref[...], shift_ref[...], scale_ref[...]).astype(BF16)
            o_ref[...] = jnp.zeros_like(o_ref)

        h = h_scr[...]
        u = jnp.dot(h, wu_ref[...], preferred_element_type=F32)
        w = jnp.dot(h, ww_ref[...], preferred_element_type=F32)
        act = (u * _sigmoid(u) * w).astype(BF16)
        o_ref[...] += jnp.dot(act, wo_ref[...], preferred_element_type=F32)

        @pl.when(j == pl.num_programs(1) - 1)
        def _():
            y = x_ref[...] + gate_ref[...] * o_ref[...]
            if final_norm:
                y = _rms(y, gf_ref[...])
            o_ref[...] = y

    return kernel


def _swiglu(x, g, mod4, layer, w_ffn_in, w_ffn_out, g_final=None):
    final_norm = g_final is not None
    nh = FF_HIDDEN // TH_FFN
    row = pl.BlockSpec((1, D_MODEL), lambda i, j: (0, 0))
    in_specs = [
        pl.BlockSpec((TM_FFN, D_MODEL), lambda i, j: (i, 0)),
        row,
        _mod_spec(layer, 3, TM_FFN), _mod_spec(layer, 4, TM_FFN), _mod_spec(layer, 5, TM_FFN),
        pl.BlockSpec((D_MODEL, TH_FFN), lambda i, j: (0, j)),
        pl.BlockSpec((D_MODEL, TH_FFN), lambda i, j: (0, nh + j)),
        pl.BlockSpec((TH_FFN, D_MODEL), lambda i, j: (j, 0)),
    ]
    args = [x, g, mod4, mod4, mod4, w_ffn_in, w_ffn_in, w_ffn_out]
    if final_norm:
        in_specs.append(row)
        args.append(g_final)
    return pl.pallas_call(
        _make_ffn_kernel(final_norm),
        grid=(N_TOK // TM_FFN, nh),
        in_specs=in_specs,
        out_specs=pl.BlockSpec((TM_FFN, D_MODEL), lambda i, j: (i, 0)),
        out_shape=jax.ShapeDtypeStruct((N_TOK, D_MODEL), F32),
        scratch_shapes=[pltpu.VMEM((TM_FFN, D_MODEL), BF16)],
        compiler_params=pltpu.CompilerParams(
            dimension_semantics=("parallel", "arbitrary"),
            vmem_limit_bytes=_vmem_limit(
                [((TM_FFN, D_MODEL), F32)] * 2 + [((D_MODEL, TH_FFN), BF16)] * 2
                + [((TH_FFN, D_MODEL), BF16)],
                scratch=[((TM_FFN, D_MODEL), BF16)],
                temps=[((TM_FFN, TH_FFN), F32)] * 4 + [((TM_FFN, D_MODEL), F32)] * 2)),
        name="swiglu_final" if final_norm else "swiglu",
    )(*args)


def _rope_tables():
    rows = DEC_SEQ // GRID_W
    row_ids = jnp.repeat(jnp.arange(rows), GRID_W).astype(F32)
    col_ids = jnp.tile(jnp.arange(GRID_W), rows).astype(F32)
    n_freq = DA_HEAD_DIM // 4
    inv = ROPE_THETA ** (-jnp.arange(n_freq, dtype=F32) / n_freq)
    ang_r, ang_c = row_ids[:, None] * inv, col_ids[:, None] * inv
    zero = jnp.zeros_like(ang_r)
    cos = jnp.concatenate([jnp.cos(ang_r)] * 2 + [jnp.cos(ang_c)] * 2, axis=-1)
    sin_lo = jnp.concatenate([-jnp.sin(ang_r), zero, -jnp.sin(ang_c), zero], axis=-1)
    sin_hi = jnp.concatenate([zero, jnp.sin(ang_r), zero, jnp.sin(ang_c)], axis=-1)
    return tuple(jnp.tile(t, (1, LANES // DA_HEAD_DIM)) for t in (cos, sin_lo, sin_hi))


def kernel(x_prompt, x_sample, cache_k, cache_v, c, c_ctx, w_mod, b_mod, g_norm1, w_in, da_lambda, da_subln,
           w_da_out, sc_conv, w_sc_out, cf_conv, cf_conv_b, cf_ln_g, cf_ln_b, w_cf_out, b_gate, w_out,
           g_norm2, w_ffn_in, w_ffn_out, g_final):
    x = jnp.concatenate([x_prompt.reshape(N_CTX, D_MODEL), x_sample.reshape(N_LAT, D_MODEL)], axis=0)
    cvec = jnp.concatenate([c_ctx[None, :], c, jnp.zeros((MOD_ROWS - 1 - DEC_BATCH, D_MODEL), F32)], axis=0)
    mod4 = _modulation(cvec, w_mod, b_mod).reshape(DEPTH, MOD_ROWS, 1, 6 * D_MODEL)
    rope = _rope_tables()
    cache_k = cache_k.reshape(DEC_BATCH, DEPTH, PAST_LEN, DA_WIDTH)
    cache_v = cache_v.reshape(DEC_BATCH, DEPTH, PAST_LEN, DA_HEADS * DA_VDIM)

    new_k, new_v = [], []
    for l in range(DEPTH):
        proj = _in_projection(x, g_norm1[l][None, :], mod4, l, w_in[l].astype(BF16))
        new_k.append(proj[:N_CTX, OFF_K:OFF_V].reshape(BATCH, SEQ, DA_HEADS, 2 * DA_HEAD_DIM))
        new_v.append(proj[:N_CTX, OFF_V:OFF_SC].reshape(BATCH, SEQ, DA_HEADS, DA_VDIM))

        subln = da_subln[l][None, :]
        attn = jnp.concatenate([
            _diff_attention(proj, da_lambda[l], subln, l, latent=False),
            _diff_attention(proj, da_lambda[l], subln, l, latent=True, cache=(cache_k, cache_v), rope=rope),
        ], axis=0)
        conv_args = (sc_conv[l], cf_conv[l], cf_conv_b[l][None, :], cf_ln_g[l][None, :], cf_ln_b[l][None, :])
        b_ctx, c_ctx_pre = _conv_branches(proj, *conv_args, latent=False)
        b_lat, c_lat = _conv_branches(proj, *conv_args, latent=True)
        b_pre = jnp.concatenate([b_ctx, b_lat], axis=0)
        c_pre = jnp.concatenate([c_ctx_pre, c_lat], axis=0)

        merged = _gated_merge(attn, b_pre, c_pre, proj, b_gate[l][None, :],
                              w_da_out[l].astype(BF16), w_sc_out[l].astype(BF16), w_cf_out[l].astype(BF16))
        x = _out_projection(merged, w_out[l].astype(BF16), x, mod4, l)
        x = _swiglu(x, g_norm2[l][None, :], mod4, l, w_ffn_in[l].astype(BF16), w_ffn_out[l].astype(BF16),
                    g_final[None, :] if l == DEPTH - 1 else None)

    y_prompt = x[:N_CTX].reshape(BATCH, SEQ, D_MODEL)
    y_sample = x[N_CTX:].reshape(DEC_BATCH, DEC_SEQ, D_MODEL)
    return (y_prompt, y_sample, jnp.stack(new_k, axis=1), jnp.stack(new_v, axis=1))
```

```python
import functools
import math

import jax
import jax.numpy as jnp
from jax import lax
from jax.experimental import pallas as pl
from jax.experimental.pallas import tpu as pltpu

D_MODEL = 2048
BATCH = 16
SEQ = 256
DEPTH = 2
DEC_BATCH = 8
DEC_SEQ = 1024
PAST_LEN = 256
GRID_W = 64
DA_HEADS = 8
DA_HEAD_DIM = 64
DA_VDIM = 2 * DA_HEAD_DIM
DA_WIDTH = DA_HEADS * 2 * DA_HEAD_DIM
SC_WIDTH = 512
SC_K = 3
CF_WIDTH = 512
CF_K = 31
N_BRANCH = 3
FF_HIDDEN = -(-8 * D_MODEL // (3 * 256)) * 256
ROPE_THETA = 10000.0
EPS = 1e-6

OFF_K = DA_WIDTH
OFF_V = 2 * DA_WIDTH
OFF_SC = 3 * DA_WIDTH
OFF_CF = OFF_SC + 3 * SC_WIDTH
OFF_GATE = OFF_CF + 2 * CF_WIDTH
IN_COLS = OFF_GATE + N_BRANCH * D_MODEL

N_CTX = BATCH * SEQ
N_LAT = DEC_BATCH * DEC_SEQ
N_TOK = N_CTX + N_LAT
MOD_ROWS = 16

F32 = jnp.float32
BF16 = jnp.bfloat16

VMEM_BUDGET_V7X = 56 * 1024 * 1024
LANES = 128
SUBLANES = 8

TM_NORM = 512
TM_IN = 2048
TM = 1024
TN = 512
TM_FFN = 512
TH_FFN = 512
TQ_LAT = 1024
LOG2_E = math.log2(math.e)
CONV_ROWS = 32


def _nbytes(shape, dtype):
    return math.prod(shape) * jnp.dtype(dtype).itemsize


def _vmem_limit(blocks, scratch=(), temps=()):
    total = 2 * sum(_nbytes(s, d) for s, d in blocks)
    total += sum(_nbytes(s, d) for s, d in scratch)
    total += sum(_nbytes(s, d) for s, d in temps)
    return min(total, VMEM_BUDGET_V7X)


def _mod_group(i, tm):
    return jnp.maximum(i * tm - N_CTX + DEC_SEQ, 0) // DEC_SEQ


def _rms(x, g):
    return x * lax.rsqrt(jnp.mean(x * x, axis=-1, keepdims=True) + EPS) * g


def _mod_norm(x, g, shift, scale):
    return _rms(x, g) * (1.0 + scale) + shift


def _sigmoid(x):
    return 0.5 * jnp.tanh(0.5 * x) + 0.5


def _mod_kernel(c_ref, w_ref, b_ref, o_ref):
    c = c_ref[...]
    s = (c * _sigmoid(c)).astype(BF16)
    o_ref[...] = jnp.dot(s, w_ref[...].astype(BF16), preferred_element_type=F32) + b_ref[...]


def _modulation(cvec, w_mod, b_mod):
    bn = 1024
    return pl.pallas_call(
        _mod_kernel,
        grid=(DEPTH, 6 * D_MODEL // bn),
        in_specs=[
            pl.BlockSpec((MOD_ROWS, D_MODEL), lambda l, j: (0, 0)),
            pl.BlockSpec((None, D_MODEL, bn), lambda l, j: (l, 0, j)),
            pl.BlockSpec((None, 1, bn), lambda l, j: (l, 0, j)),
        ],
        out_specs=pl.BlockSpec((None, MOD_ROWS, bn), lambda l, j: (l, 0, j)),
        out_shape=jax.ShapeDtypeStruct((DEPTH, MOD_ROWS, 6 * D_MODEL), F32),
        compiler_params=pltpu.CompilerParams(
            dimension_semantics=("parallel", "parallel"),
            vmem_limit_bytes=_vmem_limit(
                [((D_MODEL, bn), F32), ((MOD_ROWS, D_MODEL), F32), ((MOD_ROWS, bn), F32)],
                temps=[((D_MODEL, bn), BF16), ((D_MODEL, bn), F32)])),
        name="modulation",
    )(cvec, w_mod, b_mod.reshape(DEPTH, 1, 6 * D_MODEL))


def _mod_spec(layer, chunk, tm):
    return pl.BlockSpec((None, None, 1, D_MODEL),
                        lambda i, *_: (layer, _mod_group(i, tm), 0, chunk))


def _prenorm_kernel(x_ref, g_ref, shift_ref, scale_ref, o_ref):
    o_ref[...] = _mod_norm(x_ref[...], g_ref[...], shift_ref[...], scale_ref[...]).astype(BF16)


def _pre_norm(x, g, mod4, layer):
    return pl.pallas_call(
        _prenorm_kernel,
        grid=(N_TOK // TM_NORM,),
        in_specs=[
            pl.BlockSpec((TM_NORM, D_MODEL), lambda i: (i, 0)),
            pl.BlockSpec((1, D_MODEL), lambda i: (0, 0)),
            _mod_spec(layer, 0, TM_NORM),
            _mod_spec(layer, 1, TM_NORM),
        ],
        out_specs=pl.BlockSpec((TM_NORM, D_MODEL), lambda i: (i, 0)),
        out_shape=jax.ShapeDtypeStruct((N_TOK, D_MODEL), BF16),
        compiler_params=pltpu.CompilerParams(
            dimension_semantics=("parallel",),
            vmem_limit_bytes=_vmem_limit(
                [((TM_NORM, D_MODEL), F32), ((TM_NORM, D_MODEL), BF16)],
                temps=[((TM_NORM, D_MODEL), F32)] * 3)),
        name="pre_norm",
    )(x, g, mod4, mod4)


N_CTX_TILES = N_CTX // TM_IN
KV_TILES = DA_WIDTH // TN


def _inproj_kernel(h_ref, w_ref, *rest):
    o_ref, nk_ref, nv_ref = rest[-3:]
    i, j = pl.program_id(0), pl.program_id(1)
    o_ref[...] = jnp.dot(h_ref[...], w_ref[...], preferred_element_type=F32)

    @pl.when((i < N_CTX_TILES) & (j >= OFF_K // TN) & (j < OFF_V // TN))
    def _():
        nk_ref[...] = o_ref[...].reshape(nk_ref.shape)

    @pl.when((i < N_CTX_TILES) & (j >= OFF_V // TN) & (j < OFF_SC // TN))
    def _():
        nv_ref[...] = o_ref[...].reshape(nv_ref.shape)


def _kv_cache_spec(layer, col0):
    def index(i, j):
        col = jnp.where(i < N_CTX_TILES, jnp.clip(j - col0, 0, KV_TILES - 1), KV_TILES - 1)
        return (jnp.minimum(i, N_CTX_TILES - 1), layer, 0, col)
    return pl.BlockSpec((TM_IN // SEQ, None, SEQ, TN), index)


def _in_projection(h, layer, w_in, new_kv=None):
    in_specs = [
        pl.BlockSpec((TM_IN, D_MODEL), lambda i, j: (i, 0)),
        pl.BlockSpec((None, D_MODEL, TN), lambda i, j: (layer, 0, j)),
    ]
    args = [h, w_in]
    aliases = {}
    if new_kv is not None:
        aliases = {len(args): 1, len(args) + 1: 2}
        in_specs += [pl.BlockSpec(memory_space=pl.ANY)] * 2
        args += list(new_kv)
    kv_sds = jax.ShapeDtypeStruct((BATCH, DEPTH, SEQ, DA_WIDTH), F32)
    kv_block = ((TM_IN // SEQ, SEQ, TN), F32)
    return pl.pallas_call(
        _inproj_kernel,
        grid=(N_TOK // TM_IN, IN_COLS // TN),
        in_specs=in_specs,
        out_specs=[pl.BlockSpec((TM_IN, TN), lambda i, j: (i, j)),
                   _kv_cache_spec(layer, OFF_K // TN), _kv_cache_spec(layer, OFF_V // TN)],
        out_shape=[jax.ShapeDtypeStruct((N_TOK, IN_COLS), F32), kv_sds, kv_sds],
        input_output_aliases=aliases,
        compiler_params=pltpu.CompilerParams(
            dimension_semantics=("arbitrary", "arbitrary"),
            vmem_limit_bytes=_vmem_limit(
                [((TM_IN, D_MODEL), BF16), ((D_MODEL, TN), BF16), ((TM_IN, TN), F32), kv_block, kv_block],
                temps=[((TM_IN, TN), F32)])),
        name="in_projection",
    )(*args)


def _rope(x, cos, sin_lo, sin_hi):
    return (x * cos + pltpu.roll(x, LANES - 16, 1) * sin_lo + pltpu.roll(x, 16, 1) * sin_hi)


def _make_attn_kernel(seq_len, tq, n_cache, use_rope, lam_init, aliased):
    def kernel(*refs):
        dl_ref, g_ref, q_ref, k_ref, v_ref = refs[:5]
        refs = refs[5:]
        if n_cache:
            ck_ref, cv_ref = refs[:2]
            refs = refs[2:]
        if use_rope:
            cos_ref, slo_ref, shi_ref = refs[:3]
            refs = refs[3:]
        if aliased:
            refs = refs[1:]
        o_ref, kk_scr, vv_scr = refs
        qb = pl.program_id(2)

        @pl.when(qb == 0)
        def _():
            k = k_ref[...]
            if use_rope:
                k = _rope(k, cos_ref[...], slo_ref[...], shi_ref[...])
            if n_cache:
                kk_scr[0:n_cache, :] = ck_ref[...].astype(BF16)
                vv_scr[0:n_cache, 0:DA_VDIM] = cv_ref[...].astype(BF16)
            kk_scr[n_cache:n_cache + seq_len, :] = k.astype(BF16)
            vv_scr[n_cache:n_cache + seq_len, 0:DA_VDIM] = v_ref[...].astype(BF16)
            vv_scr[:, DA_VDIM:] = jnp.ones((n_cache + seq_len, DA_VDIM), BF16)

        q = q_ref[...]
        if use_rope:
            rows = pl.ds(pl.multiple_of(qb * tq, tq), tq)
            q = _rope(q, cos_ref[rows, :], slo_ref[rows, :], shi_ref[rows, :])
        q = q * (DA_HEAD_DIM ** -0.5 * LOG2_E)
        lane = lax.broadcasted_iota(jnp.int32, q.shape, 1)
        kk = kk_scr[...]
        vv = vv_scr[...]

        def softmax_times_v(qm):
            s = lax.dot_general(qm.astype(BF16), kk, (((1,), (1,)), ((), ())), preferred_element_type=F32)
            e = jnp.exp2(s - jnp.max(s, axis=-1, keepdims=True)).astype(BF16)
            ev = jnp.dot(e, vv, preferred_element_type=F32)
            return ev[:, :DA_VDIM] / ev[:, DA_VDIM:]

        o1 = softmax_times_v(jnp.where(lane < DA_HEAD_DIM, q, 0.0))
        o2 = softmax_times_v(jnp.where(lane >= DA_HEAD_DIM, q, 0.0))
        dl = dl_ref[...]
        lam = (jnp.exp(jnp.sum(dl[0:1] * dl[1:2], axis=-1, keepdims=True))
               - jnp.exp(jnp.sum(dl[2:3] * dl[3:4], axis=-1, keepdims=True)) + lam_init)
        o_ref[...] = (_rms(o1 - lam * o2, g_ref[...]) * (1.0 - lam_init)).astype(BF16)

    return kernel


def _diff_attention(proj, da_lambda_l, subln_g, layer, *, latent, cache=None, rope=None, out_buf=None):
    lam_init = 0.8 - 0.6 * math.exp(-0.3 * layer)
    if latent:
        n_seq, seq_len, tq, n_cache, row0 = DEC_BATCH, DEC_SEQ, TQ_LAT, PAST_LEN, N_CTX
    else:
        n_seq, seq_len, tq, n_cache, row0 = BATCH, SEQ, SEQ, 0, 0
    nq = seq_len // tq
    lk = n_cache + seq_len
    hk, hv = OFF_K // LANES, OFF_V // LANES

    in_specs = [
        pl.BlockSpec((4, DA_HEAD_DIM), lambda b, h, t: (0, 0)),
        pl.BlockSpec((1, DA_VDIM), lambda b, h, t: (0, 0)),
        pl.BlockSpec((tq, LANES), lambda b, h, t: (row0 // tq + b * nq + t, h)),
        pl.BlockSpec((seq_len, LANES), lambda b, h, t: (row0 // seq_len + b, hk + h)),
        pl.BlockSpec((seq_len, LANES), lambda b, h, t: (row0 // seq_len + b, hv + h)),
    ]
    args = [da_lambda_l, subln_g, proj, proj, proj]
    blocks = [((tq, LANES), F32), ((seq_len, LANES), F32), ((seq_len, LANES), F32), ((tq, LANES), BF16)]
    if latent:
        cache_k, cache_v = cache
        in_specs += [pl.BlockSpec((None, None, n_cache, LANES), lambda b, h, t: (b, layer, 0, h))] * 2
        args += [cache_k, cache_v]
        in_specs += [pl.BlockSpec((seq_len, LANES), lambda b, h, t: (0, 0))] * 3
        args += list(rope)
        blocks += [((n_cache, LANES), F32)] * 2 + [((seq_len, LANES), F32)] * 3
    aliases = {}
    if out_buf is not None:
        aliases = {len(args): 0}
        in_specs.append(pl.BlockSpec(memory_space=pl.ANY))
        args.append(out_buf)

    scratch = [((lk, LANES), BF16), ((lk, 2 * DA_VDIM), BF16)]
    return pl.pallas_call(
        _make_attn_kernel(seq_len, tq, n_cache, latent, lam_init, out_buf is not None),
        grid=(n_seq, DA_HEADS, nq),
        in_specs=in_specs,
        out_specs=pl.BlockSpec((tq, LANES), lambda b, h, t: (row0 // tq + b * nq + t, h)),
        out_shape=jax.ShapeDtypeStruct((N_TOK, DA_WIDTH), BF16),
        scratch_shapes=[pltpu.VMEM(s, d) for s, d in scratch],
        input_output_aliases=aliases,
        compiler_params=pltpu.CompilerParams(
            dimension_semantics=("parallel", "parallel", "arbitrary"),
            vmem_limit_bytes=_vmem_limit(
                blocks, scratch=scratch,
                temps=[((tq, lk), F32)] * 6 + [((tq, 2 * DA_VDIM), F32)] * 4)),
        name="diff_attention_latent" if latent else "diff_attention_context",
    )(*args)


def _fill_shifted(scr, value, pad, seq_len, shifts):
    width = value.shape[-1]
    n = seq_len + 2 * pad - SUBLANES
    scr[0, 0:pad, :] = jnp.zeros((pad, width), F32)
    scr[0, pad + seq_len:, :] = jnp.zeros((pad, width), F32)
    scr[0, pad:pad + seq_len, :] = value
    for s, shift in enumerate(shifts, start=1):
        scr[s, 0:n, :] = scr[0, shift:shift + n, :]


def _tap(scr, shifts, r0, offset):
    aligned, rem = offset - offset % SUBLANES, offset % SUBLANES
    slot = 0 if rem == 0 else 1 + shifts.index(rem)
    return scr[slot, pl.ds(r0 + aligned, CONV_ROWS), :]


def _make_conv_kernel(seq_len):
    pad3, pad31 = SUBLANES, 2 * SUBLANES
    shifts3 = (1, SUBLANES - 1)
    shifts31 = tuple(range(1, SUBLANES))

    def kernel(gb_ref, gc_ref, sx_ref, ca_ref, cb_ref, w3_ref, w31_ref, b31_ref, lng_ref, lnb_ref, *rest):
        bo_ref, co_ref, t_scr, u_scr, y_scr = rest[-5:]
        _fill_shifted(t_scr, gc_ref[...] * sx_ref[...], pad3, seq_len, shifts3)
        _fill_shifted(u_scr, ca_ref[...] * _sigmoid(cb_ref[...]), pad31, seq_len, shifts31)

        def chunk(c, carry):
            r0 = pl.multiple_of(c * CONV_ROWS, CONV_ROWS)
            rows = pl.ds(r0, CONV_ROWS)
            y = _tap(t_scr, shifts3, r0, pad3 - SC_K // 2) * w3_ref[0:1, :]
            for k in range(1, SC_K):
                y = y + _tap(t_scr, shifts3, r0, pad3 - SC_K // 2 + k) * w3_ref[k:k + 1, :]
            bo_ref[rows, :] = (gb_ref[rows, :] * y).astype(BF16)

            acc = _tap(u_scr, shifts31, r0, pad31 - CF_K // 2) * w31_ref[0:1, :]
            for k in range(1, CF_K):
                acc = acc + _tap(u_scr, shifts31, r0, pad31 - CF_K // 2 + k) * w31_ref[k:k + 1, :]
            y_scr[rows, :] = acc + b31_ref[...]
            return carry

        lax.fori_loop(0, seq_len // CONV_ROWS, chunk, 0)

        y = y_scr[...]
        xc = y - jnp.mean(y, axis=-1, keepdims=True)
        yn = xc * lax.rsqrt(jnp.mean(xc * xc, axis=-1, keepdims=True) + EPS) * lng_ref[...] + lnb_ref[...]
        co_ref[...] = (yn * _sigmoid(yn)).astype(BF16)

    return kernel


def _conv_branches(proj, sc_conv_l, cf_conv_l, cf_b, ln_g, ln_b, *, latent, out_bufs=None):
    if latent:
        n_seq, seq_len, row0 = DEC_BATCH, DEC_SEQ, N_CTX
    else:
        n_seq, seq_len, row0 = BATCH, SEQ, 0
    rb = row0 // seq_len
    c0 = OFF_SC // SC_WIDTH

    def col(cidx):
        return pl.BlockSpec((seq_len, SC_WIDTH), lambda b: (rb + b, cidx))

    def whole(shape):
        return pl.BlockSpec(shape, lambda b: (0, 0))

    in_specs = [col(c0), col(c0 + 1), col(c0 + 2), col(c0 + 3), col(c0 + 4),
                whole((SC_K, SC_WIDTH)), whole((CF_K, CF_WIDTH)),
                whole((1, CF_WIDTH)), whole((1, CF_WIDTH)), whole((1, CF_WIDTH))]
    args = [proj, proj, proj, proj, proj, sc_conv_l, cf_conv_l, cf_b, ln_g, ln_b]
    aliases = {}
    if out_bufs is not None:
        aliases = {len(args): 0, len(args) + 1: 1}
        in_specs += [pl.BlockSpec(memory_space=pl.ANY)] * 2
        args += list(out_bufs)

    out_spec = pl.BlockSpec((seq_len, SC_WIDTH), lambda b: (rb + b, 0))
    out_sds = jax.ShapeDtypeStruct((N_TOK, SC_WIDTH), BF16)
    scratch = [((3, seq_len + 2 * SUBLANES, SC_WIDTH), F32),
               ((SUBLANES, seq_len + 4 * SUBLANES, CF_WIDTH), F32),
               ((seq_len, CF_WIDTH), F32)]
    return pl.pallas_call(
        _make_conv_kernel(seq_len),
        grid=(n_seq,),
        in_specs=in_specs,
        out_specs=[out_spec, out_spec],
        out_shape=[out_sds, out_sds],
        scratch_shapes=[pltpu.VMEM(s, d) for s, d in scratch],
        input_output_aliases=aliases,
        compiler_params=pltpu.CompilerParams(
            dimension_semantics=("parallel",),
            vmem_limit_bytes=_vmem_limit(
                [((seq_len, SC_WIDTH), F32)] * 5 + [((seq_len, SC_WIDTH), BF16)] * 2,
                scratch=scratch,
                temps=[((seq_len, SC_WIDTH), F32)] * 4)),
        name="conv_branches_latent" if latent else "conv_branches_context",
    )(*args)


def _merge_kernel(a_ref, b_ref, c_ref, ga_ref, gb_ref, gc_ref, bga_ref, bgb_ref, bgc_ref,
                  wa_ref, wb_ref, wc_ref, o_ref):
    br_a = jnp.dot(a_ref[...], wa_ref[...].astype(BF16), preferred_element_type=F32)
    br_b = jnp.dot(b_ref[...], wb_ref[...].astype(BF16), preferred_element_type=F32)
    br_c = jnp.dot(c_ref[...], wc_ref[...].astype(BF16), preferred_element_type=F32)
    merged = (_sigmoid(ga_ref[...] + bga_ref[...]) * br_a
              + _sigmoid(gb_ref[...] + bgb_ref[...]) * br_b
              + _sigmoid(gc_ref[...] + bgc_ref[...]) * br_c)
    o_ref[...] = merged.astype(BF16)


def _gated_merge(attn, b_pre, c_pre, proj, b_gate_l, layer, w_da, w_sc, w_cf):
    g0 = OFF_GATE // TN
    gstep = D_MODEL // TN

    def gate(k):
        return pl.BlockSpec((TM, TN), lambda i, j: (i, g0 + k * gstep + j))

    def gbias(k):
        return pl.BlockSpec((1, TN), lambda i, j: (0, k * gstep + j))

    def wcol(rows):
        return pl.BlockSpec((None, rows, TN), lambda i, j: (layer, 0, j))

    return pl.pallas_call(
        _merge_kernel,
        grid=(N_TOK // TM, D_MODEL // TN),
        in_specs=[
            pl.BlockSpec((TM, DA_WIDTH), lambda i, j: (i, 0)),
            pl.BlockSpec((TM, SC_WIDTH), lambda i, j: (i, 0)),
            pl.BlockSpec((TM, CF_WIDTH), lambda i, j: (i, 0)),
            gate(0), gate(1), gate(2), gbias(0), gbias(1), gbias(2),
            wcol(DA_WIDTH), wcol(SC_WIDTH), wcol(CF_WIDTH),
        ],
        out_specs=pl.BlockSpec((TM, TN), lambda i, j: (i, j)),
        out_shape=jax.ShapeDtypeStruct((N_TOK, D_MODEL), BF16),
        compiler_params=pltpu.CompilerParams(
            dimension_semantics=("parallel", "parallel"),
            vmem_limit_bytes=_vmem_limit(
                [((TM, DA_WIDTH), BF16), ((TM, SC_WIDTH), BF16), ((TM, CF_WIDTH), BF16)]
                + [((TM, TN), F32)] * 3 + [((DA_WIDTH, TN), F32), ((SC_WIDTH, TN), F32),
                                           ((CF_WIDTH, TN), F32), ((TM, TN), BF16)],
                temps=[((TM, TN), F32)] * 8 + [((D_MODEL, TN), BF16)])),
        name="gated_merge",
    )(attn, b_pre, c_pre, proj, proj, proj, b_gate_l, b_gate_l, b_gate_l, w_da, w_sc, w_cf)


def _outproj_kernel(m_ref, w_ref, x_ref, gate_ref, o_ref):
    o_ref[...] = x_ref[...] + gate_ref[...] * jnp.dot(m_ref[...], w_ref[...].astype(BF16),
                                                      preferred_element_type=F32)


def _out_projection(merged, w_out, x, mod4, layer):
    gate_spec = pl.BlockSpec((None, None, 1, TN),
                             lambda i, j: (layer, _mod_group(i, TM), 0, 2 * (D_MODEL // TN) + j))
    return pl.pallas_call(
        _outproj_kernel,
        grid=(N_TOK // TM, D_MODEL // TN),
        in_specs=[
            pl.BlockSpec((TM, D_MODEL), lambda i, j: (i, 0)),
            pl.BlockSpec((None, D_MODEL, TN), lambda i, j: (layer, 0, j)),
            pl.BlockSpec((TM, TN), lambda i, j: (i, j)),
            gate_spec,
        ],
        out_specs=pl.BlockSpec((TM, TN), lambda i, j: (i, j)),
        out_shape=jax.ShapeDtypeStruct((N_TOK, D_MODEL), F32),
        compiler_params=pltpu.CompilerParams(
            dimension_semantics=("parallel", "parallel"),
            vmem_limit_bytes=_vmem_limit(
                [((TM, D_MODEL), BF16), ((D_MODEL, TN), F32), ((TM, TN), F32), ((TM, TN), F32)],
                temps=[((TM, TN), F32)] * 2 + [((D_MODEL, TN), BF16)])),
        name="out_projection",
    )(merged, w_out, x, mod4)


def _make_ffn_kernel(final_norm):
    def kernel(*refs):
        x_ref, g_ref, shift_ref, scale_ref, gate_ref, wu_ref, ww_ref, wo_ref = refs[:8]
        refs = refs[8:]
        if final_norm:
            gf_ref = refs[0]
            refs = refs[1:]
        out_refs, (h_scr, acc_scr) = refs[:-2], refs[-2:]
        i, j = pl.program_id(0), pl.program_id(1)

        @pl.when(j == 0)
        def _():
            h_scr[...] = _mod_norm(x_ref[...], g_ref[...], shift_ref[...], scale_ref[...]).astype(BF16)
            acc_scr[...] = jnp.zeros_like(acc_scr)

        h = h_scr[...]
        u = jnp.dot(h, wu_ref[...], preferred_element_type=F32)
        w = jnp.dot(h, ww_ref[...], preferred_element_type=F32)
        act = (u * _sigmoid(u) * w).astype(BF16)
        acc_scr[...] += jnp.dot(act, wo_ref[...], preferred_element_type=F32)

        @pl.when(j == pl.num_programs(1) - 1)
        def _():
            y = x_ref[...] + gate_ref[...] * acc_scr[...]
            if not final_norm:
                out_refs[0][...] = y
            else:
                y = _rms(y, gf_ref[...])
                y_ctx_ref, y_lat_ref = out_refs

                @pl.when(i < N_CTX // TM_FFN)
                def _():
                    y_ctx_ref[...] = y

                @pl.when(i >= N_CTX // TM_FFN)
                def _():
                    y_lat_ref[...] = y

    return kernel


def _swiglu(x, g, mod4, layer, w_ffn_in, w_ffn_out, g_final=None):
    final_norm = g_final is not None
    nh = FF_HIDDEN // TH_FFN
    n_ctx_tiles = N_CTX // TM_FFN
    row = pl.BlockSpec((1, D_MODEL), lambda i, j: (0, 0))
    in_specs = [
        pl.BlockSpec((TM_FFN, D_MODEL), lambda i, j: (i, 0)),
        row,
        _mod_spec(layer, 3, TM_FFN), _mod_spec(layer, 4, TM_FFN), _mod_spec(layer, 5, TM_FFN),
        pl.BlockSpec((None, D_MODEL, TH_FFN), lambda i, j: (layer, 0, j)),
        pl.BlockSpec((None, D_MODEL, TH_FFN), lambda i, j: (layer, 0, nh + j)),
        pl.BlockSpec((None, TH_FFN, D_MODEL), lambda i, j: (layer, j, 0)),
    ]
    args = [x, g, mod4, mod4, mod4, w_ffn_in, w_ffn_in, w_ffn_out]
    if final_norm:
        in_specs.append(row)
        args.append(g_final)
        out_specs = [pl.BlockSpec((TM_FFN, D_MODEL), lambda i, j: (jnp.minimum(i, n_ctx_tiles - 1), 0)),
                     pl.BlockSpec((TM_FFN, D_MODEL), lambda i, j: (jnp.maximum(i - n_ctx_tiles, 0), 0))]
        out_shape = [jax.ShapeDtypeStruct((N_CTX, D_MODEL), F32), jax.ShapeDtypeStruct((N_LAT, D_MODEL), F32)]
    else:
        out_specs = [pl.BlockSpec((TM_FFN, D_MODEL), lambda i, j: (i, 0))]
        out_shape = [jax.ShapeDtypeStruct((N_TOK, D_MODEL), F32)]
    scratch = [((TM_FFN, D_MODEL), BF16), ((TM_FFN, D_MODEL), F32)]
    results = pl.pallas_call(
        _make_ffn_kernel(final_norm),
        grid=(N_TOK // TM_FFN, nh),
        in_specs=in_specs,
        out_specs=out_specs,
        out_shape=out_shape,
        scratch_shapes=[pltpu.VMEM(s, d) for s, d in scratch],
        compiler_params=pltpu.CompilerParams(
            dimension_semantics=("arbitrary", "arbitrary"),
            vmem_limit_bytes=_vmem_limit(
                [((TM_FFN, D_MODEL), F32)] * (1 + len(out_specs)) + [((D_MODEL, TH_FFN), BF16)] * 2
                + [((TH_FFN, D_MODEL), BF16)],
                scratch=scratch,
                temps=[((TM_FFN, TH_FFN), F32)] * 4 + [((TM_FFN, D_MODEL), F32)] * 2)),
        name="swiglu_final" if final_norm else "swiglu",
    )(*args)
    return results if final_norm else results[0]


def _rope_tables():
    rows = DEC_SEQ // GRID_W
    row_ids = jnp.repeat(jnp.arange(rows), GRID_W).astype(F32)
    col_ids = jnp.tile(jnp.arange(GRID_W), rows).astype(F32)
    n_freq = DA_HEAD_DIM // 4
    inv = ROPE_THETA ** (-jnp.arange(n_freq, dtype=F32) / n_freq)
    ang_r, ang_c = row_ids[:, None] * inv, col_ids[:, None] * inv
    zero = jnp.zeros_like(ang_r)
    cos = jnp.concatenate([jnp.cos(ang_r)] * 2 + [jnp.cos(ang_c)] * 2, axis=-1)
    sin_lo = jnp.concatenate([-jnp.sin(ang_r), zero, -jnp.sin(ang_c), zero], axis=-1)
    sin_hi = jnp.concatenate([zero, jnp.sin(ang_r), zero, jnp.sin(ang_c)], axis=-1)
    return tuple(jnp.tile(t, (1, LANES // DA_HEAD_DIM)) for t in (cos, sin_lo, sin_hi))


def kernel(x_prompt, x_sample, cache_k, cache_v, c, c_ctx, w_mod, b_mod, g_norm1, w_in, da_lambda, da_subln,
           w_da_out, sc_conv, w_sc_out, cf_conv, cf_conv_b, cf_ln_g, cf_ln_b, w_cf_out, b_gate, w_out,
           g_norm2, w_ffn_in, w_ffn_out, g_final):
    x = jnp.concatenate([x_prompt.reshape(N_CTX, D_MODEL), x_sample.reshape(N_LAT, D_MODEL)], axis=0)
    cvec = jnp.concatenate([c_ctx[None, :], c, jnp.zeros((MOD_ROWS - 1 - DEC_BATCH, D_MODEL), F32)], axis=0)
    mod4 = _modulation(cvec, w_mod, b_mod).reshape(DEPTH, MOD_ROWS, 1, 6 * D_MODEL)
    rope = _rope_tables()
    cache_k = cache_k.reshape(DEC_BATCH, DEPTH, PAST_LEN, DA_WIDTH)
    cache_v = cache_v.reshape(DEC_BATCH, DEPTH, PAST_LEN, DA_HEADS * DA_VDIM)

    w_in_bf, w_ffn_in_bf, w_ffn_out_bf = w_in.astype(BF16), w_ffn_in.astype(BF16), w_ffn_out.astype(BF16)

    new_kv = None
    for l in range(DEPTH):
        proj, *new_kv = _in_projection(_pre_norm(x, g_norm1[l][None, :], mod4, l), l, w_in_bf, new_kv)

        subln = da_subln[l][None, :]
        attn = _diff_attention(proj, da_lambda[l], subln, l, latent=False)
        attn = _diff_attention(proj, da_lambda[l], subln, l, latent=True, cache=(cache_k, cache_v), rope=rope,
                               out_buf=attn)
        conv_args = (sc_conv[l], cf_conv[l], cf_conv_b[l][None, :], cf_ln_g[l][None, :], cf_ln_b[l][None, :])
        conv_pre = _conv_branches(proj, *conv_args, latent=False)
        b_pre, c_pre = _conv_branches(proj, *conv_args, latent=True, out_bufs=conv_pre)

        merged = _gated_merge(attn, b_pre, c_pre, proj, b_gate[l][None, :], l, w_da_out, w_sc_out, w_cf_out)
        x = _out_projection(merged, w_out, x, mod4, l)
        x = _swiglu(x, g_norm2[l][None, :], mod4, l, w_ffn_in_bf, w_ffn_out_bf,
                    g_final[None, :] if l == DEPTH - 1 else None)

    y_prompt, y_sample = x
    new_k, new_v = new_kv
    return (y_prompt.reshape(BATCH, SEQ, D_MODEL), y_sample.reshape(DEC_BATCH, DEC_SEQ, D_MODEL),
            new_k.reshape(BATCH, DEPTH, SEQ, DA_HEADS, 2 * DA_HEAD_DIM),
            new_v.reshape(BATCH, DEPTH, SEQ, DA_HEADS, DA_VDIM))
```

```python
import functools
import math

import jax
import jax.numpy as jnp
from jax import lax
from jax.experimental import pallas as pl
from jax.experimental.pallas import tpu as pltpu

D_MODEL = 2048
BATCH = 16
SEQ = 256
DEPTH = 2
DEC_BATCH = 8
DEC_SEQ = 1024
PAST_LEN = 256
GRID_W = 64
DA_HEADS = 8
DA_HEAD_DIM = 64
DA_VDIM = 2 * DA_HEAD_DIM
DA_WIDTH = DA_HEADS * 2 * DA_HEAD_DIM
SC_WIDTH = 512
SC_K = 3
CF_WIDTH = 512
CF_K = 31
N_BRANCH = 3
FF_HIDDEN = -(-8 * D_MODEL // (3 * 256)) * 256
ROPE_THETA = 10000.0
EPS = 1e-6

OFF_K = DA_WIDTH
OFF_V = 2 * DA_WIDTH
OFF_SC = 3 * DA_WIDTH
OFF_CF = OFF_SC + 3 * SC_WIDTH
OFF_GATE = OFF_CF + 2 * CF_WIDTH
IN_COLS = OFF_GATE + N_BRANCH * D_MODEL

N_CTX = BATCH * SEQ
N_LAT = DEC_BATCH * DEC_SEQ
N_TOK = N_CTX + N_LAT
MOD_ROWS = 16

F32 = jnp.float32
BF16 = jnp.bfloat16

VMEM_BUDGET_V7X = 56 * 1024 * 1024
LANES = 128
SUBLANES = 8

TM_NORM = 512
TM_IN = 2048
TM = 1024
TN = 512
TM_FFN = 512
TH_FFN = 512
TQ_LAT = 1024
HEADS_LAT = 2
HEADS_CTX = 8
TQ_SUB = 256
LOG2_E = math.log2(math.e)
CONV_ROWS = 32


def _nbytes(shape, dtype):
    return math.prod(shape) * jnp.dtype(dtype).itemsize


def _vmem_limit(blocks, scratch=(), temps=()):
    total = 2 * sum(_nbytes(s, d) for s, d in blocks)
    total += sum(_nbytes(s, d) for s, d in scratch)
    total += sum(_nbytes(s, d) for s, d in temps)
    return min(total, VMEM_BUDGET_V7X)


def _mod_group(i, tm):
    return jnp.maximum(i * tm - N_CTX + DEC_SEQ, 0) // DEC_SEQ


def _rms(x, g):
    return x * lax.rsqrt(jnp.mean(x * x, axis=-1, keepdims=True) + EPS) * g


def _mod_norm(x, g, shift, scale):
    return _rms(x, g) * (1.0 + scale) + shift


def _sigmoid(x):
    return 0.5 * jnp.tanh(0.5 * x) + 0.5


def _mod_kernel(c_ref, w_ref, b_ref, o_ref):
    c = c_ref[...]
    s = (c * _sigmoid(c)).astype(BF16)
    o_ref[...] = jnp.dot(s, w_ref[...].astype(BF16), preferred_element_type=F32) + b_ref[...]


def _modulation(cvec, w_mod, b_mod):
    bn = 1024
    return pl.pallas_call(
        _mod_kernel,
        grid=(DEPTH, 6 * D_MODEL // bn),
        in_specs=[
            pl.BlockSpec((MOD_ROWS, D_MODEL), lambda l, j: (0, 0)),
            pl.BlockSpec((None, D_MODEL, bn), lambda l, j: (l, 0, j)),
            pl.BlockSpec((None, 1, bn), lambda l, j: (l, 0, j)),
        ],
        out_specs=pl.BlockSpec((None, MOD_ROWS, bn), lambda l, j: (l, 0, j)),
        out_shape=jax.ShapeDtypeStruct((DEPTH, MOD_ROWS, 6 * D_MODEL), F32),
        compiler_params=pltpu.CompilerParams(
            dimension_semantics=("parallel", "parallel"),
            vmem_limit_bytes=_vmem_limit(
                [((D_MODEL, bn), F32), ((MOD_ROWS, D_MODEL), F32), ((MOD_ROWS, bn), F32)],
                temps=[((D_MODEL, bn), BF16), ((D_MODEL, bn), F32)])),
        name="modulation",
    )(cvec, w_mod, b_mod.reshape(DEPTH, 1, 6 * D_MODEL))


def _mod_spec(layer, chunk, tm):
    return pl.BlockSpec((None, None, 1, D_MODEL),
                        lambda i, *_: (layer, _mod_group(i, tm), 0, chunk))


def _group_specs(tm, tn, n_col_tiles):
    nct = N_CTX // tm
    if n_col_tiles is None:
        return (pl.BlockSpec((tm, tn), lambda i: (jnp.minimum(i, nct - 1), 0)),
                pl.BlockSpec((tm, tn), lambda i: (jnp.maximum(i - nct, 0), 0)))
    last = n_col_tiles - 1
    return (pl.BlockSpec((tm, tn), lambda i, j: (jnp.minimum(i, nct - 1), jnp.where(i < nct, j, last))),
            pl.BlockSpec((tm, tn), lambda i, j: (jnp.maximum(i - nct, 0), jnp.where(i < nct, 0, j))))


def _group_tile(xc_ref, xl_ref, tm):
    return jnp.where(pl.program_id(0) < N_CTX // tm, xc_ref[...], xl_ref[...])


def _prenorm_kernel(xc_ref, xl_ref, g_ref, shift_ref, scale_ref, o_ref):
    x = _group_tile(xc_ref, xl_ref, TM_NORM)
    o_ref[...] = _mod_norm(x, g_ref[...], shift_ref[...], scale_ref[...]).astype(BF16)


def _pre_norm(x_pair, g, mod4, layer):
    return pl.pallas_call(
        _prenorm_kernel,
        grid=(N_TOK // TM_NORM,),
        in_specs=[
            *_group_specs(TM_NORM, D_MODEL, None),
            pl.BlockSpec((1, D_MODEL), lambda i: (0, 0)),
            _mod_spec(layer, 0, TM_NORM),
            _mod_spec(layer, 1, TM_NORM),
        ],
        out_specs=pl.BlockSpec((TM_NORM, D_MODEL), lambda i: (i, 0)),
        out_shape=jax.ShapeDtypeStruct((N_TOK, D_MODEL), BF16),
        compiler_params=pltpu.CompilerParams(
            dimension_semantics=("arbitrary",),
            vmem_limit_bytes=_vmem_limit(
                [((TM_NORM, D_MODEL), F32)] * 2 + [((TM_NORM, D_MODEL), BF16)],
                temps=[((TM_NORM, D_MODEL), F32)] * 3)),
        name="pre_norm",
    )(*x_pair, g, mod4, mod4)


N_CTX_TILES = N_CTX // TM_IN
KV_TILES = DA_WIDTH // TN


def _inproj_kernel(h_ref, w_ref, *rest):
    o_ref, nk_ref, nv_ref = rest[-3:]
    i, j = pl.program_id(0), pl.program_id(1)
    o_ref[...] = jnp.dot(h_ref[...], w_ref[...], preferred_element_type=F32)

    @pl.when((i < N_CTX_TILES) & (j >= OFF_K // TN) & (j < OFF_V // TN))
    def _():
        nk_ref[...] = o_ref[...].reshape(nk_ref.shape)

    @pl.when((i < N_CTX_TILES) & (j >= OFF_V // TN) & (j < OFF_SC // TN))
    def _():
        nv_ref[...] = o_ref[...].reshape(nv_ref.shape)


def _kv_cache_spec(layer, col0):
    def index(i, j):
        col = jnp.where(i < N_CTX_TILES, jnp.clip(j - col0, 0, KV_TILES - 1), KV_TILES - 1)
        return (jnp.minimum(i, N_CTX_TILES - 1), layer, 0, col)
    return pl.BlockSpec((TM_IN // SEQ, None, SEQ, TN), index)


def _in_projection(h, layer, w_in, new_kv=None):
    in_specs = [
        pl.BlockSpec((TM_IN, D_MODEL), lambda i, j: (i, 0)),
        pl.BlockSpec((None, D_MODEL, TN), lambda i, j: (layer, 0, j)),
    ]
    args = [h, w_in]
    aliases = {}
    if new_kv is not None:
        aliases = {len(args): 1, len(args) + 1: 2}
        in_specs += [pl.BlockSpec(memory_space=pl.ANY)] * 2
        args += list(new_kv)
    kv_sds = jax.ShapeDtypeStruct((BATCH, DEPTH, SEQ, DA_WIDTH), F32)
    kv_block = ((TM_IN // SEQ, SEQ, TN), F32)
    return pl.pallas_call(
        _inproj_kernel,
        grid=(N_TOK // TM_IN, IN_COLS // TN),
        in_specs=in_specs,
        out_specs=[pl.BlockSpec((TM_IN, TN), lambda i, j: (i, j)),
                   _kv_cache_spec(layer, OFF_K // TN), _kv_cache_spec(layer, OFF_V // TN)],
        out_shape=[jax.ShapeDtypeStruct((N_TOK, IN_COLS), F32), kv_sds, kv_sds],
        input_output_aliases=aliases,
        compiler_params=pltpu.CompilerParams(
            dimension_semantics=("arbitrary", "arbitrary"),
            vmem_limit_bytes=_vmem_limit(
                [((TM_IN, D_MODEL), BF16), ((D_MODEL, TN), BF16), ((TM_IN, TN), F32), kv_block, kv_block],
                temps=[((TM_IN, TN), F32)])),
        name="in_projection",
    )(*args)


def _rope(x, cos, sin_lo, sin_hi):
    return (x * cos + pltpu.roll(x, LANES - 16, 1) * sin_lo + pltpu.roll(x, 16, 1) * sin_hi)


def _make_attn_kernel(seq_len, tq, n_cache, n_heads, use_rope, lam_init, aliased):
    def kernel(*refs):
        dl_ref, g_ref, q_ref, k_ref, v_ref = refs[:5]
        refs = refs[5:]
        if n_cache:
            ck_ref, cv_ref = refs[:2]
            refs = refs[2:]
        if use_rope:
            cos_ref, slo_ref, shi_ref = refs[:3]
            refs = refs[3:]
        if aliased:
            refs = refs[1:]
        o_ref, kk_scr, vv_scr = refs
        qb = pl.program_id(2)

        @pl.when(qb == 0)
        def _():
            for h in range(n_heads):
                cols = slice(h * LANES, (h + 1) * LANES)
                k = k_ref[:, cols]
                if use_rope:
                    k = _rope(k, cos_ref[...], slo_ref[...], shi_ref[...])
                if n_cache:
                    kk_scr[h, 0:n_cache, :] = ck_ref[:, cols].astype(BF16)
                    vv_scr[h, 0:n_cache, 0:DA_VDIM] = cv_ref[:, cols].astype(BF16)
                kk_scr[h, n_cache:n_cache + seq_len, :] = k.astype(BF16)
                vv_scr[h, n_cache:n_cache + seq_len, 0:DA_VDIM] = v_ref[:, cols].astype(BF16)
                vv_scr[h, :, DA_VDIM:] = jnp.ones((n_cache + seq_len, DA_VDIM), BF16)

        dl = dl_ref[...]
        lam = (jnp.exp(jnp.sum(dl[0:1] * dl[1:2], axis=-1, keepdims=True))
               - jnp.exp(jnp.sum(dl[2:3] * dl[3:4], axis=-1, keepdims=True)) + lam_init)
        tsub = min(tq, TQ_SUB)
        lane = lax.broadcasted_iota(jnp.int32, (tsub, LANES), 1)

        def softmax_times_v(qm, kk, vv):
            s = lax.dot_general(qm.astype(BF16), kk, (((1,), (1,)), ((), ())), preferred_element_type=F32)
            e = jnp.exp2(s - jnp.max(s, axis=-1, keepdims=True)).astype(BF16)
            ev = jnp.dot(e, vv, preferred_element_type=F32)
            return ev[:, :DA_VDIM] / ev[:, DA_VDIM:]

        for h in range(n_heads):
            cols = slice(h * LANES, (h + 1) * LANES)
            kk, vv = kk_scr[h], vv_scr[h]
            for c in range(tq // tsub):
                q = q_ref[c * tsub:(c + 1) * tsub, cols]
                if use_rope:
                    rows = pl.ds(pl.multiple_of(qb * tq, tq) + c * tsub, tsub)
                    q = _rope(q, cos_ref[rows, :], slo_ref[rows, :], shi_ref[rows, :])
                q = q * (DA_HEAD_DIM ** -0.5 * LOG2_E)
                o1 = softmax_times_v(jnp.where(lane < DA_HEAD_DIM, q, 0.0), kk, vv)
                o2 = softmax_times_v(jnp.where(lane >= DA_HEAD_DIM, q, 0.0), kk, vv)
                o_ref[c * tsub:(c + 1) * tsub, cols] = (
                    _rms(o1 - lam * o2, g_ref[...]) * (1.0 - lam_init)).astype(BF16)

    return kernel


def _diff_attention(proj, da_lambda_l, subln_g, layer, *, latent, cache=None, rope=None, out_buf=None):
    lam_init = 0.8 - 0.6 * math.exp(-0.3 * layer)
    if latent:
        n_seq, seq_len, tq, n_cache, row0, n_heads = DEC_BATCH, DEC_SEQ, TQ_LAT, PAST_LEN, N_CTX, HEADS_LAT
    else:
        n_seq, seq_len, tq, n_cache, row0, n_heads = BATCH, SEQ, SEQ, 0, 0, HEADS_CTX
    nq = seq_len // tq
    lk = n_cache + seq_len
    width = n_heads * LANES
    hk, hv = OFF_K // width, OFF_V // width

    in_specs = [
        pl.BlockSpec((4, DA_HEAD_DIM), lambda b, h, t: (0, 0)),
        pl.BlockSpec((1, DA_VDIM), lambda b, h, t: (0, 0)),
        pl.BlockSpec((tq, width), lambda b, h, t: (row0 // tq + b * nq + t, h)),
        pl.BlockSpec((seq_len, width), lambda b, h, t: (row0 // seq_len + b, hk + h)),
        pl.BlockSpec((seq_len, width), lambda b, h, t: (row0 // seq_len + b, hv + h)),
    ]
    args = [da_lambda_l, subln_g, proj, proj, proj]
    blocks = [((tq, width), F32), ((seq_len, width), F32), ((seq_len, width), F32), ((tq, width), BF16)]
    if latent:
        cache_k, cache_v = cache
        in_specs += [pl.BlockSpec((None, None, n_cache, width), lambda b, h, t: (b, layer, 0, h))] * 2
        args += [cache_k, cache_v]
        in_specs += [pl.BlockSpec((seq_len, LANES), lambda b, h, t: (0, 0))] * 3
        args += list(rope)
        blocks += [((n_cache, width), F32)] * 2 + [((seq_len, LANES), F32)] * 3
    aliases = {}
    if out_buf is not None:
        aliases = {len(args): 0}
        in_specs.append(pl.BlockSpec(memory_space=pl.ANY))
        args.append(out_buf)

    tsub = min(tq, TQ_SUB)
    scratch = [((n_heads, lk, LANES), BF16), ((n_heads, lk, 2 * DA_VDIM), BF16)]
    return pl.pallas_call(
        _make_attn_kernel(seq_len, tq, n_cache, n_heads, latent, lam_init, out_buf is not None),
        grid=(n_seq, DA_HEADS // n_heads, nq),
        in_specs=in_specs,
        out_specs=pl.BlockSpec((tq, width), lambda b, h, t: (row0 // tq + b * nq + t, h)),
        out_shape=jax.ShapeDtypeStruct((N_TOK, DA_WIDTH), BF16),
        scratch_shapes=[pltpu.VMEM(s, d) for s, d in scratch],
        input_output_aliases=aliases,
        compiler_params=pltpu.CompilerParams(
            dimension_semantics=("parallel", "parallel", "arbitrary"),
            vmem_limit_bytes=_vmem_limit(
                blocks, scratch=scratch,
                temps=[((tsub, lk), F32)] * 6 * 4 + [((tq, 2 * DA_VDIM), F32)] * 4)),
        name="diff_attention_latent" if latent else "diff_attention_context",
    )(*args)


def _fill_shifted(scr, value, pad, seq_len, shifts):
    width = value.shape[-1]
    n = seq_len + 2 * pad - SUBLANES
    scr[0, 0:pad, :] = jnp.zeros((pad, width), F32)
    scr[0, pad + seq_len:, :] = jnp.zeros((pad, width), F32)
    scr[0, pad:pad + seq_len, :] = value
    for s, shift in enumerate(shifts, start=1):
        scr[s, 0:n, :] = scr[0, shift:shift + n, :]


def _tap(scr, shifts, r0, offset):
    aligned, rem = offset - offset % SUBLANES, offset % SUBLANES
    slot = 0 if rem == 0 else 1 + shifts.index(rem)
    return scr[slot, pl.ds(r0 + aligned, CONV_ROWS), :]


def _make_conv_kernel(seq_len):
    pad3, pad31 = SUBLANES, 2 * SUBLANES
    shifts3 = (1, SUBLANES - 1)
    shifts31 = tuple(range(1, SUBLANES))

    def kernel(gb_ref, gc_ref, sx_ref, ca_ref, cb_ref, w3_ref, w31_ref, b31_ref, lng_ref, lnb_ref, *rest):
        bo_ref, co_ref, t_scr, u_scr, y_scr = rest[-5:]
        _fill_shifted(t_scr, gc_ref[...] * sx_ref[...], pad3, seq_len, shifts3)
        _fill_shifted(u_scr, ca_ref[...] * _sigmoid(cb_ref[...]), pad31, seq_len, shifts31)

        def chunk(c, carry):
            r0 = pl.multiple_of(c * CONV_ROWS, CONV_ROWS)
            rows = pl.ds(r0, CONV_ROWS)
            y = _tap(t_scr, shifts3, r0, pad3 - SC_K // 2) * w3_ref[0:1, :]
            for k in range(1, SC_K):
                y = y + _tap(t_scr, shifts3, r0, pad3 - SC_K // 2 + k) * w3_ref[k:k + 1, :]
            bo_ref[rows, :] = (gb_ref[rows, :] * y).astype(BF16)

            acc = _tap(u_scr, shifts31, r0, pad31 - CF_K // 2) * w31_ref[0:1, :]
            for k in range(1, CF_K):
                acc = acc + _tap(u_scr, shifts31, r0, pad31 - CF_K // 2 + k) * w31_ref[k:k + 1, :]
            y_scr[rows, :] = acc + b31_ref[...]
            return carry

        lax.fori_loop(0, seq_len // CONV_ROWS, chunk, 0)

        y = y_scr[...]
        xc = y - jnp.mean(y, axis=-1, keepdims=True)
        yn = xc * lax.rsqrt(jnp.mean(xc * xc, axis=-1, keepdims=True) + EPS) * lng_ref[...] + lnb_ref[...]
        co_ref[...] = (yn * _sigmoid(yn)).astype(BF16)

    return kernel


def _conv_branches(proj, sc_conv_l, cf_conv_l, cf_b, ln_g, ln_b, *, latent, out_bufs=None):
    if latent:
        n_seq, seq_len, row0 = DEC_BATCH, DEC_SEQ, N_CTX
    else:
        n_seq, seq_len, row0 = BATCH, SEQ, 0
    rb = row0 // seq_len
    c0 = OFF_SC // SC_WIDTH

    def col(cidx):
        return pl.BlockSpec((seq_len, SC_WIDTH), lambda b: (rb + b, cidx))

    def whole(shape):
        return pl.BlockSpec(shape, lambda b: (0, 0))

    in_specs = [col(c0), col(c0 + 1), col(c0 + 2), col(c0 + 3), col(c0 + 4),
                whole((SC_K, SC_WIDTH)), whole((CF_K, CF_WIDTH)),
                whole((1, CF_WIDTH)), whole((1, CF_WIDTH)), whole((1, CF_WIDTH))]
    args = [proj, proj, proj, proj, proj, sc_conv_l, cf_conv_l, cf_b, ln_g, ln_b]
    aliases = {}
    if out_bufs is not None:
        aliases = {len(args): 0, len(args) + 1: 1}
        in_specs += [pl.BlockSpec(memory_space=pl.ANY)] * 2
        args += list(out_bufs)

    out_spec = pl.BlockSpec((seq_len, SC_WIDTH), lambda b: (rb + b, 0))
    out_sds = jax.ShapeDtypeStruct((N_TOK, SC_WIDTH), BF16)
    scratch = [((3, seq_len + 2 * SUBLANES, SC_WIDTH), F32),
               ((SUBLANES, seq_len + 4 * SUBLANES, CF_WIDTH), F32),
               ((seq_len, CF_WIDTH), F32)]
    return pl.pallas_call(
        _make_conv_kernel(seq_len),
        grid=(n_seq,),
        in_specs=in_specs,
        out_specs=[out_spec, out_spec],
        out_shape=[out_sds, out_sds],
        scratch_shapes=[pltpu.VMEM(s, d) for s, d in scratch],
        input_output_aliases=aliases,
        compiler_params=pltpu.CompilerParams(
            dimension_semantics=("parallel",),
            vmem_limit_bytes=_vmem_limit(
                [((seq_len, SC_WIDTH), F32)] * 5 + [((seq_len, SC_WIDTH), BF16)] * 2,
                scratch=scratch,
                temps=[((seq_len, SC_WIDTH), F32)] * 4)),
        name="conv_branches_latent" if latent else "conv_branches_context",
    )(*args)


def _merge_kernel(a_ref, b_ref, c_ref, ga_ref, gb_ref, gc_ref, bga_ref, bgb_ref, bgc_ref,
                  wa_ref, wb_ref, wc_ref, o_ref):
    br_a = jnp.dot(a_ref[...], wa_ref[...], preferred_element_type=F32)
    br_b = jnp.dot(b_ref[...], wb_ref[...], preferred_element_type=F32)
    br_c = jnp.dot(c_ref[...], wc_ref[...], preferred_element_type=F32)
    merged = (_sigmoid(ga_ref[...] + bga_ref[...]) * br_a
              + _sigmoid(gb_ref[...] + bgb_ref[...]) * br_b
              + _sigmoid(gc_ref[...] + bgc_ref[...]) * br_c)
    o_ref[...] = merged.astype(BF16)


def _gated_merge(attn, b_pre, c_pre, proj, b_gate_l, layer, w_da, w_sc, w_cf):
    g0 = OFF_GATE // TN
    gstep = D_MODEL // TN

    def gate(k):
        return pl.BlockSpec((TM, TN), lambda i, j: (i, g0 + k * gstep + j))

    def gbias(k):
        return pl.BlockSpec((1, TN), lambda i, j: (0, k * gstep + j))

    def wcol(rows):
        return pl.BlockSpec((None, rows, TN), lambda i, j: (layer, 0, j))

    return pl.pallas_call(
        _merge_kernel,
        grid=(N_TOK // TM, D_MODEL // TN),
        in_specs=[
            pl.BlockSpec((TM, DA_WIDTH), lambda i, j: (i, 0)),
            pl.BlockSpec((TM, SC_WIDTH), lambda i, j: (i, 0)),
            pl.BlockSpec((TM, CF_WIDTH), lambda i, j: (i, 0)),
            gate(0), gate(1), gate(2), gbias(0), gbias(1), gbias(2),
            wcol(DA_WIDTH), wcol(SC_WIDTH), wcol(CF_WIDTH),
        ],
        out_specs=pl.BlockSpec((TM, TN), lambda i, j: (i, j)),
        out_shape=jax.ShapeDtypeStruct((N_TOK, D_MODEL), BF16),
        compiler_params=pltpu.CompilerParams(
            dimension_semantics=("parallel", "parallel"),
            vmem_limit_bytes=_vmem_limit(
                [((TM, DA_WIDTH), BF16), ((TM, SC_WIDTH), BF16), ((TM, CF_WIDTH), BF16)]
                + [((TM, TN), F32)] * 3 + [((DA_WIDTH, TN), BF16), ((SC_WIDTH, TN), BF16),
                                           ((CF_WIDTH, TN), BF16), ((TM, TN), BF16)],
                temps=[((TM, TN), F32)] * 8)),
        name="gated_merge",
    )(attn, b_pre, c_pre, proj, proj, proj, b_gate_l, b_gate_l, b_gate_l, w_da, w_sc, w_cf)


def _outproj_kernel(m_ref, w_ref, xc_ref, xl_ref, gate_ref, o_ref):
    x = _group_tile(xc_ref, xl_ref, TM)
    o_ref[...] = x + gate_ref[...] * jnp.dot(m_ref[...], w_ref[...], preferred_element_type=F32)


def _out_projection(merged, w_out, x_pair, mod4, layer):
    n_col = D_MODEL // TN
    gate_spec = pl.BlockSpec((None, None, 1, TN),
                             lambda i, j: (layer, _mod_group(i, TM), 0, 2 * n_col + j))
    return pl.pallas_call(
        _outproj_kernel,
        grid=(N_TOK // TM, n_col),
        in_specs=[
            pl.BlockSpec((TM, D_MODEL), lambda i, j: (i, 0)),
            pl.BlockSpec((None, D_MODEL, TN), lambda i, j: (layer, 0, j)),
            *_group_specs(TM, TN, n_col),
            gate_spec,
        ],
        out_specs=pl.BlockSpec((TM, TN), lambda i, j: (i, j)),
        out_shape=jax.ShapeDtypeStruct((N_TOK, D_MODEL), F32),
        compiler_params=pltpu.CompilerParams(
            dimension_semantics=("arbitrary", "arbitrary"),
            vmem_limit_bytes=_vmem_limit(
                [((TM, D_MODEL), BF16), ((D_MODEL, TN), BF16)] + [((TM, TN), F32)] * 3,
                temps=[((TM, TN), F32)] * 3)),
        name="out_projection",
    )(merged, w_out, *x_pair, mod4)


def _make_ffn_kernel(final_norm):
    def kernel(*refs):
        x_ref, g_ref, shift_ref, scale_ref, gate_ref, wu_ref, ww_ref, wo_ref = refs[:8]
        refs = refs[8:]
        if final_norm:
            gf_ref = refs[0]
            refs = refs[1:]
        y_ctx_ref, y_lat_ref, h_scr, acc_scr = refs
        i, j = pl.program_id(0), pl.program_id(1)

        @pl.when(j == 0)
        def _():
            h_scr[...] = _mod_norm(x_ref[...], g_ref[...], shift_ref[...], scale_ref[...]).astype(BF16)
            acc_scr[...] = jnp.zeros_like(acc_scr)

        h = h_scr[...]
        u = jnp.dot(h, wu_ref[...], preferred_element_type=F32)
        w = jnp.dot(h, ww_ref[...], preferred_element_type=F32)
        act = (u * _sigmoid(u) * w).astype(BF16)
        acc_scr[...] += jnp.dot(act, wo_ref[...], preferred_element_type=F32)

        def result():
            y = x_ref[...] + gate_ref[...] * acc_scr[...]
            return _rms(y, gf_ref[...]) if final_norm else y

        last = j == pl.num_programs(1) - 1

        @pl.when(last & (i < N_CTX // TM_FFN))
        def _():
            y_ctx_ref[...] = result()

        @pl.when(last & (i >= N_CTX // TM_FFN))
        def _():
            y_lat_ref[...] = result()

    return kernel


def _swiglu(x, g, mod4, layer, w_ffn_in, w_ffn_out, g_final=None):
    final_norm = g_final is not None
    nh = FF_HIDDEN // TH_FFN
    n_ctx_tiles = N_CTX // TM_FFN
    row = pl.BlockSpec((1, D_MODEL), lambda i, j: (0, 0))
    in_specs = [
        pl.BlockSpec((TM_FFN, D_MODEL), lambda i, j: (i, 0)),
        row,
        _mod_spec(layer, 3, TM_FFN), _mod_spec(layer, 4, TM_FFN), _mod_spec(layer, 5, TM_FFN),
        pl.BlockSpec((None, D_MODEL, TH_FFN), lambda i, j: (layer, 0, j)),
        pl.BlockSpec((None, D_MODEL, TH_FFN), lambda i, j: (layer, 0, nh + j)),
        pl.BlockSpec((None, TH_FFN, D_MODEL), lambda i, j: (layer, j, 0)),
    ]
    args = [x, g, mod4, mod4, mod4, w_ffn_in, w_ffn_in, w_ffn_out]
    if final_norm:
        in_specs.append(row)
        args.append(g_final)
    out_specs = [pl.BlockSpec((TM_FFN, D_MODEL), lambda i, j: (jnp.minimum(i, n_ctx_tiles - 1), 0)),
                 pl.BlockSpec((TM_FFN, D_MODEL), lambda i, j: (jnp.maximum(i - n_ctx_tiles, 0), 0))]
    out_shape = [jax.ShapeDtypeStruct((N_CTX, D_MODEL), F32), jax.ShapeDtypeStruct((N_LAT, D_MODEL), F32)]
    scratch = [((TM_FFN, D_MODEL), BF16), ((TM_FFN, D_MODEL), F32)]
    return pl.pallas_call(
        _make_ffn_kernel(final_norm),
        grid=(N_TOK // TM_FFN, nh),
        in_specs=in_specs,
        out_specs=out_specs,
        out_shape=out_shape,
        scratch_shapes=[pltpu.VMEM(s, d) for s, d in scratch],
        compiler_params=pltpu.CompilerParams(
            dimension_semantics=("arbitrary", "arbitrary"),
            vmem_limit_bytes=_vmem_limit(
                [((TM_FFN, D_MODEL), F32)] * 3 + [((D_MODEL, TH_FFN), BF16)] * 2
                + [((TH_FFN, D_MODEL), BF16)],
                scratch=scratch,
                temps=[((TM_FFN, TH_FFN), F32)] * 4 + [((TM_FFN, D_MODEL), F32)] * 2)),
        name="swiglu_final" if final_norm else "swiglu",
    )(*args)


def _rope_tables():
    rows = DEC_SEQ // GRID_W
    row_ids = jnp.repeat(jnp.arange(rows), GRID_W).astype(F32)
    col_ids = jnp.tile(jnp.arange(GRID_W), rows).astype(F32)
    n_freq = DA_HEAD_DIM // 4
    inv = ROPE_THETA ** (-jnp.arange(n_freq, dtype=F32) / n_freq)
    ang_r, ang_c = row_ids[:, None] * inv, col_ids[:, None] * inv
    zero = jnp.zeros_like(ang_r)
    cos = jnp.concatenate([jnp.cos(ang_r)] * 2 + [jnp.cos(ang_c)] * 2, axis=-1)
    sin_lo = jnp.concatenate([-jnp.sin(ang_r), zero, -jnp.sin(ang_c), zero], axis=-1)
    sin_hi = jnp.concatenate([zero, jnp.sin(ang_r), zero, jnp.sin(ang_c)], axis=-1)
    return tuple(jnp.tile(t, (1, LANES // DA_HEAD_DIM)) for t in (cos, sin_lo, sin_hi))


def kernel(x_prompt, x_sample, cache_k, cache_v, c, c_ctx, w_mod, b_mod, g_norm1, w_in, da_lambda, da_subln,
           w_da_out, sc_conv, w_sc_out, cf_conv, cf_conv_b, cf_ln_g, cf_ln_b, w_cf_out, b_gate, w_out,
           g_norm2, w_ffn_in, w_ffn_out, g_final):
    x_pair = (x_prompt.reshape(N_CTX, D_MODEL), x_sample.reshape(N_LAT, D_MODEL))
    cvec =jnp.concatenate([c_ctx[None, :], c, jnp.zeros((MOD_ROWS - 1 - DEC_BATCH, D_MODEL), F32)], axis=0)
    mod4 = _modulation(cvec, w_mod, b_mod).reshape(DEPTH, MOD_ROWS, 1, 6 * D_MODEL)
    rope = _rope_tables()
    cache_k = cache_k.reshape(DEC_BATCH, DEPTH, PAST_LEN, DA_WIDTH)
    cache_v = cache_v.reshape(DEC_BATCH, DEPTH, PAST_LEN, DA_HEADS * DA_VDIM)

    w_in_bf, w_ffn_in_bf, w_ffn_out_bf = w_in.astype(BF16), w_ffn_in.astype(BF16), w_ffn_out.astype(BF16)
    w_da_bf, w_sc_bf, w_cf_bf, w_out_bf = (w.astype(BF16) for w in (w_da_out, w_sc_out, w_cf_out, w_out))

    new_kv = None
    for l in range(DEPTH):
        proj, *new_kv = _in_projection(_pre_norm(x_pair, g_norm1[l][None, :], mod4, l), l, w_in_bf, new_kv)

        subln = da_subln[l][None, :]
        attn = _diff_attention(proj, da_lambda[l], subln, l, latent=False)
        attn = _diff_attention(proj, da_lambda[l], subln, l, latent=True, cache=(cache_k, cache_v), rope=rope,
                               out_buf=attn)
        conv_args = (sc_conv[l], cf_conv[l], cf_conv_b[l][None, :], cf_ln_g[l][None, :], cf_ln_b[l][None, :])
        conv_pre = _conv_branches(proj, *conv_args, latent=False)
        b_pre, c_pre = _conv_branches(proj, *conv_args, latent=True, out_bufs=conv_pre)

        merged = _gated_merge(attn, b_pre, c_pre, proj, b_gate[l][None, :], l, w_da_bf, w_sc_bf, w_cf_bf)
        x_mid = _out_projection(merged, w_out_bf, x_pair, mod4, l)
        x_pair = _swiglu(x_mid, g_norm2[l][None, :], mod4, l, w_ffn_in_bf, w_ffn_out_bf,
                         g_final[None, :] if l == DEPTH - 1 else None)

    y_prompt, y_sample = x_pair
    new_k, new_v = new_kv
    return (y_prompt.reshape(BATCH, SEQ, D_MODEL), y_sample.reshape(DEC_BATCH, DEC_SEQ, D_MODEL),
            new_k.reshape(BATCH, DEPTH, SEQ, DA_HEADS, 2 * DA_HEAD_DIM),
            new_v.reshape(BATCH, DEPTH, SEQ, DA_HEADS, DA_VDIM))
```

```python
import functools
import math

import jax
import jax.numpy as jnp
from jax import lax
from jax.experimental import pallas as pl
from jax.experimental.pallas import tpu as pltpu

D_MODEL = 2048
BATCH = 16
SEQ = 256
DEPTH = 2
DEC_BATCH = 8
DEC_SEQ = 1024
PAST_LEN = 256
GRID_W = 64
DA_HEADS = 8
DA_HEAD_DIM = 64
DA_VDIM = 2 * DA_HEAD_DIM
DA_WIDTH = DA_HEADS * 2 * DA_HEAD_DIM
SC_WIDTH = 512
SC_K = 3
CF_WIDTH = 512
CF_K = 31
N_BRANCH = 3
FF_HIDDEN = -(-8 * D_MODEL // (3 * 256)) * 256
ROPE_THETA = 10000.0
EPS = 1e-6

OFF_K = DA_WIDTH
OFF_V = 2 * DA_WIDTH
OFF_SC = 3 * DA_WIDTH
OFF_CF = OFF_SC + 3 * SC_WIDTH
OFF_GATE = OFF_CF + 2 * CF_WIDTH
IN_COLS = OFF_GATE + N_BRANCH * D_MODEL

N_CTX = BATCH * SEQ
N_LAT = DEC_BATCH * DEC_SEQ
N_TOK = N_CTX + N_LAT
MOD_ROWS = 16

F32 = jnp.float32
BF16 = jnp.bfloat16

VMEM_BUDGET_V7X = 56 * 1024 * 1024
LANES = 128
SUBLANES = 8

TM_NORM = 512
TM_IN = 2048
TM = 1024
TN = 512
TM_FFN = 512
TH_FFN = 512
TQ_LAT = 1024
HEADS_LAT = 2
HEADS_CTX = 8
TQ_SUB = 256
LOG2_E = math.log2(math.e)
CONV_ROWS = 32


def _nbytes(shape, dtype):
    return math.prod(shape) * jnp.dtype(dtype).itemsize


def _vmem_limit(blocks, scratch=(), temps=()):
    total = 2 * sum(_nbytes(s, d) for s, d in blocks)
    total += sum(_nbytes(s, d) for s, d in scratch)
    total += sum(_nbytes(s, d) for s, d in temps)
    return min(total, VMEM_BUDGET_V7X)


def _mod_group(i, tm):
    return jnp.maximum(i * tm - N_CTX + DEC_SEQ, 0) // DEC_SEQ


def _rms(x, g):
    return x * lax.rsqrt(jnp.mean(x * x, axis=-1, keepdims=True) + EPS) * g


def _mod_norm(x, g, shift, scale):
    return _rms(x, g) * (1.0 + scale) + shift


def _sigmoid(x):
    return 0.5 * jnp.tanh(0.5 * x) + 0.5


def _mod_kernel(c_ref, w_ref, b_ref, o_ref):
    c = c_ref[...]
    s = (c * _sigmoid(c)).astype(BF16)
    o_ref[...] = jnp.dot(s, w_ref[...].astype(BF16), preferred_element_type=F32) + b_ref[...]


def _modulation(cvec, w_mod, b_mod):
    bn = 1024
    return pl.pallas_call(
        _mod_kernel,
        grid=(DEPTH, 6 * D_MODEL // bn),
        in_specs=[
            pl.BlockSpec((MOD_ROWS, D_MODEL), lambda l, j: (0, 0)),
            pl.BlockSpec((None, D_MODEL, bn), lambda l, j: (l, 0, j)),
            pl.BlockSpec((None, 1, bn), lambda l, j: (l, 0, j)),
        ],
        out_specs=pl.BlockSpec((None, MOD_ROWS, bn), lambda l, j: (l, 0, j)),
        out_shape=jax.ShapeDtypeStruct((DEPTH, MOD_ROWS, 6 * D_MODEL), F32),
        compiler_params=pltpu.CompilerParams(
            dimension_semantics=("parallel", "parallel"),
            vmem_limit_bytes=_vmem_limit(
                [((D_MODEL, bn), F32), ((MOD_ROWS, D_MODEL), F32), ((MOD_ROWS, bn), F32)],
                temps=[((D_MODEL, bn), BF16), ((D_MODEL, bn), F32)])),
        name="modulation",
    )(cvec, w_mod, b_mod.reshape(DEPTH, 1, 6 * D_MODEL))


def _mod_spec(layer, chunk, tm):
    return pl.BlockSpec((None, None, 1, D_MODEL),
                        lambda i, *_: (layer, _mod_group(i, tm), 0, chunk))


def _group_specs(tm, tn, n_col_tiles=None):
    nct = N_CTX // tm
    if n_col_tiles is None:
        return (pl.BlockSpec((tm, tn), lambda i, *_: (jnp.minimum(i, nct - 1), 0)),
                pl.BlockSpec((tm, tn), lambda i, *_: (jnp.maximum(i - nct, 0), 0)))
    last = n_col_tiles - 1
    return (pl.BlockSpec((tm, tn), lambda i, j: (jnp.minimum(i, nct - 1), jnp.where(i < nct, j, last))),
            pl.BlockSpec((tm, tn), lambda i, j: (jnp.maximum(i - nct, 0), jnp.where(i < nct, 0, j))))


def _group_tile(xc_ref, xl_ref, tm):
    return jnp.where(pl.program_id(0) < N_CTX // tm, xc_ref[...], xl_ref[...])


def _prenorm_kernel(xc_ref, xl_ref, g_ref, shift_ref, scale_ref, o_ref):
    x = _group_tile(xc_ref, xl_ref, TM_NORM)
    o_ref[...] = _mod_norm(x, g_ref[...], shift_ref[...], scale_ref[...]).astype(BF16)


def _pre_norm(x_pair, g, mod4, layer):
    return pl.pallas_call(
        _prenorm_kernel,
        grid=(N_TOK // TM_NORM,),
        in_specs=[
            *_group_specs(TM_NORM, D_MODEL),
            pl.BlockSpec((1, D_MODEL), lambda i: (0, 0)),
            _mod_spec(layer, 0, TM_NORM),
            _mod_spec(layer, 1, TM_NORM),
        ],
        out_specs=pl.BlockSpec((TM_NORM, D_MODEL), lambda i: (i, 0)),
        out_shape=jax.ShapeDtypeStruct((N_TOK, D_MODEL), BF16),
        compiler_params=pltpu.CompilerParams(
            dimension_semantics=("arbitrary",),
            vmem_limit_bytes=_vmem_limit(
                [((TM_NORM, D_MODEL), F32)] * 2 + [((TM_NORM, D_MODEL), BF16)],
                temps=[((TM_NORM, D_MODEL), F32)] * 3)),
        name="pre_norm",
    )(*x_pair, g, mod4, mod4)


N_CTX_TILES = N_CTX // TM_IN
KV_TILES = DA_WIDTH // TN


MIX_COLS = OFF_GATE
GATE_COLS = N_BRANCH * D_MODEL


def _cast_weight_tile(w_ref, wbf_scr):
    @pl.when(pl.program_id(1) == 0)
    def _():
        wbf_scr[...] = w_ref[...].astype(BF16)


def _inproj_mix_kernel(h_ref, w_ref, nk_hbm, nv_hbm, o_ref, nk_ref, nv_ref, wbf_scr):
    del nk_hbm, nv_hbm
    j, i = pl.program_id(0), pl.program_id(1)
    _cast_weight_tile(w_ref, wbf_scr)
    o_ref[...] = jnp.dot(h_ref[...], wbf_scr[...], preferred_element_type=F32)

    @pl.when((i < N_CTX_TILES) & (j >= OFF_K // TN) & (j < OFF_V // TN))
    def _():
        nk_ref[...] = o_ref[...].reshape(nk_ref.shape)

    @pl.when((i < N_CTX_TILES) & (j >= OFF_V // TN) & (j < OFF_SC // TN))
    def _():
        nv_ref[...] = o_ref[...].reshape(nv_ref.shape)


def _inproj_gate_kernel(h_ref, w_ref, o_ref, wbf_scr):
    _cast_weight_tile(w_ref, wbf_scr)
    o_ref[...] = jnp.dot(h_ref[...], wbf_scr[...], preferred_element_type=F32).astype(BF16)


def _kv_cache_spec(layer, col0):
    def index(j, i):
        rel = j - col0
        row = jnp.where(rel < 0, 0,
                        jnp.where(rel >= KV_TILES, N_CTX_TILES - 1, jnp.minimum(i, N_CTX_TILES - 1)))
        return (row, layer, 0, jnp.clip(rel, 0, KV_TILES - 1))
    return pl.BlockSpec((TM_IN // SEQ, None, SEQ, TN), index)


def _in_projection(h, layer, w_in, new_kv):
    h_spec = pl.BlockSpec((TM_IN, D_MODEL), lambda j, i: (i, 0))
    kv_sds = jax.ShapeDtypeStruct((BATCH, DEPTH, SEQ, DA_WIDTH), F32)
    kv_block = ((TM_IN // SEQ, SEQ, TN), F32)
    hbm = pl.BlockSpec(memory_space=pl.ANY)
    mix, new_k, new_v = pl.pallas_call(
        _inproj_mix_kernel,
        grid=(MIX_COLS // TN, N_TOK // TM_IN),
        in_specs=[h_spec, pl.BlockSpec((None, D_MODEL, TN), lambda j, i: (layer, 0, j)), hbm, hbm],
        out_specs=[pl.BlockSpec((TM_IN, TN), lambda j, i: (i, j)),
                   _kv_cache_spec(layer, OFF_K // TN), _kv_cache_spec(layer, OFF_V // TN)],
        out_shape=[jax.ShapeDtypeStruct((N_TOK, MIX_COLS), F32), kv_sds, kv_sds],
        scratch_shapes=[pltpu.VMEM((D_MODEL, TN), BF16)],
        input_output_aliases={2: 1, 3: 2},
        compiler_params=pltpu.CompilerParams(
            dimension_semantics=("arbitrary", "arbitrary"),
            vmem_limit_bytes=_vmem_limit(
                [((TM_IN, D_MODEL), BF16), ((D_MODEL, TN), F32), ((TM_IN, TN), F32), kv_block, kv_block],
                scratch=[((D_MODEL, TN), BF16)], temps=[((TM_IN, TN), F32)])),
        name="in_projection_mix",
    )(h, w_in, *new_kv)
    gates = pl.pallas_call(
        _inproj_gate_kernel,
        grid=(GATE_COLS // TN, N_TOK // TM_IN),
        in_specs=[h_spec, pl.BlockSpec((None, D_MODEL, TN), lambda j, i: (layer, 0, MIX_COLS // TN + j))],
        out_specs=pl.BlockSpec((TM_IN, TN), lambda j, i: (i, j)),
        out_shape=jax.ShapeDtypeStruct((N_TOK, GATE_COLS), BF16),
        scratch_shapes=[pltpu.VMEM((D_MODEL, TN), BF16)],
        compiler_params=pltpu.CompilerParams(
            dimension_semantics=("arbitrary", "arbitrary"),
            vmem_limit_bytes=_vmem_limit(
                [((TM_IN, D_MODEL), BF16), ((D_MODEL, TN), F32), ((TM_IN, TN), BF16)],
                scratch=[((D_MODEL, TN), BF16)], temps=[((TM_IN, TN), F32)])),
        name="in_projection_gates",
    )(h, w_in)
    return mix, gates, (new_k, new_v)


def _rope(x, cos, sin_lo, sin_hi):
    return (x * cos + pltpu.roll(x, LANES - 16, 1) * sin_lo + pltpu.roll(x, 16, 1) * sin_hi)


def _make_attn_kernel(seq_len, tq, n_cache, n_heads, use_rope, lam_init):
    def kernel(*refs):
        dl_ref, g_ref, q_ref, k_ref, v_ref = refs[:5]
        refs = refs[5:]
        if n_cache:
            ck_ref, cv_ref = refs[:2]
            refs = refs[2:]
        if use_rope:
            cos_ref, slo_ref, shi_ref = refs[:3]
            refs = refs[3:]
        o_ref, kk_scr, vv_scr = refs
        qb = pl.program_id(2)

        @pl.when(qb == 0)
        def _():
            for h in range(n_heads):
                cols = slice(h * LANES, (h + 1) * LANES)
                k = k_ref[:, cols]
                if use_rope:
                    k = _rope(k, cos_ref[...], slo_ref[...], shi_ref[...])
                if n_cache:
                    kk_scr[h, 0:n_cache, :] = ck_ref[:, cols].astype(BF16)
                    vv_scr[h, 0:n_cache, 0:DA_VDIM] = cv_ref[:, cols].astype(BF16)
                kk_scr[h, n_cache:n_cache + seq_len, :] = k.astype(BF16)
                vv_scr[h, n_cache:n_cache + seq_len, 0:DA_VDIM] = v_ref[:, cols].astype(BF16)
                vv_scr[h, :, DA_VDIM:] = jnp.ones((n_cache + seq_len, DA_VDIM), BF16)

        dl = dl_ref[...]
        lam = (jnp.exp(jnp.sum(dl[0:1] * dl[1:2], axis=-1, keepdims=True))
               - jnp.exp(jnp.sum(dl[2:3] * dl[3:4], axis=-1, keepdims=True)) + lam_init)
        tsub = min(tq, TQ_SUB)
        lane = lax.broadcasted_iota(jnp.int32, (tsub, LANES), 1)

        def softmax_times_v(qm, kk, vv):
            s = lax.dot_general(qm.astype(BF16), kk, (((1,), (1,)), ((), ())), preferred_element_type=F32)
            e = jnp.exp2(s - jnp.max(s, axis=-1, keepdims=True)).astype(BF16)
            ev = jnp.dot(e, vv, preferred_element_type=F32)
            return ev[:, :DA_VDIM] / ev[:, DA_VDIM:]

        for h in range(n_heads):
            cols = slice(h * LANES, (h + 1) * LANES)
            kk, vv = kk_scr[h], vv_scr[h]
            for c in range(tq // tsub):
                q = q_ref[c * tsub:(c + 1) * tsub, cols]
                if use_rope:
                    rows = pl.ds(pl.multiple_of(qb * tq, tq) + c * tsub, tsub)
                    q = _rope(q, cos_ref[rows, :], slo_ref[rows, :], shi_ref[rows, :])
                q = q * (DA_HEAD_DIM ** -0.5 * LOG2_E)
                o1 = softmax_times_v(jnp.where(lane < DA_HEAD_DIM, q, 0.0), kk, vv)
                o2 = softmax_times_v(jnp.where(lane >= DA_HEAD_DIM, q, 0.0), kk, vv)
                o_ref[c * tsub:(c + 1) * tsub, cols] = (
                    _rms(o1 - lam * o2, g_ref[...]) * (1.0 - lam_init)).astype(BF16)

    return kernel


def _diff_attention(proj, da_lambda_l, subln_g, layer, *, latent, cache=None, rope=None):
    lam_init = 0.8 - 0.6 * math.exp(-0.3 * layer)
    if latent:
        n_seq, seq_len, tq, n_cache, row0, n_heads = DEC_BATCH, DEC_SEQ, TQ_LAT, PAST_LEN, N_CTX, HEADS_LAT
    else:
        n_seq, seq_len, tq, n_cache, row0, n_heads = BATCH, SEQ, SEQ, 0, 0, HEADS_CTX
    nq = seq_len // tq
    lk = n_cache + seq_len
    width = n_heads * LANES
    hk, hv = OFF_K // width, OFF_V // width

    in_specs = [
        pl.BlockSpec((4, DA_HEAD_DIM), lambda b, h, t: (0, 0)),
        pl.BlockSpec((1, DA_VDIM), lambda b, h, t: (0, 0)),
        pl.BlockSpec((tq, width), lambda b, h, t: (row0 // tq + b * nq + t, h)),
        pl.BlockSpec((seq_len, width), lambda b, h, t: (row0 // seq_len + b, hk + h)),
        pl.BlockSpec((seq_len, width), lambda b, h, t: (row0 // seq_len + b, hv + h)),
    ]
    args = [da_lambda_l, subln_g, proj, proj, proj]
    blocks = [((tq, width), F32), ((seq_len, width), F32), ((seq_len, width), F32), ((tq, width), BF16)]
    if latent:
        cache_k, cache_v = cache
        in_specs += [pl.BlockSpec((None, None, n_cache, width), lambda b, h, t: (b, layer, 0, h))] * 2
        args += [cache_k, cache_v]
        in_specs += [pl.BlockSpec((seq_len, LANES), lambda b, h, t: (0, 0))] * 3
        args += list(rope)
        blocks += [((n_cache, width), F32)] * 2 + [((seq_len, LANES), F32)] * 3

    tsub = min(tq, TQ_SUB)
    scratch = [((n_heads, lk, LANES), BF16), ((n_heads, lk, 2 * DA_VDIM), BF16)]
    return pl.pallas_call(
        _make_attn_kernel(seq_len, tq, n_cache, n_heads, latent, lam_init),
        grid=(n_seq, DA_HEADS // n_heads, nq),
        in_specs=in_specs,
        out_specs=pl.BlockSpec((tq, width), lambda b, h, t: (b * nq + t, h)),
        out_shape=jax.ShapeDtypeStruct((n_seq * seq_len, DA_WIDTH), BF16),
        scratch_shapes=[pltpu.VMEM(s, d) for s, d in scratch],
        compiler_params=pltpu.CompilerParams(
            dimension_semantics=("parallel", "parallel", "arbitrary"),
            vmem_limit_bytes=_vmem_limit(
                blocks, scratch=scratch,
                temps=[((tsub, lk), F32)] * 6 * 4 + [((tq, 2 * DA_VDIM), F32)] * 4)),
        name="diff_attention_latent" if latent else "diff_attention_context",
    )(*args)


def _fill_shifted(scr, value, pad, seq_len, shifts):
    width = value.shape[-1]
    n = seq_len + 2 * pad - SUBLANES
    scr[0, 0:pad, :] = jnp.zeros((pad, width), F32)
    scr[0, pad + seq_len:, :] = jnp.zeros((pad, width), F32)
    scr[0, pad:pad + seq_len, :] = value
    for s, shift in enumerate(shifts, start=1):
        scr[s, 0:n, :] = scr[0, shift:shift + n, :]


def _tap(scr, shifts, r0, offset):
    aligned, rem = offset - offset % SUBLANES, offset % SUBLANES
    slot = 0 if rem == 0 else 1 + shifts.index(rem)
    return scr[slot, pl.ds(r0 + aligned, CONV_ROWS), :]


def _make_conv_kernel(seq_len):
    pad3, pad31 = SUBLANES, 2 * SUBLANES
    shifts3 = (1, SUBLANES - 1)
    shifts31 = tuple(range(1, SUBLANES))

    def kernel(gb_ref, gc_ref, sx_ref, ca_ref, cb_ref, w3_ref, w31_ref, b31_ref, lng_ref, lnb_ref,
               bo_ref, co_ref, t_scr, u_scr, y_scr):
        _fill_shifted(t_scr, gc_ref[...] * sx_ref[...], pad3, seq_len, shifts3)
        _fill_shifted(u_scr, ca_ref[...] * _sigmoid(cb_ref[...]), pad31, seq_len, shifts31)

        def chunk(c, carry):
            r0 = pl.multiple_of(c * CONV_ROWS, CONV_ROWS)
            rows = pl.ds(r0, CONV_ROWS)
            y = _tap(t_scr, shifts3, r0, pad3 - SC_K // 2) * w3_ref[0:1, :]
            for k in range(1, SC_K):
                y = y + _tap(t_scr, shifts3, r0, pad3 - SC_K // 2 + k) * w3_ref[k:k + 1, :]
            bo_ref[rows, :] = (gb_ref[rows, :] * y).astype(BF16)

            acc = _tap(u_scr, shifts31, r0, pad31 - CF_K // 2) * w31_ref[0:1, :]
            for k in range(1, CF_K):
                acc = acc + _tap(u_scr, shifts31, r0, pad31 - CF_K // 2 + k) * w31_ref[k:k + 1, :]
            y_scr[rows, :] = acc + b31_ref[...]
            return carry

        lax.fori_loop(0, seq_len // CONV_ROWS, chunk, 0)

        y = y_scr[...]
        xc = y - jnp.mean(y, axis=-1, keepdims=True)
        yn = xc * lax.rsqrt(jnp.mean(xc * xc, axis=-1, keepdims=True) + EPS) * lng_ref[...] + lnb_ref[...]
        co_ref[...] = (yn * _sigmoid(yn)).astype(BF16)

    return kernel


def _conv_branches(proj, sc_conv_l, cf_conv_l, cf_b, ln_g, ln_b, *, latent):
    if latent:
        n_seq, seq_len, row0 = DEC_BATCH, DEC_SEQ, N_CTX
    else:
        n_seq, seq_len, row0 = BATCH, SEQ, 0
    rb = row0 // seq_len
    c0 = OFF_SC // SC_WIDTH

    def col(cidx):
        return pl.BlockSpec((seq_len, SC_WIDTH), lambda b: (rb + b, cidx))

    def whole(shape):
        return pl.BlockSpec(shape, lambda b: (0, 0))

    in_specs = [col(c0), col(c0 + 1), col(c0 + 2), col(c0 + 3), col(c0 + 4),
                whole((SC_K, SC_WIDTH)), whole((CF_K, CF_WIDTH)),
                whole((1, CF_WIDTH)), whole((1, CF_WIDTH)), whole((1, CF_WIDTH))]
    args = [proj, proj, proj, proj, proj, sc_conv_l, cf_conv_l, cf_b, ln_g, ln_b]
    out_spec = pl.BlockSpec((seq_len, SC_WIDTH), lambda b: (b, 0))
    out_sds = jax.ShapeDtypeStruct((n_seq * seq_len, SC_WIDTH), BF16)
    scratch = [((3, seq_len + 2 * SUBLANES, SC_WIDTH), F32),
               ((SUBLANES, seq_len + 4 * SUBLANES, CF_WIDTH), F32),
               ((seq_len, CF_WIDTH), F32)]
    return pl.pallas_call(
        _make_conv_kernel(seq_len),
        grid=(n_seq,),
        in_specs=in_specs,
        out_specs=[out_spec, out_spec],
        out_shape=[out_sds, out_sds],
        scratch_shapes=[pltpu.VMEM(s, d) for s, d in scratch],
        compiler_params=pltpu.CompilerParams(
            dimension_semantics=("parallel",),
            vmem_limit_bytes=_vmem_limit(
                [((seq_len, SC_WIDTH), F32)] * 5 + [((seq_len, SC_WIDTH), BF16)] * 2,
                scratch=scratch,
                temps=[((seq_len, SC_WIDTH), F32)] * 4)),
        name="conv_branches_latent" if latent else "conv_branches_context",
    )(*args)


def _merge_kernel(ac_ref, al_ref, bc_ref, bl_ref, cc_ref, cl_ref, ga_ref, gb_ref, gc_ref,
                  bga_ref, bgb_ref, bgc_ref, wa_ref, wb_ref, wc_ref, o_ref):
    def merge(a_ref, b_ref, c_ref):
        br_a = jnp.dot(a_ref[...], wa_ref[...], preferred_element_type=F32)
        br_b = jnp.dot(b_ref[...], wb_ref[...], preferred_element_type=F32)
        br_c = jnp.dot(c_ref[...], wc_ref[...], preferred_element_type=F32)
        merged = (_sigmoid(ga_ref[...].astype(F32) + bga_ref[...]) * br_a
                  + _sigmoid(gb_ref[...].astype(F32) + bgb_ref[...]) * br_b
                  + _sigmoid(gc_ref[...].astype(F32) + bgc_ref[...]) * br_c)
        o_ref[...] = merged.astype(BF16)

    is_ctx = pl.program_id(0) < N_CTX // TM
    pl.when(is_ctx)(lambda: merge(ac_ref, bc_ref, cc_ref))
    pl.when(jnp.logical_not(is_ctx))(lambda: merge(al_ref, bl_ref, cl_ref))


def _gated_merge(attn_pair, b_pair, c_pair, gates, b_gate_l, layer, w_da, w_sc, w_cf):
    gstep = D_MODEL // TN

    def gate(k):
        return pl.BlockSpec((TM, TN), lambda i, j: (i, k * gstep + j))

    def gbias(k):
        return pl.BlockSpec((1, TN), lambda i, j: (0, k * gstep + j))

    def wcol(rows):
        return pl.BlockSpec((None, rows, TN), lambda i, j: (layer, 0, j))

    return pl.pallas_call(
        _merge_kernel,
        grid=(N_TOK // TM, gstep),
        in_specs=[
            *_group_specs(TM, DA_WIDTH), *_group_specs(TM, SC_WIDTH), *_group_specs(TM, CF_WIDTH),
            gate(0), gate(1), gate(2), gbias(0), gbias(1), gbias(2),
            wcol(DA_WIDTH), wcol(SC_WIDTH), wcol(CF_WIDTH),
        ],
        out_specs=pl.BlockSpec((TM, TN), lambda i, j: (i, j)),
        out_shape=jax.ShapeDtypeStruct((N_TOK, D_MODEL), BF16),
        compiler_params=pltpu.CompilerParams(
            dimension_semantics=("arbitrary", "arbitrary"),
            vmem_limit_bytes=_vmem_limit(
                [((TM, DA_WIDTH), BF16), ((TM, SC_WIDTH), BF16), ((TM, CF_WIDTH), BF16)] * 2
                + [((TM, TN), BF16)] * 3 + [((DA_WIDTH, TN), BF16), ((SC_WIDTH, TN), BF16),
                                            ((CF_WIDTH, TN), BF16), ((TM, TN), BF16)],
                temps=[((TM, TN), F32)] * 8 + [((TM, D_MODEL), BF16)])),
        name="gated_merge",
    )(*attn_pair, *b_pair, *c_pair, gates, gates, gates, b_gate_l, b_gate_l, b_gate_l, w_da, w_sc, w_cf)


def _outproj_kernel(m_ref, w_ref, xc_ref, xl_ref, gate_ref, o_ref):
    x = _group_tile(xc_ref, xl_ref, TM)
    o_ref[...] = x + gate_ref[...] * jnp.dot(m_ref[...], w_ref[...], preferred_element_type=F32)


def _out_projection(merged, w_out, x_pair, mod4, layer):
    n_col = D_MODEL // TN
    gate_spec = pl.BlockSpec((None, None, 1, TN),
                             lambda i, j: (layer, _mod_group(i, TM), 0, 2 * n_col + j))
    return pl.pallas_call(
        _outproj_kernel,
        grid=(N_TOK // TM, n_col),
        in_specs=[
            pl.BlockSpec((TM, D_MODEL), lambda i, j: (i, 0)),
            pl.BlockSpec((None, D_MODEL, TN), lambda i, j: (layer, 0, j)),
            *_group_specs(TM, TN, n_col),
            gate_spec,
        ],
        out_specs=pl.BlockSpec((TM, TN), lambda i, j: (i, j)),
        out_shape=jax.ShapeDtypeStruct((N_TOK, D_MODEL), F32),
        compiler_params=pltpu.CompilerParams(
            dimension_semantics=("arbitrary", "arbitrary"),
            vmem_limit_bytes=_vmem_limit(
                [((TM, D_MODEL), BF16), ((D_MODEL, TN), BF16)] + [((TM, TN), F32)] * 3,
                temps=[((TM, TN), F32)] * 3)),
        name="out_projection",
    )(merged, w_out, *x_pair, mod4)


def _make_ffn_kernel(final_norm):
    def kernel(*refs):
        x_ref, g_ref, shift_ref, scale_ref, gate_ref, wu_ref, ww_ref, wo_ref = refs[:8]
        refs = refs[8:]
        if final_norm:
            gf_ref = refs[0]
            refs = refs[1:]
        y_ctx_ref, y_lat_ref, h_scr, acc_scr = refs
        i, j = pl.program_id(0), pl.program_id(1)

        @pl.when(j == 0)
        def _():
            h_scr[...] = _mod_norm(x_ref[...], g_ref[...], shift_ref[...], scale_ref[...]).astype(BF16)
            acc_scr[...] = jnp.zeros_like(acc_scr)

        h = h_scr[...]
        u = jnp.dot(h, wu_ref[...], preferred_element_type=F32)
        w = jnp.dot(h, ww_ref[...], preferred_element_type=F32)
        act = (u * _sigmoid(u) * w).astype(BF16)
        acc_scr[...] += jnp.dot(act, wo_ref[...], preferred_element_type=F32)

        def result():
            y = x_ref[...] + gate_ref[...] * acc_scr[...]
            return _rms(y, gf_ref[...]) if final_norm else y

        last = j == pl.num_programs(1) - 1

        @pl.when(last & (i < N_CTX // TM_FFN))
        def _():
            y_ctx_ref[...] = result()

        @pl.when(last & (i >= N_CTX // TM_FFN))
        def _():
            y_lat_ref[...] = result()

    return kernel


def _swiglu(x, g, mod4, layer, w_ffn_in, w_ffn_out, g_final=None):
    final_norm = g_final is not None
    nh = FF_HIDDEN // TH_FFN
    n_ctx_tiles = N_CTX // TM_FFN
    row = pl.BlockSpec((1, D_MODEL), lambda i, j: (0, 0))
    in_specs = [
        pl.BlockSpec((TM_FFN, D_MODEL), lambda i, j: (i, 0)),
        row,
        _mod_spec(layer, 3, TM_FFN), _mod_spec(layer, 4, TM_FFN), _mod_spec(layer, 5, TM_FFN),
        pl.BlockSpec((None, D_MODEL, TH_FFN), lambda i, j: (layer, 0, j)),
        pl.BlockSpec((None, D_MODEL, TH_FFN), lambda i, j: (layer, 0, nh + j)),
        pl.BlockSpec((None, TH_FFN, D_MODEL), lambda i, j: (layer, j, 0)),
    ]
    args = [x, g, mod4, mod4, mod4, w_ffn_in, w_ffn_in, w_ffn_out]
    if final_norm:
        in_specs.append(row)
        args.append(g_final)
    out_specs = [pl.BlockSpec((TM_FFN, D_MODEL), lambda i, j: (jnp.minimum(i, n_ctx_tiles - 1), 0)),
                 pl.BlockSpec((TM_FFN, D_MODEL), lambda i, j: (jnp.maximum(i - n_ctx_tiles, 0), 0))]
    out_shape = [jax.ShapeDtypeStruct((N_CTX, D_MODEL), F32), jax.ShapeDtypeStruct((N_LAT, D_MODEL), F32)]
    scratch = [((TM_FFN, D_MODEL), BF16), ((TM_FFN, D_MODEL), F32)]
    return pl.pallas_call(
        _make_ffn_kernel(final_norm),
        grid=(N_TOK // TM_FFN, nh),
        in_specs=in_specs,
        out_specs=out_specs,
        out_shape=out_shape,
        scratch_shapes=[pltpu.VMEM(s, d) for s, d in scratch],
        compiler_params=pltpu.CompilerParams(
            dimension_semantics=("arbitrary", "arbitrary"),
            vmem_limit_bytes=_vmem_limit(
                [((TM_FFN, D_MODEL), F32)] * 3 + [((D_MODEL, TH_FFN), BF16)] * 2
                + [((TH_FFN, D_MODEL), BF16)],
                scratch=scratch,
                temps=[((TM_FFN, TH_FFN), F32)] * 4 + [((TM_FFN, D_MODEL), F32)] * 2)),
        name="swiglu_final" if final_norm else "swiglu",
    )(*args)


def _rope_tables():
    rows = DEC_SEQ // GRID_W
    row_ids = jnp.repeat(jnp.arange(rows), GRID_W).astype(F32)
    col_ids = jnp.tile(jnp.arange(GRID_W), rows).astype(F32)
    n_freq = DA_HEAD_DIM // 4
    inv = ROPE_THETA ** (-jnp.arange(n_freq, dtype=F32) / n_freq)
    ang_r, ang_c = row_ids[:, None] * inv, col_ids[:, None] * inv
    zero = jnp.zeros_like(ang_r)
    cos = jnp.concatenate([jnp.cos(ang_r)] * 2 + [jnp.cos(ang_c)] * 2, axis=-1)
    sin_lo = jnp.concatenate([-jnp.sin(ang_r), zero, -jnp.sin(ang_c), zero], axis=-1)
    sin_hi = jnp.concatenate([zero, jnp.sin(ang_r), zero, jnp.sin(ang_c)], axis=-1)
    return tuple(jnp.tile(t, (1, LANES // DA_HEAD_DIM)) for t in (cos, sin_lo, sin_hi))


def kernel(x_prompt, x_sample, cache_k, cache_v, c, c_ctx, w_mod, b_mod, g_norm1, w_in, da_lambda, da_subln,
           w_da_out, sc_conv, w_sc_out, cf_conv, cf_conv_b, cf_ln_g, cf_ln_b, w_cf_out, b_gate, w_out,
           g_norm2, w_ffn_in, w_ffn_out, g_final):
    x_pair = (x_prompt.reshape(N_CTX, D_MODEL), x_sample.reshape(N_LAT, D_MODEL))
    cvec =jnp.concatenate([c_ctx[None, :], c, jnp.zeros((MOD_ROWS - 1 - DEC_BATCH, D_MODEL), F32)], axis=0)
    mod4 = _modulation(cvec, w_mod, b_mod).reshape(DEPTH, MOD_ROWS, 1, 6 * D_MODEL)
    rope = _rope_tables()
    cache_k = cache_k.reshape(DEC_BATCH, DEPTH, PAST_LEN, DA_WIDTH)
    cache_v = cache_v.reshape(DEC_BATCH, DEPTH, PAST_LEN, DA_HEADS * DA_VDIM)

    w_ffn_in_bf, w_ffn_out_bf = w_ffn_in.astype(BF16), w_ffn_out.astype(BF16)
    w_da_bf, w_sc_bf, w_cf_bf, w_out_bf = (w.astype(BF16) for w in (w_da_out, w_sc_out, w_cf_out, w_out))

    new_kv = tuple(jnp.zeros((BATCH, DEPTH, SEQ, DA_WIDTH), F32) for _ in range(2))
    for l in range(DEPTH):
        h = _pre_norm(x_pair, g_norm1[l][None, :], mod4, l)
        proj, gates, new_kv = _in_projection(h, l, w_in, new_kv)

        subln = da_subln[l][None, :]
        attn_pair = (_diff_attention(proj, da_lambda[l], subln, l, latent=False),
                     _diff_attention(proj, da_lambda[l], subln, l, latent=True, cache=(cache_k, cache_v),
                                     rope=rope))
        conv_args = (sc_conv[l], cf_conv[l], cf_conv_b[l][None, :], cf_ln_g[l][None, :], cf_ln_b[l][None, :])
        b_ctx, c_ctx_pre = _conv_branches(proj, *conv_args, latent=False)
        b_lat, c_lat = _conv_branches(proj, *conv_args, latent=True)

        merged = _gated_merge(attn_pair, (b_ctx, b_lat), (c_ctx_pre, c_lat), gates, b_gate[l][None, :], l,
                              w_da_bf, w_sc_bf, w_cf_bf)
        x_mid = _out_projection(merged, w_out_bf, x_pair, mod4, l)
        x_pair = _swiglu(x_mid, g_norm2[l][None, :], mod4, l, w_ffn_in_bf, w_ffn_out_bf,
                         g_final[None, :] if l == DEPTH - 1 else None)

    y_prompt, y_sample = x_pair
    new_k, new_v = new_kv
    return (y_prompt.reshape(BATCH, SEQ, D_MODEL), y_sample.reshape(DEC_BATCH, DEC_SEQ, D_MODEL),
            new_k.reshape(BATCH, DEPTH, SEQ, DA_HEADS, 2 * DA_HEAD_DIM),
            new_v.reshape(BATCH, DEPTH, SEQ, DA_HEADS, DA_VDIM))
```

```python
import functools
import math

import jax
import jax.numpy as jnp
from jax import lax
from jax.experimental import pallas as pl
from jax.experimental.pallas import tpu as pltpu

D_MODEL = 2048
BATCH = 16
SEQ = 256
DEPTH = 2
DEC_BATCH = 8
DEC_SEQ = 1024
PAST_LEN = 256
GRID_W = 64
DA_HEADS = 8
DA_HEAD_DIM = 64
DA_VDIM = 2 * DA_HEAD_DIM
DA_WIDTH = DA_HEADS * 2 * DA_HEAD_DIM
SC_WIDTH = 512
SC_K = 3
CF_WIDTH = 512
CF_K = 31
N_BRANCH = 3
FF_HIDDEN = -(-8 * D_MODEL // (3 * 256)) * 256
ROPE_THETA = 10000.0
EPS = 1e-6

OFF_K = DA_WIDTH
OFF_V = 2 * DA_WIDTH
OFF_SC = 3 * DA_WIDTH
OFF_CF = OFF_SC + 3 * SC_WIDTH
OFF_GATE = OFF_CF + 2 * CF_WIDTH
IN_COLS = OFF_GATE + N_BRANCH * D_MODEL

N_CTX = BATCH * SEQ
N_LAT = DEC_BATCH * DEC_SEQ
N_TOK = N_CTX + N_LAT
MOD_ROWS = 16

F32 = jnp.float32
BF16 = jnp.bfloat16

VMEM_BUDGET_V7X = 56 * 1024 * 1024
LANES = 128
SUBLANES = 8

TM_NORM = 512
TM_IN = 2048
TM = 1024
TN = 512
TM_FFN = 512
TH_FFN = 512
TQ_LAT = 1024
HEADS_LAT = 2
HEADS_CTX = 8
TQ_SUB = 256
LOG2_E = math.log2(math.e)
NORM_ROWS = 128
CONV_ROWS = 32


def _nbytes(shape, dtype):
    return math.prod(shape) * jnp.dtype(dtype).itemsize


def _vmem_limit(blocks, scratch=(), temps=()):
    total = 2 * sum(_nbytes(s, d) for s, d in blocks)
    total += sum(_nbytes(s, d) for s, d in scratch)
    total += sum(_nbytes(s, d) for s, d in temps)
    return min(total, VMEM_BUDGET_V7X)


def _mod_group(i, tm):
    return jnp.maximum(i * tm - N_CTX + DEC_SEQ, 0) // DEC_SEQ


def _rms(x, g):
    return x * lax.rsqrt(jnp.mean(x * x, axis=-1, keepdims=True) + EPS) * g


def _mod_norm_rows(o_ref, x_refs, n_rows, g, shift, scale):
    gain = g * (1.0 + scale)
    is_ctx = pl.program_id(0) < N_CTX // n_rows

    def load(rows):
        if len(x_refs) == 2:
            return jnp.where(is_ctx, x_refs[0][rows, :], x_refs[1][rows, :])
        return x_refs[0][rows, :]

    for c in range(n_rows // NORM_ROWS):
        rows = slice(c * NORM_ROWS, (c + 1) * NORM_ROWS)
        x = load(rows)
        r = lax.rsqrt(jnp.mean(x * x, axis=-1, keepdims=True) + EPS)
        o_ref[rows, :] = ((load(rows) * r) * gain + shift).astype(BF16)


def _sigmoid(x):
    return 0.5 * jnp.tanh(0.5 * x) + 0.5


def _mod_kernel(c_ref, w_ref, b_ref, o_ref):
    c = c_ref[...]
    s = (c * _sigmoid(c)).astype(BF16)
    o_ref[...] = jnp.dot(s, w_ref[...].astype(BF16), preferred_element_type=F32) + b_ref[...]


def _modulation(cvec, w_mod, b_mod):
    bn = 1024
    return pl.pallas_call(
        _mod_kernel,
        grid=(DEPTH, 6 * D_MODEL // bn),
        in_specs=[
            pl.BlockSpec((MOD_ROWS, D_MODEL), lambda l, j: (0, 0)),
            pl.BlockSpec((None, D_MODEL, bn), lambda l, j: (l, 0, j)),
            pl.BlockSpec((None, 1, bn), lambda l, j: (l, 0, j)),
        ],
        out_specs=pl.BlockSpec((None, MOD_ROWS, bn), lambda l, j: (l, 0, j)),
        out_shape=jax.ShapeDtypeStruct((DEPTH, MOD_ROWS, 6 * D_MODEL), F32),
        compiler_params=pltpu.CompilerParams(
            dimension_semantics=("parallel", "parallel"),
            vmem_limit_bytes=_vmem_limit(
                [((D_MODEL, bn), F32), ((MOD_ROWS, D_MODEL), F32), ((MOD_ROWS, bn), F32)],
                temps=[((D_MODEL, bn), BF16), ((D_MODEL, bn), F32)])),
        name="modulation",
    )(cvec, w_mod, b_mod.reshape(DEPTH, 1, 6 * D_MODEL))


def _mod_spec(layer, chunk, tm):
    return pl.BlockSpec((None, None, 1, D_MODEL),
                        lambda i, *_: (layer, _mod_group(i, tm), 0, chunk))


def _group_specs(tm, tn, n_col_tiles=None):
    nct = N_CTX // tm
    if n_col_tiles is None:
        return (pl.BlockSpec((tm, tn), lambda i, *_: (jnp.minimum(i, nct - 1), 0)),
                pl.BlockSpec((tm, tn), lambda i, *_: (jnp.maximum(i - nct, 0), 0)))
    last = n_col_tiles - 1
    return (pl.BlockSpec((tm, tn), lambda i, j: (jnp.minimum(i, nct - 1), jnp.where(i < nct, j, last))),
            pl.BlockSpec((tm, tn), lambda i, j: (jnp.maximum(i - nct, 0), jnp.where(i < nct, 0, j))))


def _group_tile(xc_ref, xl_ref, tm):
    return jnp.where(pl.program_id(0) < N_CTX // tm, xc_ref[...], xl_ref[...])


def _prenorm_kernel(xc_ref, xl_ref, g_ref, shift_ref, scale_ref, o_ref):
    _mod_norm_rows(o_ref, (xc_ref, xl_ref), TM_NORM, g_ref[...], shift_ref[...], scale_ref[...])


def _pre_norm(x_pair, g, mod4, layer):
    return pl.pallas_call(
        _prenorm_kernel,
        grid=(N_TOK // TM_NORM,),
        in_specs=[
            *_group_specs(TM_NORM, D_MODEL),
            pl.BlockSpec((1, D_MODEL), lambda i: (0, 0)),
            _mod_spec(layer, 0, TM_NORM),
            _mod_spec(layer, 1, TM_NORM),
        ],
        out_specs=pl.BlockSpec((TM_NORM, D_MODEL), lambda i: (i, 0)),
        out_shape=jax.ShapeDtypeStruct((N_TOK, D_MODEL), BF16),
        compiler_params=pltpu.CompilerParams(
            dimension_semantics=("arbitrary",),
            vmem_limit_bytes=_vmem_limit(
                [((TM_NORM, D_MODEL), F32)] * 2 + [((TM_NORM, D_MODEL), BF16)],
                temps=[((TM_NORM, D_MODEL), F32)] * 3)),
        name="pre_norm",
    )(*x_pair, g, mod4, mod4)


N_CTX_TILES = N_CTX // TM_IN
KV_TILES = DA_WIDTH // TN


MIX_COLS = OFF_GATE
GATE_COLS = N_BRANCH * D_MODEL


def _cast_weight_tile(w_ref, wbf_scr):
    @pl.when(pl.program_id(1) == 0)
    def _():
        wbf_scr[...] = w_ref[...].astype(BF16)


def _inproj_mix_kernel(h_ref, w_ref, nk_hbm, nv_hbm, o_ref, nk_ref, nv_ref, wbf_scr):
    del nk_hbm, nv_hbm
    j, i = pl.program_id(0), pl.program_id(1)
    _cast_weight_tile(w_ref, wbf_scr)
    o_ref[...] = jnp.dot(h_ref[...], wbf_scr[...], preferred_element_type=F32)

    @pl.when((i < N_CTX_TILES) & (j >= OFF_K // TN) & (j < OFF_V // TN))
    def _():
        nk_ref[...] = o_ref[...].reshape(nk_ref.shape)

    @pl.when((i < N_CTX_TILES) & (j >= OFF_V // TN) & (j < OFF_SC // TN))
    def _():
        nv_ref[...] = o_ref[...].reshape(nv_ref.shape)


def _inproj_gate_kernel(h_ref, w_ref, o_ref, wbf_scr):
    _cast_weight_tile(w_ref, wbf_scr)
    o_ref[...] = jnp.dot(h_ref[...], wbf_scr[...], preferred_element_type=F32).astype(BF16)


def _kv_cache_spec(layer, col0):
    def index(j, i):
        rel = j - col0
        row = jnp.where(rel < 0, 0,
                        jnp.where(rel >= KV_TILES, N_CTX_TILES - 1, jnp.minimum(i, N_CTX_TILES - 1)))
        return (row, layer, 0, jnp.clip(rel, 0, KV_TILES - 1))
    return pl.BlockSpec((TM_IN // SEQ, None, SEQ, TN), index)


def _in_projection(h, layer, w_in, new_kv):
    h_spec = pl.BlockSpec((TM_IN, D_MODEL), lambda j, i: (i, 0))
    kv_sds = jax.ShapeDtypeStruct((BATCH, DEPTH, SEQ, DA_WIDTH), F32)
    kv_block = ((TM_IN // SEQ, SEQ, TN), F32)
    hbm = pl.BlockSpec(memory_space=pl.ANY)
    mix, new_k, new_v = pl.pallas_call(
        _inproj_mix_kernel,
        grid=(MIX_COLS // TN, N_TOK // TM_IN),
        in_specs=[h_spec, pl.BlockSpec((None, D_MODEL, TN), lambda j, i: (layer, 0, j)), hbm, hbm],
        out_specs=[pl.BlockSpec((TM_IN, TN), lambda j, i: (i, j)),
                   _kv_cache_spec(layer, OFF_K // TN), _kv_cache_spec(layer, OFF_V // TN)],
        out_shape=[jax.ShapeDtypeStruct((N_TOK, MIX_COLS), F32), kv_sds, kv_sds],
        scratch_shapes=[pltpu.VMEM((D_MODEL, TN), BF16)],
        input_output_aliases={2: 1, 3: 2},
        compiler_params=pltpu.CompilerParams(
            dimension_semantics=("arbitrary", "arbitrary"),
            vmem_limit_bytes=_vmem_limit(
                [((TM_IN, D_MODEL), BF16), ((D_MODEL, TN), F32), ((TM_IN, TN), F32), kv_block, kv_block],
                scratch=[((D_MODEL, TN), BF16)], temps=[((TM_IN, TN), F32)])),
        name="in_projection_mix",
    )(h, w_in, *new_kv)
    gates = pl.pallas_call(
        _inproj_gate_kernel,
        grid=(GATE_COLS // TN, N_TOK // TM_IN),
        in_specs=[h_spec, pl.BlockSpec((None, D_MODEL, TN), lambda j, i: (layer, 0, MIX_COLS // TN + j))],
        out_specs=pl.BlockSpec((TM_IN, TN), lambda j, i: (i, j)),
        out_shape=jax.ShapeDtypeStruct((N_TOK, GATE_COLS), BF16),
        scratch_shapes=[pltpu.VMEM((D_MODEL, TN), BF16)],
        compiler_params=pltpu.CompilerParams(
            dimension_semantics=("arbitrary", "arbitrary"),
            vmem_limit_bytes=_vmem_limit(
                [((TM_IN, D_MODEL), BF16), ((D_MODEL, TN), F32), ((TM_IN, TN), BF16)],
                scratch=[((D_MODEL, TN), BF16)], temps=[((TM_IN, TN), F32)])),
        name="in_projection_gates",
    )(h, w_in)
    return mix, gates, (new_k, new_v)


def _rope(x, cos, sin_lo, sin_hi):
    return (x * cos + pltpu.roll(x, LANES - 16, 1) * sin_lo + pltpu.roll(x, 16, 1) * sin_hi)


def _make_attn_kernel(seq_len, tq, n_cache, n_heads, use_rope, lam_init):
    def kernel(*refs):
        dl_ref, g_ref, q_ref, k_ref, v_ref = refs[:5]
        refs = refs[5:]
        if n_cache:
            ck_ref, cv_ref = refs[:2]
            refs = refs[2:]
        if use_rope:
            cos_ref, slo_ref, shi_ref = refs[:3]
            refs = refs[3:]
        o_ref, kk_scr, vv_scr = refs
        qb = pl.program_id(2)

        @pl.when(qb == 0)
        def _():
            for h in range(n_heads):
                cols = slice(h * LANES, (h + 1) * LANES)
                k = k_ref[:, cols]
                if use_rope:
                    k = _rope(k, cos_ref[...], slo_ref[...], shi_ref[...])
                if n_cache:
                    head = pl.program_id(1) * n_heads + h
                    rows = pl.ds(head, n_cache, stride=DA_HEADS)
                    kk_scr[h, 0:n_cache, :] = ck_ref[rows, :].astype(BF16)
                    vv_scr[h, 0:n_cache, 0:DA_VDIM] = cv_ref[rows, :].astype(BF16)
                kk_scr[h, n_cache:n_cache + seq_len, :] = k.astype(BF16)
                vv_scr[h, n_cache:n_cache + seq_len, 0:DA_VDIM] = v_ref[:, cols].astype(BF16)
                vv_scr[h, :, DA_VDIM:] = jnp.ones((n_cache + seq_len, DA_VDIM), BF16)

        dl = dl_ref[...]
        lam = (jnp.exp(jnp.sum(dl[0:1] * dl[1:2], axis=-1, keepdims=True))
               - jnp.exp(jnp.sum(dl[2:3] * dl[3:4], axis=-1, keepdims=True)) + lam_init)
        tsub = min(tq, TQ_SUB)
        lane = lax.broadcasted_iota(jnp.int32, (tsub, LANES), 1)

        def softmax_times_v(qm, kk, vv):
            s = lax.dot_general(qm.astype(BF16), kk, (((1,), (1,)), ((), ())), preferred_element_type=F32)
            e = jnp.exp2(s - jnp.max(s, axis=-1, keepdims=True)).astype(BF16)
            ev = jnp.dot(e, vv, preferred_element_type=F32)
            return ev[:, :DA_VDIM] / ev[:, DA_VDIM:]

        for h in range(n_heads):
            cols = slice(h * LANES, (h + 1) * LANES)
            kk, vv = kk_scr[h], vv_scr[h]
            for c in range(tq // tsub):
                q = q_ref[c * tsub:(c + 1) * tsub, cols]
                if use_rope:
                    rows = pl.ds(pl.multiple_of(qb * tq, tq) + c * tsub, tsub)
                    q = _rope(q, cos_ref[rows, :], slo_ref[rows, :], shi_ref[rows, :])
                q = q * (DA_HEAD_DIM ** -0.5 * LOG2_E)
                o1 = softmax_times_v(jnp.where(lane < DA_HEAD_DIM, q, 0.0), kk, vv)
                o2 = softmax_times_v(jnp.where(lane >= DA_HEAD_DIM, q, 0.0), kk, vv)
                o_ref[c * tsub:(c + 1) * tsub, cols] = (
                    _rms(o1 - lam * o2, g_ref[...]) * (1.0 - lam_init)).astype(BF16)

    return kernel


def _diff_attention(proj, da_lambda_l, subln_g, layer, *, latent, cache=None, rope=None):
    lam_init = 0.8 - 0.6 * math.exp(-0.3 * layer)
    if latent:
        n_seq, seq_len, tq, n_cache, row0, n_heads = DEC_BATCH, DEC_SEQ, TQ_LAT, PAST_LEN, N_CTX, HEADS_LAT
    else:
        n_seq, seq_len, tq, n_cache, row0, n_heads = BATCH, SEQ, SEQ, 0, 0, HEADS_CTX
    nq = seq_len // tq
    lk = n_cache + seq_len
    width = n_heads * LANES
    hk, hv = OFF_K // width, OFF_V // width

    in_specs = [
        pl.BlockSpec((4, DA_HEAD_DIM), lambda b, h, t: (0, 0)),
        pl.BlockSpec((1, DA_VDIM), lambda b, h, t: (0, 0)),
        pl.BlockSpec((tq, width), lambda b, h, t: (row0 // tq + b * nq + t, h)),
        pl.BlockSpec((seq_len, width), lambda b, h, t: (row0 // seq_len + b, hk + h)),
        pl.BlockSpec((seq_len, width), lambda b, h, t: (row0 // seq_len + b, hv + h)),
    ]
    args = [da_lambda_l, subln_g, proj, proj, proj]
    blocks = [((tq, width), F32), ((seq_len, width), F32), ((seq_len, width), F32), ((tq, width), BF16)]
    if latent:
        cache_k, cache_v = cache
        cache_block = (n_cache * DA_HEADS, LANES)
        in_specs += [pl.BlockSpec((None, None) + cache_block, lambda b, h, t: (b, layer, 0, 0))] * 2
        args += [cache_k, cache_v]
        in_specs += [pl.BlockSpec((seq_len, LANES), lambda b, h, t: (0, 0))] * 3
        args += list(rope)
        blocks += [(cache_block, F32)] * 2 + [((seq_len, LANES), F32)] * 3

    tsub = min(tq, TQ_SUB)
    scratch = [((n_heads, lk, LANES), BF16), ((n_heads, lk, 2 * DA_VDIM), BF16)]
    return pl.pallas_call(
        _make_attn_kernel(seq_len, tq, n_cache, n_heads, latent, lam_init),
        grid=(n_seq, DA_HEADS // n_heads, nq),
        in_specs=in_specs,
        out_specs=pl.BlockSpec((tq, width), lambda b, h, t: (b * nq + t, h)),
        out_shape=jax.ShapeDtypeStruct((n_seq * seq_len, DA_WIDTH), BF16),
        scratch_shapes=[pltpu.VMEM(s, d) for s, d in scratch],
        compiler_params=pltpu.CompilerParams(
            dimension_semantics=("parallel", "parallel", "arbitrary"),
            vmem_limit_bytes=_vmem_limit(
                blocks, scratch=scratch,
                temps=[((tsub, lk), F32)] * 6 * 4 + [((tq, 2 * DA_VDIM), F32)] * 4)),
        name="diff_attention_latent" if latent else "diff_attention_context",
    )(*args)


def _fill_shifted(scr, value, pad, seq_len, shifts):
    width = value.shape[-1]
    n = seq_len + 2 * pad - SUBLANES
    scr[0, 0:pad, :] = jnp.zeros((pad, width), F32)
    scr[0, pad + seq_len:, :] = jnp.zeros((pad, width), F32)
    scr[0, pad:pad + seq_len, :] = value
    for s, shift in enumerate(shifts, start=1):
        scr[s, 0:n, :] = scr[0, shift:shift + n, :]


def _tap(scr, shifts, r0, offset):
    aligned, rem = offset - offset % SUBLANES, offset % SUBLANES
    slot = 0 if rem == 0 else 1 + shifts.index(rem)
    return scr[slot, pl.ds(r0 + aligned, CONV_ROWS), :]


def _make_conv_kernel(seq_len):
    pad3, pad31 = SUBLANES, 2 * SUBLANES
    shifts3 = (1, SUBLANES - 1)
    shifts31 = tuple(range(1, SUBLANES))

    def kernel(gb_ref, gc_ref, sx_ref, ca_ref, cb_ref, w3_ref, w31_ref, b31_ref, lng_ref, lnb_ref,
               bo_ref, co_ref, t_scr, u_scr, y_scr):
        _fill_shifted(t_scr, gc_ref[...] * sx_ref[...], pad3, seq_len, shifts3)
        _fill_shifted(u_scr, ca_ref[...] * _sigmoid(cb_ref[...]), pad31, seq_len, shifts31)

        def chunk(c, carry):
            r0 = pl.multiple_of(c * CONV_ROWS, CONV_ROWS)
            rows = pl.ds(r0, CONV_ROWS)
            y = _tap(t_scr, shifts3, r0, pad3 - SC_K // 2) * w3_ref[0:1, :]
            for k in range(1, SC_K):
                y = y + _tap(t_scr, shifts3, r0, pad3 - SC_K // 2 + k) * w3_ref[k:k + 1, :]
            bo_ref[rows, :] = (gb_ref[rows, :] * y).astype(BF16)

            acc = _tap(u_scr, shifts31, r0, pad31 - CF_K // 2) * w31_ref[0:1, :]
            for k in range(1, CF_K):
                acc = acc + _tap(u_scr, shifts31, r0, pad31 - CF_K // 2 + k) * w31_ref[k:k + 1, :]
            y_scr[rows, :] = acc + b31_ref[...]
            return carry

        lax.fori_loop(0, seq_len // CONV_ROWS, chunk, 0)

        y = y_scr[...]
        xc = y - jnp.mean(y, axis=-1, keepdims=True)
        yn = xc * lax.rsqrt(jnp.mean(xc * xc, axis=-1, keepdims=True) + EPS) * lng_ref[...] + lnb_ref[...]
        co_ref[...] = (yn * _sigmoid(yn)).astype(BF16)

    return kernel


def _conv_branches(proj, sc_conv_l, cf_conv_l, cf_b, ln_g, ln_b, *, latent):
    if latent:
        n_seq, seq_len, row0 = DEC_BATCH, DEC_SEQ, N_CTX
    else:
        n_seq, seq_len, row0 = BATCH, SEQ, 0
    rb = row0 // seq_len
    c0 = OFF_SC // SC_WIDTH

    def col(cidx):
        return pl.BlockSpec((seq_len, SC_WIDTH), lambda b: (rb + b, cidx))

    def whole(shape):
        return pl.BlockSpec(shape, lambda b: (0, 0))

    in_specs = [col(c0), col(c0 + 1), col(c0 + 2), col(c0 + 3), col(c0 + 4),
                whole((SC_K, SC_WIDTH)), whole((CF_K, CF_WIDTH)),
                whole((1, CF_WIDTH)), whole((1, CF_WIDTH)), whole((1, CF_WIDTH))]
    args = [proj, proj, proj, proj, proj, sc_conv_l, cf_conv_l, cf_b, ln_g, ln_b]
    out_spec = pl.BlockSpec((seq_len, SC_WIDTH), lambda b: (b, 0))
    out_sds = jax.ShapeDtypeStruct((n_seq * seq_len, SC_WIDTH), BF16)
    scratch = [((3, seq_len + 2 * SUBLANES, SC_WIDTH), F32),
               ((SUBLANES, seq_len + 4 * SUBLANES, CF_WIDTH), F32),
               ((seq_len, CF_WIDTH), F32)]
    return pl.pallas_call(
        _make_conv_kernel(seq_len),
        grid=(n_seq,),
        in_specs=in_specs,
        out_specs=[out_spec, out_spec],
        out_shape=[out_sds, out_sds],
        scratch_shapes=[pltpu.VMEM(s, d) for s, d in scratch],
        compiler_params=pltpu.CompilerParams(
            dimension_semantics=("parallel",),
            vmem_limit_bytes=_vmem_limit(
                [((seq_len, SC_WIDTH), F32)] * 5 + [((seq_len, SC_WIDTH), BF16)] * 2,
                scratch=scratch,
                temps=[((seq_len, SC_WIDTH), F32)] * 4)),
        name="conv_branches_latent" if latent else "conv_branches_context",
    )(*args)


def _merge_kernel(ac_ref, al_ref, bc_ref, bl_ref, cc_ref, cl_ref, ga_ref, gb_ref, gc_ref,
                  bga_ref, bgb_ref, bgc_ref, wa_ref, wb_ref, wc_ref, o_ref):
    def merge(a_ref, b_ref, c_ref):
        br_a = jnp.dot(a_ref[...], wa_ref[...], preferred_element_type=F32)
        br_b = jnp.dot(b_ref[...], wb_ref[...], preferred_element_type=F32)
        br_c = jnp.dot(c_ref[...], wc_ref[...], preferred_element_type=F32)
        merged = (_sigmoid(ga_ref[...].astype(F32) + bga_ref[...]) * br_a
                  + _sigmoid(gb_ref[...].astype(F32) + bgb_ref[...]) * br_b
                  + _sigmoid(gc_ref[...].astype(F32) + bgc_ref[...]) * br_c)
        o_ref[...] = merged.astype(BF16)

    is_ctx = pl.program_id(0) < N_CTX // TM
    pl.when(is_ctx)(lambda: merge(ac_ref, bc_ref, cc_ref))
    pl.when(jnp.logical_not(is_ctx))(lambda: merge(al_ref, bl_ref, cl_ref))


def _gated_merge(attn_pair, b_pair, c_pair, gates, b_gate_l, layer, w_da, w_sc, w_cf):
    gstep = D_MODEL // TN

    def gate(k):
        return pl.BlockSpec((TM, TN), lambda i, j: (i, k * gstep + j))

    def gbias(k):
        return pl.BlockSpec((1, TN), lambda i, j: (0, k * gstep + j))

    def wcol(rows):
        return pl.BlockSpec((None, rows, TN), lambda i, j: (layer, 0, j))

    return pl.pallas_call(
        _merge_kernel,
        grid=(N_TOK // TM, gstep),
        in_specs=[
            *_group_specs(TM, DA_WIDTH), *_group_specs(TM, SC_WIDTH), *_group_specs(TM, CF_WIDTH),
            gate(0), gate(1), gate(2), gbias(0), gbias(1), gbias(2),
            wcol(DA_WIDTH), wcol(SC_WIDTH), wcol(CF_WIDTH),
        ],
        out_specs=pl.BlockSpec((TM, TN), lambda i, j: (i, j)),
        out_shape=jax.ShapeDtypeStruct((N_TOK, D_MODEL), BF16),
        compiler_params=pltpu.CompilerParams(
            dimension_semantics=("arbitrary", "arbitrary"),
            vmem_limit_bytes=_vmem_limit(
                [((TM, DA_WIDTH), BF16), ((TM, SC_WIDTH), BF16), ((TM, CF_WIDTH), BF16)] * 2
                + [((TM, TN), BF16)] * 3 + [((DA_WIDTH, TN), BF16), ((SC_WIDTH, TN), BF16),
                                            ((CF_WIDTH, TN), BF16), ((TM, TN), BF16)],
                temps=[((TM, TN), F32)] * 8 + [((TM, D_MODEL), BF16)])),
        name="gated_merge",
    )(*attn_pair, *b_pair, *c_pair, gates, gates, gates, b_gate_l, b_gate_l, b_gate_l, w_da, w_sc, w_cf)


def _outproj_kernel(m_ref, w_ref, xc_ref, xl_ref, gate_ref, o_ref):
    x = _group_tile(xc_ref, xl_ref, TM)
    o_ref[...] = x + gate_ref[...] * jnp.dot(m_ref[...], w_ref[...], preferred_element_type=F32)


def _out_projection(merged, w_out, x_pair, mod4, layer):
    n_col = D_MODEL // TN
    gate_spec = pl.BlockSpec((None, None, 1, TN),
                             lambda i, j: (layer, _mod_group(i, TM), 0, 2 * n_col + j))
    return pl.pallas_call(
        _outproj_kernel,
        grid=(N_TOK // TM, n_col),
        in_specs=[
            pl.BlockSpec((TM, D_MODEL), lambda i, j: (i, 0)),
            pl.BlockSpec((None, D_MODEL, TN), lambda i, j: (layer, 0, j)),
            *_group_specs(TM, TN, n_col),
            gate_spec,
        ],
        out_specs=pl.BlockSpec((TM, TN), lambda i, j: (i, j)),
        out_shape=jax.ShapeDtypeStruct((N_TOK, D_MODEL), F32),
        compiler_params=pltpu.CompilerParams(
            dimension_semantics=("arbitrary", "arbitrary"),
            vmem_limit_bytes=_vmem_limit(
                [((TM, D_MODEL), BF16), ((D_MODEL, TN), BF16)] + [((TM, TN), F32)] * 3,
                temps=[((TM, TN), F32)] * 3)),
        name="out_projection",
    )(merged, w_out, *x_pair, mod4)


def _make_ffn_kernel(final_norm):
    def kernel(*refs):
        x_ref, g_ref, shift_ref, scale_ref, gate_ref, wu_ref, ww_ref, wo_ref = refs[:8]
        refs = refs[8:]
        if final_norm:
            gf_ref = refs[0]
            refs = refs[1:]
        y_ctx_ref, y_lat_ref, h_scr, acc_scr, act_scr = refs
        i, j = pl.program_id(0), pl.program_id(1)
        last = j == pl.num_programs(1) - 1

        def up(slot):
            h = h_scr[...]
            u = jnp.dot(h, wu_ref[...], preferred_element_type=F32)
            w = jnp.dot(h, ww_ref[...], preferred_element_type=F32)
            act_scr[slot] = (u * _sigmoid(u) * w).astype(BF16)

        def down(slot):
            return jnp.dot(act_scr[slot], wo_ref[...], preferred_element_type=F32)

        @pl.when(j == 0)
        def _():
            _mod_norm_rows(h_scr, (x_ref,), TM_FFN, g_ref[...], shift_ref[...], scale_ref[...])
            up(0)

        @pl.when(j == 1)
        def _():
            acc_scr[...] = down(0)
            up(1)

        @pl.when((j > 1) & jnp.logical_not(last))
        def _():
            acc_scr[...] += down((j - 1) % 2)
            up(j % 2)

        @pl.when(last)
        def _():
            acc_scr[...] += down((j - 1) % 2)

        def result():
            y = x_ref[...] + gate_ref[...] * acc_scr[...]
            return _rms(y, gf_ref[...]) if final_norm else y


        @pl.when(last & (i < N_CTX // TM_FFN))
        def _():
            y_ctx_ref[...] = result()

        @pl.when(last & (i >= N_CTX // TM_FFN))
        def _():
            y_lat_ref[...] = result()

    return kernel


def _swiglu(x, g, mod4, layer, w_ffn_in, w_ffn_out, g_final=None):
    final_norm = g_final is not None
    nh = FF_HIDDEN // TH_FFN
    n_ctx_tiles = N_CTX // TM_FFN
    row = pl.BlockSpec((1, D_MODEL), lambda i, j: (0, 0))
    in_specs = [
        pl.BlockSpec((TM_FFN, D_MODEL), lambda i, j: (i, 0)),
        row,
        _mod_spec(layer, 3, TM_FFN), _mod_spec(layer, 4, TM_FFN), _mod_spec(layer, 5, TM_FFN),
        pl.BlockSpec((None, D_MODEL, TH_FFN), lambda i, j: (layer, 0, jnp.minimum(j, nh - 1))),
        pl.BlockSpec((None, D_MODEL, TH_FFN), lambda i, j: (layer, 0, nh + jnp.minimum(j, nh - 1))),
        pl.BlockSpec((None, TH_FFN, D_MODEL), lambda i, j: (layer, jnp.maximum(j - 1, 0), 0)),
    ]
    args = [x, g, mod4, mod4, mod4, w_ffn_in, w_ffn_in, w_ffn_out]
    if final_norm:
        in_specs.append(row)
        args.append(g_final)
    out_specs = [pl.BlockSpec((TM_FFN, D_MODEL), lambda i, j: (jnp.minimum(i, n_ctx_tiles - 1), 0)),
                 pl.BlockSpec((TM_FFN, D_MODEL), lambda i, j: (jnp.maximum(i - n_ctx_tiles, 0), 0))]
    out_shape = [jax.ShapeDtypeStruct((N_CTX, D_MODEL), F32), jax.ShapeDtypeStruct((N_LAT, D_MODEL), F32)]
    scratch = [((TM_FFN, D_MODEL), BF16), ((TM_FFN, D_MODEL), F32), ((2, TM_FFN, TH_FFN), BF16)]
    return pl.pallas_call(
        _make_ffn_kernel(final_norm),
        grid=(N_TOK // TM_FFN, nh + 1),
        in_specs=in_specs,
        out_specs=out_specs,
        out_shape=out_shape,
        scratch_shapes=[pltpu.VMEM(s, d) for s, d in scratch],
        compiler_params=pltpu.CompilerParams(
            dimension_semantics=("arbitrary", "arbitrary"),
            vmem_limit_bytes=_vmem_limit(
                [((TM_FFN, D_MODEL), F32)] * 3 + [((D_MODEL, TH_FFN), BF16)] * 2
                + [((TH_FFN, D_MODEL), BF16)],
                scratch=scratch,
                temps=[((TM_FFN, TH_FFN), F32)] * 4 + [((TM_FFN, D_MODEL), F32)] * 2)),
        name="swiglu_final" if final_norm else "swiglu",
    )(*args)


def _rope_tables():
    rows = DEC_SEQ // GRID_W
    row_ids = jnp.repeat(jnp.arange(rows), GRID_W).astype(F32)
    col_ids = jnp.tile(jnp.arange(GRID_W), rows).astype(F32)
    n_freq = DA_HEAD_DIM // 4
    inv = ROPE_THETA ** (-jnp.arange(n_freq, dtype=F32) / n_freq)
    ang_r, ang_c = row_ids[:, None] * inv, col_ids[:, None] * inv
    zero = jnp.zeros_like(ang_r)
    cos = jnp.concatenate([jnp.cos(ang_r)] * 2 + [jnp.cos(ang_c)] * 2, axis=-1)
    sin_lo = jnp.concatenate([-jnp.sin(ang_r), zero, -jnp.sin(ang_c), zero], axis=-1)
    sin_hi = jnp.concatenate([zero, jnp.sin(ang_r), zero, jnp.sin(ang_c)], axis=-1)
    return tuple(jnp.tile(t, (1, LANES // DA_HEAD_DIM)) for t in (cos, sin_lo, sin_hi))


def kernel(x_prompt, x_sample, cache_k, cache_v, c, c_ctx, w_mod, b_mod, g_norm1, w_in, da_lambda, da_subln,
           w_da_out, sc_conv, w_sc_out, cf_conv, cf_conv_b, cf_ln_g, cf_ln_b, w_cf_out, b_gate, w_out,
           g_norm2, w_ffn_in, w_ffn_out, g_final):
    x_pair = (x_prompt.reshape(N_CTX, D_MODEL), x_sample.reshape(N_LAT, D_MODEL))
    cvec =jnp.concatenate([c_ctx[None, :], c, jnp.zeros((MOD_ROWS - 1 - DEC_BATCH, D_MODEL), F32)], axis=0)
    mod4 = _modulation(cvec, w_mod, b_mod).reshape(DEPTH, MOD_ROWS, 1, 6 * D_MODEL)
    rope = _rope_tables()
    cache_k = cache_k.reshape(DEC_BATCH, DEPTH, PAST_LEN * DA_HEADS, 2 * DA_HEAD_DIM)
    cache_v = cache_v.reshape(DEC_BATCH, DEPTH, PAST_LEN * DA_HEADS, DA_VDIM)

    w_ffn_in_bf, w_ffn_out_bf = w_ffn_in.astype(BF16), w_ffn_out.astype(BF16)
    w_da_bf, w_sc_bf, w_cf_bf, w_out_bf = (w.astype(BF16) for w in (w_da_out, w_sc_out, w_cf_out, w_out))

    new_kv = tuple(jnp.zeros((BATCH, DEPTH, SEQ, DA_WIDTH), F32) for _ in range(2))
    for l in range(DEPTH):
        h = _pre_norm(x_pair, g_norm1[l][None, :], mod4, l)
        proj, gates, new_kv = _in_projection(h, l, w_in, new_kv)

        subln = da_subln[l][None, :]
        attn_pair = (_diff_attention(proj, da_lambda[l], subln, l, latent=False),
                     _diff_attention(proj, da_lambda[l], subln, l, latent=True, cache=(cache_k, cache_v),
                                     rope=rope))
        conv_args = (sc_conv[l], cf_conv[l], cf_conv_b[l][None, :], cf_ln_g[l][None, :], cf_ln_b[l][None, :])
        b_ctx, c_ctx_pre = _conv_branches(proj, *conv_args, latent=False)
        b_lat, c_lat = _conv_branches(proj, *conv_args, latent=True)

        merged = _gated_merge(attn_pair, (b_ctx, b_lat), (c_ctx_pre, c_lat), gates, b_gate[l][None, :], l,
                              w_da_bf, w_sc_bf, w_cf_bf)
        x_mid = _out_projection(merged, w_out_bf, x_pair, mod4, l)
        x_pair = _swiglu(x_mid, g_norm2[l][None, :], mod4, l, w_ffn_in_bf, w_ffn_out_bf,
                         g_final[None, :] if l == DEPTH - 1 else None)

    y_prompt, y_sample = x_pair
    new_k, new_v = new_kv
    return (y_prompt.reshape(BATCH, SEQ, D_MODEL), y_sample.reshape(DEC_BATCH, DEC_SEQ, D_MODEL),
            new_k.reshape(BATCH, DEPTH, SEQ, DA_HEADS, 2 * DA_HEAD_DIM),
            new_v.reshape(BATCH, DEPTH, SEQ, DA_HEADS, DA_VDIM))
```

```python
import functools
import math

import jax
import jax.numpy as jnp
from jax import lax
from jax.experimental import pallas as pl
from jax.experimental.pallas import tpu as pltpu

D_MODEL = 2048
BATCH = 16
SEQ = 256
DEPTH = 2
DEC_BATCH = 8
DEC_SEQ = 1024
PAST_LEN = 256
GRID_W = 64
DA_HEADS = 8
DA_HEAD_DIM = 64
DA_VDIM = 2 * DA_HEAD_DIM
DA_WIDTH = DA_HEADS * 2 * DA_HEAD_DIM
SC_WIDTH = 512
SC_K = 3
CF_WIDTH = 512
CF_K = 31
N_BRANCH = 3
FF_HIDDEN = -(-8 * D_MODEL // (3 * 256)) * 256
ROPE_THETA = 10000.0
EPS = 1e-6

OFF_K = DA_WIDTH
OFF_V = 2 * DA_WIDTH
OFF_SC = 3 * DA_WIDTH
OFF_CF = OFF_SC + 3 * SC_WIDTH
OFF_GATE = OFF_CF + 2 * CF_WIDTH
IN_COLS = OFF_GATE + N_BRANCH * D_MODEL

N_CTX = BATCH * SEQ
N_LAT = DEC_BATCH * DEC_SEQ
N_TOK = N_CTX + N_LAT
MOD_ROWS = 16

F32 = jnp.float32
BF16 = jnp.bfloat16

VMEM_BUDGET_V7X = 56 * 1024 * 1024
LANES = 128
SUBLANES = 8

TM_NORM = 512
TM_IN = 2048
TM = 1024
TN = 512
TM_FFN = 512
TH_FFN = 512
TQ_LAT = 1024
HEADS_LAT = 2
HEADS_CTX = 8
TQ_SUB = 256
LOG2_E = math.log2(math.e)
NORM_ROWS = 128
CONV_ROWS = 32


def _nbytes(shape, dtype):
    return math.prod(shape) * jnp.dtype(dtype).itemsize


def _vmem_limit(blocks, scratch=(), temps=()):
    total = 2 * sum(_nbytes(s, d) for s, d in blocks)
    total += sum(_nbytes(s, d) for s, d in scratch)
    total += sum(_nbytes(s, d) for s, d in temps)
    return min(total, VMEM_BUDGET_V7X)


def _mod_group(i, tm):
    return jnp.maximum(i * tm - N_CTX + DEC_SEQ, 0) // DEC_SEQ


def _rms(x, g):
    return x * lax.rsqrt(jnp.mean(x * x, axis=-1, keepdims=True) + EPS) * g


def _mod_norm_rows(o_ref, x_refs, n_rows, g, shift, scale):
    gain = g * (1.0 + scale)
    is_ctx = pl.program_id(0) < N_CTX // n_rows

    def load(rows):
        if len(x_refs) == 2:
            return jnp.where(is_ctx, x_refs[0][rows, :], x_refs[1][rows, :])
        return x_refs[0][rows, :]

    for c in range(n_rows // NORM_ROWS):
        rows = slice(c * NORM_ROWS, (c + 1) * NORM_ROWS)
        x = load(rows)
        r = lax.rsqrt(jnp.mean(x * x, axis=-1, keepdims=True) + EPS)
        o_ref[rows, :] = ((load(rows) * r) * gain + shift).astype(BF16)


def _sigmoid(x):
    return 0.5 * jnp.tanh(0.5 * x) + 0.5


def _mod_kernel(c_ref, w_ref, b_ref, o_ref):
    c = c_ref[...]
    s = (c * _sigmoid(c)).astype(BF16)
    o_ref[...] = jnp.dot(s, w_ref[...].astype(BF16), preferred_element_type=F32) + b_ref[...]


def _modulation(cvec, w_mod, b_mod):
    bn = 1024
    return pl.pallas_call(
        _mod_kernel,
        grid=(DEPTH, 6 * D_MODEL // bn),
        in_specs=[
            pl.BlockSpec((MOD_ROWS, D_MODEL), lambda l, j: (0, 0)),
            pl.BlockSpec((None, D_MODEL, bn), lambda l, j: (l, 0, j)),
            pl.BlockSpec((None, 1, bn), lambda l, j: (l, 0, j)),
        ],
        out_specs=pl.BlockSpec((None, MOD_ROWS, bn), lambda l, j: (l, 0, j)),
        out_shape=jax.ShapeDtypeStruct((DEPTH, MOD_ROWS, 6 * D_MODEL), F32),
        compiler_params=pltpu.CompilerParams(
            dimension_semantics=("parallel", "parallel"),
            vmem_limit_bytes=_vmem_limit(
                [((D_MODEL, bn), F32), ((MOD_ROWS, D_MODEL), F32), ((MOD_ROWS, bn), F32)],
                temps=[((D_MODEL, bn), BF16), ((D_MODEL, bn), F32)])),
        name="modulation",
    )(cvec, w_mod, b_mod.reshape(DEPTH, 1, 6 * D_MODEL))


def _mod_spec(layer, chunk, tm):
    return pl.BlockSpec((None, None, 1, D_MODEL),
                        lambda i, *_: (layer, _mod_group(i, tm), 0, chunk))


def _group_specs(tm, tn, n_col_tiles=None):
    nct = N_CTX // tm
    if n_col_tiles is None:
        return (pl.BlockSpec((tm, tn), lambda i, *_: (jnp.minimum(i, nct - 1), 0)),
                pl.BlockSpec((tm, tn), lambda i, *_: (jnp.maximum(i - nct, 0), 0)))
    last = n_col_tiles - 1
    return (pl.BlockSpec((tm, tn), lambda i, j: (jnp.minimum(i, nct - 1), jnp.where(i < nct, j, last))),
            pl.BlockSpec((tm, tn), lambda i, j: (jnp.maximum(i - nct, 0), jnp.where(i < nct, 0, j))))


def _group_tile(xc_ref, xl_ref, tm):
    return jnp.where(pl.program_id(0) < N_CTX // tm, xc_ref[...], xl_ref[...])


def _prenorm_kernel(xc_ref, xl_ref, g_ref, shift_ref, scale_ref, o_ref):
    _mod_norm_rows(o_ref, (xc_ref, xl_ref), TM_NORM, g_ref[...], shift_ref[...], scale_ref[...])


def _pre_norm(x_pair, g, mod4, layer):
    return pl.pallas_call(
        _prenorm_kernel,
        grid=(N_TOK // TM_NORM,),
        in_specs=[
            *_group_specs(TM_NORM, D_MODEL),
            pl.BlockSpec((1, D_MODEL), lambda i: (0, 0)),
            _mod_spec(layer, 0, TM_NORM),
            _mod_spec(layer, 1, TM_NORM),
        ],
        out_specs=pl.BlockSpec((TM_NORM, D_MODEL), lambda i: (i, 0)),
        out_shape=jax.ShapeDtypeStruct((N_TOK, D_MODEL), BF16),
        compiler_params=pltpu.CompilerParams(
            dimension_semantics=("arbitrary",),
            vmem_limit_bytes=_vmem_limit(
                [((TM_NORM, D_MODEL), F32)] * 2 + [((TM_NORM, D_MODEL), BF16)],
                temps=[((TM_NORM, D_MODEL), F32)] * 3)),
        name="pre_norm",
    )(*x_pair, g, mod4, mod4)


N_CTX_TILES = N_CTX // TM_IN
KV_TILES = DA_WIDTH // TN


MIX_COLS = OFF_GATE
GATE_COLS = N_BRANCH * D_MODEL


def _cast_weight_tile(w_ref, wbf_scr):
    @pl.when(pl.program_id(1) == 0)
    def _():
        wbf_scr[...] = w_ref[...].astype(BF16)


def _inproj_mix_kernel(h_ref, w_ref, nk_hbm, nv_hbm, o_ref, nk_ref, nv_ref, wbf_scr):
    del nk_hbm, nv_hbm
    j, i = pl.program_id(0), pl.program_id(1)
    _cast_weight_tile(w_ref, wbf_scr)
    o_ref[...] = jnp.dot(h_ref[...], wbf_scr[...], preferred_element_type=F32)

    @pl.when((i < N_CTX_TILES) & (j >= OFF_K // TN) & (j < OFF_V // TN))
    def _():
        nk_ref[...] = o_ref[...].reshape(nk_ref.shape)

    @pl.when((i < N_CTX_TILES) & (j >= OFF_V // TN) & (j < OFF_SC // TN))
    def _():
        nv_ref[...] = o_ref[...].reshape(nv_ref.shape)


def _inproj_gate_kernel(h_ref, wa_ref, wb_ref, o_ref, wa_scr, wb_scr):
    _cast_weight_tile(wa_ref, wa_scr)
    _cast_weight_tile(wb_ref, wb_scr)
    h = h_ref[...]
    o_ref[:, :TN] = jnp.dot(h, wa_scr[...], preferred_element_type=F32).astype(BF16)
    o_ref[:, TN:] = jnp.dot(h, wb_scr[...], preferred_element_type=F32).astype(BF16)


def _kv_cache_spec(layer, col0):
    def index(j, i):
        rel = j - col0
        row = jnp.where(rel < 0, 0,
                        jnp.where(rel >= KV_TILES, N_CTX_TILES - 1, jnp.minimum(i, N_CTX_TILES - 1)))
        return (row, layer, 0, jnp.clip(rel, 0, KV_TILES - 1))
    return pl.BlockSpec((TM_IN // SEQ, None, SEQ, TN), index)


def _in_projection(h, layer, w_in, new_kv):
    h_spec = pl.BlockSpec((TM_IN, D_MODEL), lambda j, i: (i, 0))
    kv_sds = jax.ShapeDtypeStruct((BATCH, DEPTH, SEQ, DA_WIDTH), F32)
    kv_block = ((TM_IN // SEQ, SEQ, TN), F32)
    hbm = pl.BlockSpec(memory_space=pl.ANY)
    mix, new_k, new_v = pl.pallas_call(
        _inproj_mix_kernel,
        grid=(MIX_COLS // TN, N_TOK // TM_IN),
        in_specs=[h_spec, pl.BlockSpec((None, D_MODEL, TN), lambda j, i: (layer, 0, j)), hbm, hbm],
        out_specs=[pl.BlockSpec((TM_IN, TN), lambda j, i: (i, j)),
                   _kv_cache_spec(layer, OFF_K // TN), _kv_cache_spec(layer, OFF_V // TN)],
        out_shape=[jax.ShapeDtypeStruct((N_TOK, MIX_COLS), F32), kv_sds, kv_sds],
        scratch_shapes=[pltpu.VMEM((D_MODEL, TN), BF16)],
        input_output_aliases={2: 1, 3: 2},
        compiler_params=pltpu.CompilerParams(
            dimension_semantics=("arbitrary", "arbitrary"),
            vmem_limit_bytes=_vmem_limit(
                [((TM_IN, D_MODEL), BF16), ((D_MODEL, TN), F32), ((TM_IN, TN), F32), kv_block, kv_block],
                scratch=[((D_MODEL, TN), BF16)], temps=[((TM_IN, TN), F32)])),
        name="in_projection_mix",
    )(h, w_in, *new_kv)
    gates = pl.pallas_call(
        _inproj_gate_kernel,
        grid=(GATE_COLS // (2 * TN), N_TOK // TM_IN),
        in_specs=[h_spec,
                  pl.BlockSpec((None, D_MODEL, TN), lambda j, i: (layer, 0, MIX_COLS // TN + 2 * j)),
                  pl.BlockSpec((None, D_MODEL, TN), lambda j, i: (layer, 0, MIX_COLS // TN + 2 * j + 1))],
        out_specs=pl.BlockSpec((TM_IN, 2 * TN), lambda j, i: (i, j)),
        out_shape=jax.ShapeDtypeStruct((N_TOK, GATE_COLS), BF16),
        scratch_shapes=[pltpu.VMEM((D_MODEL, TN), BF16)] * 2,
        compiler_params=pltpu.CompilerParams(
            dimension_semantics=("arbitrary", "arbitrary"),
            vmem_limit_bytes=_vmem_limit(
                [((TM_IN, D_MODEL), BF16), ((D_MODEL, 2 * TN), F32), ((TM_IN, 2 * TN), BF16)],
                scratch=[((D_MODEL, 2 * TN), BF16)],
                temps=[((TM_IN, 2 * TN), F32), ((TM_IN, 2 * TN), BF16)])),
        name="in_projection_gates",
    )(h, w_in, w_in)
    return mix, gates, (new_k, new_v)


def _rope(x, cos, sin_lo, sin_hi):
    return (x * cos + pltpu.roll(x, LANES - 16, 1) * sin_lo + pltpu.roll(x, 16, 1) * sin_hi)


def _make_attn_kernel(seq_len, tq, n_cache, n_heads, use_rope, lam_init):
    def kernel(*refs):
        dl_ref, g_ref, q_ref, k_ref, v_ref = refs[:5]
        refs = refs[5:]
        if n_cache:
            ck_ref, cv_ref = refs[:2]
            refs = refs[2:]
        if use_rope:
            cos_ref, slo_ref, shi_ref = refs[:3]
            refs = refs[3:]
        o_ref, kk_scr, vv_scr = refs
        qb = pl.program_id(2)

        @pl.when(qb == 0)
        def _():
            for h in range(n_heads):
                cols = slice(h * LANES, (h + 1) * LANES)
                k = k_ref[:, cols]
                if use_rope:
                    k = _rope(k, cos_ref[...], slo_ref[...], shi_ref[...])
                if n_cache:
                    head = pl.program_id(1) * n_heads + h
                    rows = pl.ds(head, n_cache, stride=DA_HEADS)
                    kk_scr[h, 0:n_cache, :] = ck_ref[rows, :].astype(BF16)
                    vv_scr[h, 0:n_cache, 0:DA_VDIM] = cv_ref[rows, :].astype(BF16)
                kk_scr[h, n_cache:n_cache + seq_len, :] = k.astype(BF16)
                vv_scr[h, n_cache:n_cache + seq_len, 0:DA_VDIM] = v_ref[:, cols].astype(BF16)
                vv_scr[h, :, DA_VDIM:] = jnp.ones((n_cache + seq_len, DA_VDIM), BF16)

        dl = dl_ref[...]
        lam = (jnp.exp(jnp.sum(dl[0:1] * dl[1:2], axis=-1, keepdims=True))
               - jnp.exp(jnp.sum(dl[2:3] * dl[3:4], axis=-1, keepdims=True)) + lam_init)
        tsub = min(tq, TQ_SUB)
        lane = lax.broadcasted_iota(jnp.int32, (tsub, LANES), 1)

        def softmax_times_v(qm, kk, vv):
            s = lax.dot_general(qm.astype(BF16), kk, (((1,), (1,)), ((), ())), preferred_element_type=F32)
            e = jnp.exp2(s - jnp.max(s, axis=-1, keepdims=True)).astype(BF16)
            ev = jnp.dot(e, vv, preferred_element_type=F32)
            return ev[:, :DA_VDIM] / ev[:, DA_VDIM:]

        for h in range(n_heads):
            cols = slice(h * LANES, (h + 1) * LANES)
            kk, vv = kk_scr[h], vv_scr[h]
            for c in range(tq // tsub):
                q = q_ref[c * tsub:(c + 1) * tsub, cols]
                if use_rope:
                    rows = pl.ds(pl.multiple_of(qb * tq, tq) + c * tsub, tsub)
                    q = _rope(q, cos_ref[rows, :], slo_ref[rows, :], shi_ref[rows, :])
                q = q * (DA_HEAD_DIM ** -0.5 * LOG2_E)
                o1 = softmax_times_v(jnp.where(lane < DA_HEAD_DIM, q, 0.0), kk, vv)
                o2 = softmax_times_v(jnp.where(lane >= DA_HEAD_DIM, q, 0.0), kk, vv)
                o_ref[c * tsub:(c + 1) * tsub, cols] = (
                    _rms(o1 - lam * o2, g_ref[...]) * (1.0 - lam_init)).astype(BF16)

    return kernel


def _diff_attention(proj, da_lambda_l, subln_g, layer, *, latent, cache=None, rope=None):
    lam_init = 0.8 - 0.6 * math.exp(-0.3 * layer)
    if latent:
        n_seq, seq_len, tq, n_cache, row0, n_heads = DEC_BATCH, DEC_SEQ, TQ_LAT, PAST_LEN, N_CTX, HEADS_LAT
    else:
        n_seq, seq_len, tq, n_cache, row0, n_heads = BATCH, SEQ, SEQ, 0, 0, HEADS_CTX
    nq = seq_len // tq
    lk = n_cache + seq_len
    width = n_heads * LANES
    hk, hv = OFF_K // width, OFF_V // width

    in_specs = [
        pl.BlockSpec((4, DA_HEAD_DIM), lambda b, h, t: (0, 0)),
        pl.BlockSpec((1, DA_VDIM), lambda b, h, t: (0, 0)),
        pl.BlockSpec((tq, width), lambda b, h, t: (row0 // tq + b * nq + t, h)),
        pl.BlockSpec((seq_len, width), lambda b, h, t: (row0 // seq_len + b, hk + h)),
        pl.BlockSpec((seq_len, width), lambda b, h, t: (row0 // seq_len + b, hv + h)),
    ]
    args = [da_lambda_l, subln_g, proj, proj, proj]
    blocks = [((tq, width), F32), ((seq_len, width), F32), ((seq_len, width), F32), ((tq, width), BF16)]
    if latent:
        cache_k, cache_v = cache
        cache_block = (n_cache * DA_HEADS, LANES)
        in_specs += [pl.BlockSpec((None, None) + cache_block, lambda b, h, t: (b, layer, 0, 0))] * 2
        args += [cache_k, cache_v]
        in_specs += [pl.BlockSpec((seq_len, LANES), lambda b, h, t: (0, 0))] * 3
        args += list(rope)
        blocks += [(cache_block, F32)] * 2 + [((seq_len, LANES), F32)] * 3

    tsub = min(tq, TQ_SUB)
    scratch = [((n_heads, lk, LANES), BF16), ((n_heads, lk, 2 * DA_VDIM), BF16)]
    return pl.pallas_call(
        _make_attn_kernel(seq_len, tq, n_cache, n_heads, latent, lam_init),
        grid=(n_seq, DA_HEADS // n_heads, nq),
        in_specs=in_specs,
        out_specs=pl.BlockSpec((tq, width), lambda b, h, t: (b * nq + t, h)),
        out_shape=jax.ShapeDtypeStruct((n_seq * seq_len, DA_WIDTH), BF16),
        scratch_shapes=[pltpu.VMEM(s, d) for s, d in scratch],
        compiler_params=pltpu.CompilerParams(
            dimension_semantics=("parallel", "parallel", "arbitrary"),
            vmem_limit_bytes=_vmem_limit(
                blocks, scratch=scratch,
                temps=[((tsub, lk), F32)] * 6 * 4 + [((tq, 2 * DA_VDIM), F32)] * 4)),
        name="diff_attention_latent" if latent else "diff_attention_context",
    )(*args)


def _fill_shifted(scr, value, pad, seq_len, shifts):
    width = value.shape[-1]
    n = seq_len + 2 * pad - SUBLANES
    scr[0, 0:pad, :] = jnp.zeros((pad, width), F32)
    scr[0, pad + seq_len:, :] = jnp.zeros((pad, width), F32)
    scr[0, pad:pad + seq_len, :] = value
    for s, shift in enumerate(shifts, start=1):
        scr[s, 0:n, :] = scr[0, shift:shift + n, :]


def _tap(scr, shifts, r0, offset):
    aligned, rem = offset - offset % SUBLANES, offset % SUBLANES
    slot = 0 if rem == 0 else 1 + shifts.index(rem)
    return scr[slot, pl.ds(r0 + aligned, CONV_ROWS), :]


def _make_conv_kernel(seq_len):
    pad3, pad31 = SUBLANES, 2 * SUBLANES
    shifts3 = (1, SUBLANES - 1)
    shifts31 = tuple(range(1, SUBLANES))

    def kernel(gb_ref, gc_ref, sx_ref, ca_ref, cb_ref, w3_ref, w31_ref, b31_ref, lng_ref, lnb_ref,
               bo_ref, co_ref, t_scr, u_scr, y_scr):
        _fill_shifted(t_scr, gc_ref[...] * sx_ref[...], pad3, seq_len, shifts3)
        _fill_shifted(u_scr, ca_ref[...] * _sigmoid(cb_ref[...]), pad31, seq_len, shifts31)

        def chunk(c, carry):
            r0 = pl.multiple_of(c * CONV_ROWS, CONV_ROWS)
            rows = pl.ds(r0, CONV_ROWS)
            y = _tap(t_scr, shifts3, r0, pad3 - SC_K // 2) * w3_ref[0:1, :]
            for k in range(1, SC_K):
                y = y + _tap(t_scr, shifts3, r0, pad3 - SC_K // 2 + k) * w3_ref[k:k + 1, :]
            bo_ref[rows, :] = (gb_ref[rows, :] * y).astype(BF16)

            acc = _tap(u_scr, shifts31, r0, pad31 - CF_K // 2) * w31_ref[0:1, :]
            for k in range(1, CF_K):
                acc = acc + _tap(u_scr, shifts31, r0, pad31 - CF_K // 2 + k) * w31_ref[k:k + 1, :]
            y_scr[rows, :] = acc + b31_ref[...]
            return carry

        lax.fori_loop(0, seq_len // CONV_ROWS, chunk, 0)

        y = y_scr[...]
        xc = y - jnp.mean(y, axis=-1, keepdims=True)
        yn = xc * lax.rsqrt(jnp.mean(xc * xc, axis=-1, keepdims=True) + EPS) * lng_ref[...] + lnb_ref[...]
        co_ref[...] = (yn * _sigmoid(yn)).astype(BF16)

    return kernel


def _conv_branches(proj, sc_conv_l, cf_conv_l, cf_b, ln_g, ln_b, *, latent):
    if latent:
        n_seq, seq_len, row0 = DEC_BATCH, DEC_SEQ, N_CTX
    else:
        n_seq, seq_len, row0 = BATCH, SEQ, 0
    rb = row0 // seq_len
    c0 = OFF_SC // SC_WIDTH

    def col(cidx):
        return pl.BlockSpec((seq_len, SC_WIDTH), lambda b: (rb + b, cidx))

    def whole(shape):
        return pl.BlockSpec(shape, lambda b: (0, 0))

    in_specs = [col(c0), col(c0 + 1), col(c0 + 2), col(c0 + 3), col(c0 + 4),
                whole((SC_K, SC_WIDTH)), whole((CF_K, CF_WIDTH)),
                whole((1, CF_WIDTH)), whole((1, CF_WIDTH)), whole((1, CF_WIDTH))]
    args = [proj, proj, proj, proj, proj, sc_conv_l, cf_conv_l, cf_b, ln_g, ln_b]
    out_spec = pl.BlockSpec((seq_len, SC_WIDTH), lambda b: (b, 0))
    out_sds = jax.ShapeDtypeStruct((n_seq * seq_len, SC_WIDTH), BF16)
    scratch = [((3, seq_len + 2 * SUBLANES, SC_WIDTH), F32),
               ((SUBLANES, seq_len + 4 * SUBLANES, CF_WIDTH), F32),
               ((seq_len, CF_WIDTH), F32)]
    return pl.pallas_call(
        _make_conv_kernel(seq_len),
        grid=(n_seq,),
        in_specs=in_specs,
        out_specs=[out_spec, out_spec],
        out_shape=[out_sds, out_sds],
        scratch_shapes=[pltpu.VMEM(s, d) for s, d in scratch],
        compiler_params=pltpu.CompilerParams(
            dimension_semantics=("parallel",),
            vmem_limit_bytes=_vmem_limit(
                [((seq_len, SC_WIDTH), F32)] * 5 + [((seq_len, SC_WIDTH), BF16)] * 2,
                scratch=scratch,
                temps=[((seq_len, SC_WIDTH), F32)] * 4)),
        name="conv_branches_latent" if latent else "conv_branches_context",
    )(*args)


def _merge_kernel(ac_ref, al_ref, bc_ref, bl_ref, cc_ref, cl_ref, ga_ref, gb_ref, gc_ref,
                  bga_ref, bgb_ref, bgc_ref, wa_ref, wb_ref, wc_ref, o_ref):
    def merge(a_ref, b_ref, c_ref):
        br_a = jnp.dot(a_ref[...], wa_ref[...], preferred_element_type=F32)
        br_b = jnp.dot(b_ref[...], wb_ref[...], preferred_element_type=F32)
        br_c = jnp.dot(c_ref[...], wc_ref[...], preferred_element_type=F32)
        merged = (_sigmoid(ga_ref[...].astype(F32) + bga_ref[...]) * br_a
                  + _sigmoid(gb_ref[...].astype(F32) + bgb_ref[...]) * br_b
                  + _sigmoid(gc_ref[...].astype(F32) + bgc_ref[...]) * br_c)
        o_ref[...] = merged.astype(BF16)

    is_ctx = pl.program_id(0) < N_CTX // TM
    pl.when(is_ctx)(lambda: merge(ac_ref, bc_ref, cc_ref))
    pl.when(jnp.logical_not(is_ctx))(lambda: merge(al_ref, bl_ref, cl_ref))


def _gated_merge(attn_pair, b_pair, c_pair, gates, b_gate_l, layer, w_da, w_sc, w_cf):
    gstep = D_MODEL // TN

    def gate(k):
        return pl.BlockSpec((TM, TN), lambda i, j: (i, k * gstep + j))

    def gbias(k):
        return pl.BlockSpec((1, TN), lambda i, j: (0, k * gstep + j))

    def wcol(rows):
        return pl.BlockSpec((None, rows, TN), lambda i, j: (layer, 0, j))

    return pl.pallas_call(
        _merge_kernel,
        grid=(N_TOK // TM, gstep),
        in_specs=[
            *_group_specs(TM, DA_WIDTH), *_group_specs(TM, SC_WIDTH), *_group_specs(TM, CF_WIDTH),
            gate(0), gate(1), gate(2), gbias(0), gbias(1), gbias(2),
            wcol(DA_WIDTH), wcol(SC_WIDTH), wcol(CF_WIDTH),
        ],
        out_specs=pl.BlockSpec((TM, TN), lambda i, j: (i, j)),
        out_shape=jax.ShapeDtypeStruct((N_TOK, D_MODEL), BF16),
        compiler_params=pltpu.CompilerParams(
            dimension_semantics=("arbitrary", "arbitrary"),
            vmem_limit_bytes=_vmem_limit(
                [((TM, DA_WIDTH), BF16), ((TM, SC_WIDTH), BF16), ((TM, CF_WIDTH), BF16)] * 2
                + [((TM, TN), BF16)] * 3 + [((DA_WIDTH, TN), BF16), ((SC_WIDTH, TN), BF16),
                                            ((CF_WIDTH, TN), BF16), ((TM, TN), BF16)],
                temps=[((TM, TN), F32)] * 8 + [((TM, D_MODEL), BF16)])),
        name="gated_merge",
    )(*attn_pair, *b_pair, *c_pair, gates, gates, gates, b_gate_l, b_gate_l, b_gate_l, w_da, w_sc, w_cf)


def _outproj_kernel(m_ref, w_ref, xc_ref, xl_ref, gate_ref, o_ref):
    x = _group_tile(xc_ref, xl_ref, TM)
    o_ref[...] = x + gate_ref[...] * jnp.dot(m_ref[...], w_ref[...], preferred_element_type=F32)


def _out_projection(merged, w_out, x_pair, mod4, layer):
    n_col = D_MODEL // TN
    gate_spec = pl.BlockSpec((None, None, 1, TN),
                             lambda i, j: (layer, _mod_group(i, TM), 0, 2 * n_col + j))
    return pl.pallas_call(
        _outproj_kernel,
        grid=(N_TOK // TM, n_col),
        in_specs=[
            pl.BlockSpec((TM, D_MODEL), lambda i, j: (i, 0)),
            pl.BlockSpec((None, D_MODEL, TN), lambda i, j: (layer, 0, j)),
            *_group_specs(TM, TN, n_col),
            gate_spec,
        ],
        out_specs=pl.BlockSpec((TM, TN), lambda i, j: (i, j)),
        out_shape=jax.ShapeDtypeStruct((N_TOK, D_MODEL), F32),
        compiler_params=pltpu.CompilerParams(
            dimension_semantics=("arbitrary", "arbitrary"),
            vmem_limit_bytes=_vmem_limit(
                [((TM, D_MODEL), BF16), ((D_MODEL, TN), BF16)] + [((TM, TN), F32)] * 3,
                temps=[((TM, TN), F32)] * 3)),
        name="out_projection",
    )(merged, w_out, *x_pair, mod4)


def _make_ffn_kernel(final_norm):
    def kernel(*refs):
        x_ref, g_ref, shift_ref, scale_ref, gate_ref, wu_ref, ww_ref, wo_ref = refs[:8]
        refs = refs[8:]
        if final_norm:
            gf_ref = refs[0]
            refs = refs[1:]
        y_ctx_ref, y_lat_ref, h_scr, acc_scr = refs
        i, j = pl.program_id(0), pl.program_id(1)

        @pl.when(j == 0)
        def _():
            _mod_norm_rows(h_scr, (x_ref,), TM_FFN, g_ref[...], shift_ref[...], scale_ref[...])
            acc_scr[...] = jnp.zeros_like(acc_scr)

        h = h_scr[...]
        u = jnp.dot(h, wu_ref[...], preferred_element_type=F32)
        w = jnp.dot(h, ww_ref[...], preferred_element_type=F32)
        act = (u * _sigmoid(u) * w).astype(BF16)
        acc_scr[...] += jnp.dot(act, wo_ref[...], preferred_element_type=F32)

        def result():
            y = x_ref[...] + gate_ref[...] * acc_scr[...]
            return _rms(y, gf_ref[...]) if final_norm else y

        last = j == pl.num_programs(1) - 1

        @pl.when(last & (i < N_CTX // TM_FFN))
        def _():
            y_ctx_ref[...] = result()

        @pl.when(last & (i >= N_CTX // TM_FFN))
        def _():
            y_lat_ref[...] = result()

    return kernel


def _swiglu(x, g, mod4, layer, w_ffn_in, w_ffn_out, g_final=None):
    final_norm = g_final is not None
    nh = FF_HIDDEN // TH_FFN
    n_ctx_tiles = N_CTX // TM_FFN
    row = pl.BlockSpec((1, D_MODEL), lambda i, j: (0, 0))
    in_specs = [
        pl.BlockSpec((TM_FFN, D_MODEL), lambda i, j: (i, 0)),
        row,
        _mod_spec(layer, 3, TM_FFN), _mod_spec(layer, 4, TM_FFN), _mod_spec(layer, 5, TM_FFN),
        pl.BlockSpec((None, D_MODEL, TH_FFN), lambda i, j: (layer, 0, j)),
        pl.BlockSpec((None, D_MODEL, TH_FFN), lambda i, j: (layer, 0, nh + j)),
        pl.BlockSpec((None, TH_FFN, D_MODEL), lambda i, j: (layer, j, 0)),
    ]
    args = [x, g, mod4, mod4, mod4, w_ffn_in, w_ffn_in, w_ffn_out]
    if final_norm:
        in_specs.append(row)
        args.append(g_final)
    out_specs = [pl.BlockSpec((TM_FFN, D_MODEL), lambda i, j: (jnp.minimum(i, n_ctx_tiles - 1), 0)),
                 pl.BlockSpec((TM_FFN, D_MODEL), lambda i, j: (jnp.maximum(i - n_ctx_tiles, 0), 0))]
    out_shape = [jax.ShapeDtypeStruct((N_CTX, D_MODEL), F32), jax.ShapeDtypeStruct((N_LAT, D_MODEL), F32)]
    scratch = [((TM_FFN, D_MODEL), BF16), ((TM_FFN, D_MODEL), F32)]
    return pl.pallas_call(
        _make_ffn_kernel(final_norm),
        grid=(N_TOK // TM_FFN, nh),
        in_specs=in_specs,
        out_specs=out_specs,
        out_shape=out_shape,
        scratch_shapes=[pltpu.VMEM(s, d) for s, d in scratch],
        compiler_params=pltpu.CompilerParams(
            dimension_semantics=("arbitrary", "arbitrary"),
            vmem_limit_bytes=_vmem_limit(
                [((TM_FFN, D_MODEL), F32)] * 3 + [((D_MODEL, TH_FFN), BF16)] * 2
                + [((TH_FFN, D_MODEL), BF16)],
                scratch=scratch,
                temps=[((TM_FFN, TH_FFN), F32)] * 4 + [((TM_FFN, D_MODEL), F32)] * 2)),
        name="swiglu_final" if final_norm else "swiglu",
    )(*args)


def _rope_tables():
    rows = DEC_SEQ // GRID_W
    row_ids = jnp.repeat(jnp.arange(rows), GRID_W).astype(F32)
    col_ids = jnp.tile(jnp.arange(GRID_W), rows).astype(F32)
    n_freq = DA_HEAD_DIM // 4
    inv = ROPE_THETA ** (-jnp.arange(n_freq, dtype=F32) / n_freq)
    ang_r, ang_c = row_ids[:, None] * inv, col_ids[:, None] * inv
    zero = jnp.zeros_like(ang_r)
    cos = jnp.concatenate([jnp.cos(ang_r)] * 2 + [jnp.cos(ang_c)] * 2, axis=-1)
    sin_lo = jnp.concatenate([-jnp.sin(ang_r), zero, -jnp.sin(ang_c), zero], axis=-1)
    sin_hi = jnp.concatenate([zero, jnp.sin(ang_r), zero, jnp.sin(ang_c)], axis=-1)
    return tuple(jnp.tile(t, (1, LANES // DA_HEAD_DIM)) for t in (cos, sin_lo, sin_hi))


def kernel(x_prompt, x_sample, cache_k, cache_v, c, c_ctx, w_mod, b_mod, g_norm1, w_in, da_lambda, da_subln,
           w_da_out, sc_conv, w_sc_out, cf_conv, cf_conv_b, cf_ln_g, cf_ln_b, w_cf_out, b_gate, w_out,
           g_norm2, w_ffn_in, w_ffn_out, g_final):
    x_pair = (x_prompt.reshape(N_CTX, D_MODEL), x_sample.reshape(N_LAT, D_MODEL))
    cvec =jnp.concatenate([c_ctx[None, :], c, jnp.zeros((MOD_ROWS - 1 - DEC_BATCH, D_MODEL), F32)], axis=0)
    mod4 = _modulation(cvec, w_mod, b_mod).reshape(DEPTH, MOD_ROWS, 1, 6 * D_MODEL)
    rope = _rope_tables()
    cache_k = cache_k.reshape(DEC_BATCH, DEPTH, PAST_LEN * DA_HEADS, 2 * DA_HEAD_DIM)
    cache_v = cache_v.reshape(DEC_BATCH, DEPTH, PAST_LEN * DA_HEADS, DA_VDIM)

    w_ffn_in_bf, w_ffn_out_bf = w_ffn_in.astype(BF16), w_ffn_out.astype(BF16)
    w_da_bf, w_sc_bf, w_cf_bf, w_out_bf = (w.astype(BF16) for w in (w_da_out, w_sc_out, w_cf_out, w_out))

    new_kv = tuple(jnp.zeros((BATCH, DEPTH, SEQ, DA_WIDTH), F32) for _ in range(2))
    for l in range(DEPTH):
        h = _pre_norm(x_pair, g_norm1[l][None, :], mod4, l)
        proj, gates, new_kv = _in_projection(h, l, w_in, new_kv)

        subln = da_subln[l][None, :]
        attn_pair = (_diff_attention(proj, da_lambda[l], subln, l, latent=False),
                     _diff_attention(proj, da_lambda[l], subln, l, latent=True, cache=(cache_k, cache_v),
                                     rope=rope))
        conv_args = (sc_conv[l], cf_conv[l], cf_conv_b[l][None, :], cf_ln_g[l][None, :], cf_ln_b[l][None, :])
        b_ctx, c_ctx_pre = _conv_branches(proj, *conv_args, latent=False)
        b_lat, c_lat = _conv_branches(proj, *conv_args, latent=True)

        merged = _gated_merge(attn_pair, (b_ctx, b_lat), (c_ctx_pre, c_lat), gates, b_gate[l][None, :], l,
                              w_da_bf, w_sc_bf, w_cf_bf)
        x_mid = _out_projection(merged, w_out_bf, x_pair, mod4, l)
        x_pair = _swiglu(x_mid, g_norm2[l][None, :], mod4, l, w_ffn_in_bf, w_ffn_out_bf,
                         g_final[None, :] if l == DEPTH - 1 else None)

    y_prompt, y_sample = x_pair
    new_k, new_v = new_kv
    return (y_prompt.reshape(BATCH, SEQ, D_MODEL), y_sample.reshape(DEC_BATCH, DEC_SEQ, D_MODEL),
            new_k.reshape(BATCH, DEPTH, SEQ, DA_HEADS, 2 * DA_HEAD_DIM),
            new_v.reshape(BATCH, DEPTH, SEQ, DA_HEADS, DA_VDIM))
```

```python
import functools
import math

import jax
import jax.numpy as jnp
from jax import lax
from jax.experimental import pallas as pl
from jax.experimental.pallas import tpu as pltpu

D_MODEL = 2048
BATCH = 16
SEQ = 256
DEPTH = 2
DEC_BATCH = 8
DEC_SEQ = 1024
PAST_LEN = 256
GRID_W = 64
DA_HEADS = 8
DA_HEAD_DIM = 64
DA_VDIM = 2 * DA_HEAD_DIM
DA_WIDTH = DA_HEADS * 2 * DA_HEAD_DIM
SC_WIDTH = 512
SC_K = 3
CF_WIDTH = 512
CF_K = 31
N_BRANCH = 3
FF_HIDDEN = -(-8 * D_MODEL // (3 * 256)) * 256
ROPE_THETA = 10000.0
EPS = 1e-6

OFF_K = DA_WIDTH
OFF_V = 2 * DA_WIDTH
OFF_SC = 3 * DA_WIDTH
OFF_CF = OFF_SC + 3 * SC_WIDTH
OFF_GATE = OFF_CF + 2 * CF_WIDTH
IN_COLS = OFF_GATE + N_BRANCH * D_MODEL

N_CTX = BATCH * SEQ
N_LAT = DEC_BATCH * DEC_SEQ
N_TOK = N_CTX + N_LAT
MOD_ROWS = 16

F32 = jnp.float32
BF16 = jnp.bfloat16

VMEM_BUDGET_V7X = 56 * 1024 * 1024
LANES = 128
SUBLANES = 8

TM_NORM = 512
TM_IN = 2048
TM = 1024
TN = 512
TM_FFN = 512
TH_FFN = 512
TQ_LAT = 1024
HEADS_LAT = 2
HEADS_CTX = 8
TQ_SUB = 256
LOG2_E = math.log2(math.e)
NORM_ROWS = 128
CONV_ROWS = 32


def _nbytes(shape, dtype):
    return math.prod(shape) * jnp.dtype(dtype).itemsize


def _vmem_limit(blocks, scratch=(), temps=()):
    total = 2 * sum(_nbytes(s, d) for s, d in blocks)
    total += sum(_nbytes(s, d) for s, d in scratch)
    total += sum(_nbytes(s, d) for s, d in temps)
    return min(total, VMEM_BUDGET_V7X)


def _mod_group(i, tm):
    return jnp.maximum(i * tm - N_CTX + DEC_SEQ, 0) // DEC_SEQ


def _rms(x, g):
    return x * lax.rsqrt(jnp.mean(x * x, axis=-1, keepdims=True) + EPS) * g


def _mod_norm_rows(o_ref, x_refs, n_rows, g, shift, scale):
    gain = g * (1.0 + scale)
    is_ctx = pl.program_id(0) < N_CTX // n_rows

    def load(rows):
        if len(x_refs) == 2:
            return jnp.where(is_ctx, x_refs[0][rows, :], x_refs[1][rows, :])
        return x_refs[0][rows, :]

    for c in range(n_rows // NORM_ROWS):
        rows = slice(c * NORM_ROWS, (c + 1) * NORM_ROWS)
        x = load(rows)
        r = lax.rsqrt(jnp.mean(x * x, axis=-1, keepdims=True) + EPS)
        o_ref[rows, :] = ((load(rows) * r) * gain + shift).astype(BF16)


def _sigmoid(x):
    return 0.5 * jnp.tanh(0.5 * x) + 0.5


def _mod_kernel(c_ref, w_ref, b_ref, o_ref):
    c = c_ref[...]
    s = (c * _sigmoid(c)).astype(BF16)
    o_ref[...] = jnp.dot(s, w_ref[...].astype(BF16), preferred_element_type=F32) + b_ref[...]


def _modulation(cvec, w_mod, b_mod):
    bn = 1024
    return pl.pallas_call(
        _mod_kernel,
        grid=(DEPTH, 6 * D_MODEL // bn),
        in_specs=[
            pl.BlockSpec((MOD_ROWS, D_MODEL), lambda l, j: (0, 0)),
            pl.BlockSpec((None, D_MODEL, bn), lambda l, j: (l, 0, j)),
            pl.BlockSpec((None, 1, bn), lambda l, j: (l, 0, j)),
        ],
        out_specs=pl.BlockSpec((None, MOD_ROWS, bn), lambda l, j: (l, 0, j)),
        out_shape=jax.ShapeDtypeStruct((DEPTH, MOD_ROWS, 6 * D_MODEL), F32),
        compiler_params=pltpu.CompilerParams(
            dimension_semantics=("parallel", "parallel"),
            vmem_limit_bytes=_vmem_limit(
                [((D_MODEL, bn), F32), ((MOD_ROWS, D_MODEL), F32), ((MOD_ROWS, bn), F32)],
                temps=[((D_MODEL, bn), BF16), ((D_MODEL, bn), F32)])),
        name="modulation",
    )(cvec, w_mod, b_mod.reshape(DEPTH, 1, 6 * D_MODEL))


def _mod_spec(layer, chunk, tm):
    return pl.BlockSpec((None, None, 1, D_MODEL),
                        lambda i, *_: (layer, _mod_group(i, tm), 0, chunk))


def _group_specs(tm, tn, n_col_tiles=None):
    nct = N_CTX // tm
    if n_col_tiles is None:
        return (pl.BlockSpec((tm, tn), lambda i, *_: (jnp.minimum(i, nct - 1), 0)),
                pl.BlockSpec((tm, tn), lambda i, *_: (jnp.maximum(i - nct, 0), 0)))
    last = n_col_tiles - 1
    return (pl.BlockSpec((tm, tn), lambda i, j: (jnp.minimum(i, nct - 1), jnp.where(i < nct, j, last))),
            pl.BlockSpec((tm, tn), lambda i, j: (jnp.maximum(i - nct, 0), jnp.where(i < nct, 0, j))))


def _group_tile(xc_ref, xl_ref, tm):
    return jnp.where(pl.program_id(0) < N_CTX // tm, xc_ref[...], xl_ref[...])


def _prenorm_kernel(xc_ref, xl_ref, g_ref, shift_ref, scale_ref, o_ref):
    _mod_norm_rows(o_ref, (xc_ref, xl_ref), TM_NORM, g_ref[...], shift_ref[...], scale_ref[...])


def _pre_norm(x_pair, g, mod4, layer):
    return pl.pallas_call(
        _prenorm_kernel,
        grid=(N_TOK // TM_NORM,),
        in_specs=[
            *_group_specs(TM_NORM, D_MODEL),
            pl.BlockSpec((1, D_MODEL), lambda i: (0, 0)),
            _mod_spec(layer, 0, TM_NORM),
            _mod_spec(layer, 1, TM_NORM),
        ],
        out_specs=pl.BlockSpec((TM_NORM, D_MODEL), lambda i: (i, 0)),
        out_shape=jax.ShapeDtypeStruct((N_TOK, D_MODEL), BF16),
        compiler_params=pltpu.CompilerParams(
            dimension_semantics=("arbitrary",),
            vmem_limit_bytes=_vmem_limit(
                [((TM_NORM, D_MODEL), F32)] * 2 + [((TM_NORM, D_MODEL), BF16)],
                temps=[((TM_NORM, D_MODEL), F32)] * 3)),
        name="pre_norm",
    )(*x_pair, g, mod4, mod4)


N_CTX_TILES = N_CTX // TM_IN
KV_TILES = DA_WIDTH // TN


MIX_COLS = OFF_GATE
GATE_COLS = N_BRANCH * D_MODEL


def _cast_weight_tile(w_ref, wbf_scr):
    @pl.when(pl.program_id(1) == 0)
    def _():
        wbf_scr[...] = w_ref[...].astype(BF16)


def _inproj_mix_kernel(h_ref, w_ref, nk_hbm, nv_hbm, o_ref, nk_ref, nv_ref, wbf_scr):
    del nk_hbm, nv_hbm
    j, i = pl.program_id(0), pl.program_id(1)
    _cast_weight_tile(w_ref, wbf_scr)
    o_ref[...] = jnp.dot(h_ref[...], wbf_scr[...], preferred_element_type=F32)

    @pl.when((i < N_CTX_TILES) & (j >= OFF_K // TN) & (j < OFF_V // TN))
    def _():
        nk_ref[...] = o_ref[...].reshape(nk_ref.shape)

    @pl.when((i < N_CTX_TILES) & (j >= OFF_V // TN) & (j < OFF_SC // TN))
    def _():
        nv_ref[...] = o_ref[...].reshape(nv_ref.shape)


def _inproj_gate_kernel(h_ref, wa_ref, wb_ref, o_ref, wa_scr, wb_scr):
    _cast_weight_tile(wa_ref, wa_scr)
    _cast_weight_tile(wb_ref, wb_scr)
    h = h_ref[...]
    o_ref[:, :TN] = jnp.dot(h, wa_scr[...], preferred_element_type=F32).astype(BF16)
    o_ref[:, TN:] = jnp.dot(h, wb_scr[...], preferred_element_type=F32).astype(BF16)


def _kv_cache_spec(layer, col0):
    def index(j, i):
        rel = j - col0
        row = jnp.where(rel < 0, 0,
                        jnp.where(rel >= KV_TILES, N_CTX_TILES - 1, jnp.minimum(i, N_CTX_TILES - 1)))
        return (row, layer, 0, jnp.clip(rel, 0, KV_TILES - 1))
    return pl.BlockSpec((TM_IN // SEQ, None, SEQ, TN), index)


def _in_projection(h, layer, w_in, new_kv):
    h_spec = pl.BlockSpec((TM_IN, D_MODEL), lambda j, i: (i, 0))
    kv_sds = jax.ShapeDtypeStruct((BATCH, DEPTH, SEQ, DA_WIDTH), F32)
    kv_block = ((TM_IN // SEQ, SEQ, TN), F32)
    hbm = pl.BlockSpec(memory_space=pl.ANY)
    mix, new_k, new_v = pl.pallas_call(
        _inproj_mix_kernel,
        grid=(MIX_COLS // TN, N_TOK // TM_IN),
        in_specs=[h_spec, pl.BlockSpec((None, D_MODEL, TN), lambda j, i: (layer, 0, j)), hbm, hbm],
        out_specs=[pl.BlockSpec((TM_IN, TN), lambda j, i: (i, j)),
                   _kv_cache_spec(layer, OFF_K // TN), _kv_cache_spec(layer, OFF_V // TN)],
        out_shape=[jax.ShapeDtypeStruct((N_TOK, MIX_COLS), F32), kv_sds, kv_sds],
        scratch_shapes=[pltpu.VMEM((D_MODEL, TN), BF16)],
        input_output_aliases={2: 1, 3: 2},
        compiler_params=pltpu.CompilerParams(
            dimension_semantics=("arbitrary", "arbitrary"),
            vmem_limit_bytes=_vmem_limit(
                [((TM_IN, D_MODEL), BF16), ((D_MODEL, TN), F32), ((TM_IN, TN), F32), kv_block, kv_block],
                scratch=[((D_MODEL, TN), BF16)], temps=[((TM_IN, TN), F32)])),
        name="in_projection_mix",
    )(h, w_in, *new_kv)
    gates = pl.pallas_call(
        _inproj_gate_kernel,
        grid=(GATE_COLS // (2 * TN), N_TOK // TM_IN),
        in_specs=[h_spec,
                  pl.BlockSpec((None, D_MODEL, TN), lambda j, i: (layer, 0, MIX_COLS // TN + 2 * j)),
                  pl.BlockSpec((None, D_MODEL, TN), lambda j, i: (layer, 0, MIX_COLS // TN + 2 * j + 1))],
        out_specs=pl.BlockSpec((TM_IN, 2 * TN), lambda j, i: (i, j)),
        out_shape=jax.ShapeDtypeStruct((N_TOK, GATE_COLS), BF16),
        scratch_shapes=[pltpu.VMEM((D_MODEL, TN), BF16)] * 2,
        compiler_params=pltpu.CompilerParams(
            dimension_semantics=("arbitrary", "arbitrary"),
            vmem_limit_bytes=_vmem_limit(
                [((TM_IN, D_MODEL), BF16), ((D_MODEL, 2 * TN), F32), ((TM_IN, 2 * TN), BF16)],
                scratch=[((D_MODEL, 2 * TN), BF16)],
                temps=[((TM_IN, 2 * TN), F32), ((TM_IN, 2 * TN), BF16)])),
        name="in_projection_gates",
    )(h, w_in, w_in)
    return mix, gates, (new_k, new_v)


def _rope(x, cos, sin_lo, sin_hi):
    return (x * cos + pltpu.roll(x, LANES - 16, 1) * sin_lo + pltpu.roll(x, 16, 1) * sin_hi)


def _make_attn_kernel(seq_len, tq, n_cache, n_heads, use_rope, lam_init, n_cast):
    def kernel(*refs):
        dl_ref, g_ref, q_ref, k_ref, v_ref = refs[:5]
        refs = refs[5:]
        if n_cache:
            ck_ref, cv_ref = refs[:2]
            refs = refs[2:]
        if use_rope:
            cos_ref, slo_ref, shi_ref = refs[:3]
            refs = refs[3:]
        cast_in, refs = refs[:n_cast], refs[n_cast:]
        o_ref, refs = refs[0], refs[1:]
        cast_out, refs = refs[:n_cast], refs[n_cast:]
        kk_scr, vv_scr = refs
        qb = pl.program_id(2)
        for src, dst in zip(cast_in, cast_out):
            dst[...] = src[...].astype(BF16)

        @pl.when(qb == 0)
        def _():
            for h in range(n_heads):
                cols = slice(h * LANES, (h + 1) * LANES)
                k = k_ref[:, cols]
                if use_rope:
                    k = _rope(k, cos_ref[...], slo_ref[...], shi_ref[...])
                if n_cache:
                    head = pl.program_id(1) * n_heads + h
                    rows = pl.ds(head, n_cache, stride=DA_HEADS)
                    kk_scr[h, 0:n_cache, :] = ck_ref[rows, :].astype(BF16)
                    vv_scr[h, 0:n_cache, 0:DA_VDIM] = cv_ref[rows, :].astype(BF16)
                kk_scr[h, n_cache:n_cache + seq_len, :] = k.astype(BF16)
                vv_scr[h, n_cache:n_cache + seq_len, 0:DA_VDIM] = v_ref[:, cols].astype(BF16)
                vv_scr[h, :, DA_VDIM:] = jnp.ones((n_cache + seq_len, DA_VDIM), BF16)

        dl = dl_ref[...]
        lam = (jnp.exp(jnp.sum(dl[0:1] * dl[1:2], axis=-1, keepdims=True))
               - jnp.exp(jnp.sum(dl[2:3] * dl[3:4], axis=-1, keepdims=True)) + lam_init)
        tsub = min(tq, TQ_SUB)
        lane = lax.broadcasted_iota(jnp.int32, (tsub, LANES), 1)

        def softmax_times_v(qm, kk, vv):
            s = lax.dot_general(qm.astype(BF16), kk, (((1,), (1,)), ((), ())), preferred_element_type=F32)
            e = jnp.exp2(s - jnp.max(s, axis=-1, keepdims=True)).astype(BF16)
            ev = jnp.dot(e, vv, preferred_element_type=F32)
            return ev[:, :DA_VDIM] / ev[:, DA_VDIM:]

        for h in range(n_heads):
            cols = slice(h * LANES, (h + 1) * LANES)
            kk, vv = kk_scr[h], vv_scr[h]
            for c in range(tq // tsub):
                q = q_ref[c * tsub:(c + 1) * tsub, cols]
                if use_rope:
                    rows = pl.ds(pl.multiple_of(qb * tq, tq) + c * tsub, tsub)
                    q = _rope(q, cos_ref[rows, :], slo_ref[rows, :], shi_ref[rows, :])
                q = q * (DA_HEAD_DIM ** -0.5 * LOG2_E)
                o1 = softmax_times_v(jnp.where(lane < DA_HEAD_DIM, q, 0.0), kk, vv)
                o2 = softmax_times_v(jnp.where(lane >= DA_HEAD_DIM, q, 0.0), kk, vv)
                o_ref[c * tsub:(c + 1) * tsub, cols] = (
                    _rms(o1 - lam * o2, g_ref[...]) * (1.0 - lam_init)).astype(BF16)

    return kernel


def _diff_attention(proj, da_lambda_l, subln_g, layer, *, latent, cache=None, rope=None, cast_weights=()):
    lam_init = 0.8 - 0.6 * math.exp(-0.3 * layer)
    if latent:
        n_seq, seq_len, tq, n_cache, row0, n_heads = DEC_BATCH, DEC_SEQ, TQ_LAT, PAST_LEN, N_CTX, HEADS_LAT
    else:
        n_seq, seq_len, tq, n_cache, row0, n_heads = BATCH, SEQ, SEQ, 0, 0, HEADS_CTX
    nq = seq_len // tq
    lk = n_cache + seq_len
    width = n_heads * LANES
    hk, hv = OFF_K // width, OFF_V // width

    in_specs = [
        pl.BlockSpec((4, DA_HEAD_DIM), lambda b, h, t: (0, 0)),
        pl.BlockSpec((1, DA_VDIM), lambda b, h, t: (0, 0)),
        pl.BlockSpec((tq, width), lambda b, h, t: (row0 // tq + b * nq + t, h)),
        pl.BlockSpec((seq_len, width), lambda b, h, t: (row0 // seq_len + b, hk + h)),
        pl.BlockSpec((seq_len, width), lambda b, h, t: (row0 // seq_len + b, hv + h)),
    ]
    args = [da_lambda_l, subln_g, proj, proj, proj]
    blocks = [((tq, width), F32), ((seq_len, width), F32), ((seq_len, width), F32), ((tq, width), BF16)]
    if latent:
        cache_k, cache_v = cache
        cache_block = (n_cache * DA_HEADS, LANES)
        in_specs += [pl.BlockSpec((None, None) + cache_block, lambda b, h, t: (b, layer, 0, 0))] * 2
        args += [cache_k, cache_v]
        in_specs += [pl.BlockSpec((seq_len, LANES), lambda b, h, t: (0, 0))] * 3
        args += list(rope)
        blocks += [(cache_block, F32)] * 2 + [((seq_len, LANES), F32)] * 3

    n_hgroups = DA_HEADS // n_heads
    n_steps = n_seq * n_hgroups * nq
    out_specs = [pl.BlockSpec((tq, width), lambda b, h, t: (b * nq + t, h))]
    out_shape = [jax.ShapeDtypeStruct((n_seq * seq_len, DA_WIDTH), BF16)]
    for w in cast_weights:
        _, rows, cols = w.shape
        slab = rows // n_steps
        assert slab * n_steps == rows and slab % 16 == 0, (w.shape, n_steps)

        def step(b, h, t):
            return (b * n_hgroups + h) * nq + t

        in_specs.append(pl.BlockSpec((None, slab, cols), lambda b, h, t: (layer, step(b, h, t), 0)))
        args.append(w)
        out_specs.append(pl.BlockSpec((slab, cols), lambda b, h, t: (step(b, h, t), 0)))
        out_shape.append(jax.ShapeDtypeStruct((rows, cols), BF16))
        blocks += [((slab, cols), F32), ((slab, cols), BF16)]

    tsub = min(tq, TQ_SUB)
    scratch = [((n_heads, lk, LANES), BF16), ((n_heads, lk, 2 * DA_VDIM), BF16)]
    return pl.pallas_call(
        _make_attn_kernel(seq_len, tq, n_cache, n_heads, latent, lam_init, len(cast_weights)),
        grid=(n_seq, n_hgroups, nq),
        in_specs=in_specs,
        out_specs=out_specs,
        out_shape=out_shape,
        scratch_shapes=[pltpu.VMEM(s, d) for s, d in scratch],
        compiler_params=pltpu.CompilerParams(
            dimension_semantics=("parallel", "parallel", "arbitrary"),
            vmem_limit_bytes=_vmem_limit(
                blocks, scratch=scratch,
                temps=[((tsub, lk), F32)] * 6 * 4 + [((tq, 2 * DA_VDIM), F32)] * 4)),
        name="diff_attention_latent" if latent else "diff_attention_context",
    )(*args)


def _fill_shifted(scr, value, pad, seq_len, shifts):
    width = value.shape[-1]
    n = seq_len + 2 * pad - SUBLANES
    scr[0, 0:pad, :] = jnp.zeros((pad, width), F32)
    scr[0, pad + seq_len:, :] = jnp.zeros((pad, width), F32)
    scr[0, pad:pad + seq_len, :] = value
    for s, shift in enumerate(shifts, start=1):
        scr[s, 0:n, :] = scr[0, shift:shift + n, :]


def _tap(scr, shifts, r0, offset):
    aligned, rem = offset - offset % SUBLANES, offset % SUBLANES
    slot = 0 if rem == 0 else 1 + shifts.index(rem)
    return scr[slot, pl.ds(r0 + aligned, CONV_ROWS), :]


def _make_conv_kernel(seq_len):
    pad3, pad31 = SUBLANES, 2 * SUBLANES
    shifts3 = (1, SUBLANES - 1)
    shifts31 = tuple(range(1, SUBLANES))

    def kernel(gb_ref, gc_ref, sx_ref, ca_ref, cb_ref, w3_ref, w31_ref, b31_ref, lng_ref, lnb_ref,
               bo_ref, co_ref, t_scr, u_scr, y_scr):
        _fill_shifted(t_scr, gc_ref[...] * sx_ref[...], pad3, seq_len, shifts3)
        _fill_shifted(u_scr, ca_ref[...] * _sigmoid(cb_ref[...]), pad31, seq_len, shifts31)

        def chunk(c, carry):
            r0 = pl.multiple_of(c * CONV_ROWS, CONV_ROWS)
            rows = pl.ds(r0, CONV_ROWS)
            y = _tap(t_scr, shifts3, r0, pad3 - SC_K // 2) * w3_ref[0:1, :]
            for k in range(1, SC_K):
                y = y + _tap(t_scr, shifts3, r0, pad3 - SC_K // 2 + k) * w3_ref[k:k + 1, :]
            bo_ref[rows, :] = (gb_ref[rows, :] * y).astype(BF16)

            acc = _tap(u_scr, shifts31, r0, pad31 - CF_K // 2) * w31_ref[0:1, :]
            for k in range(1, CF_K):
                acc = acc + _tap(u_scr, shifts31, r0, pad31 - CF_K // 2 + k) * w31_ref[k:k + 1, :]
            y_scr[rows, :] = acc + b31_ref[...]
            return carry

        lax.fori_loop(0, seq_len // CONV_ROWS, chunk, 0)

        y = y_scr[...]
        xc = y - jnp.mean(y, axis=-1, keepdims=True)
        yn = xc * lax.rsqrt(jnp.mean(xc * xc, axis=-1, keepdims=True) + EPS) * lng_ref[...] + lnb_ref[...]
        co_ref[...] = (yn * _sigmoid(yn)).astype(BF16)

    return kernel


def _conv_branches(proj, sc_conv_l, cf_conv_l, cf_b, ln_g, ln_b, *, latent):
    if latent:
        n_seq, seq_len, row0 = DEC_BATCH, DEC_SEQ, N_CTX
    else:
        n_seq, seq_len, row0 = BATCH, SEQ, 0
    rb = row0 // seq_len
    c0 = OFF_SC // SC_WIDTH

    def col(cidx):
        return pl.BlockSpec((seq_len, SC_WIDTH), lambda b: (rb + b, cidx))

    def whole(shape):
        return pl.BlockSpec(shape, lambda b: (0, 0))

    in_specs = [col(c0), col(c0 + 1), col(c0 + 2), col(c0 + 3), col(c0 + 4),
                whole((SC_K, SC_WIDTH)), whole((CF_K, CF_WIDTH)),
                whole((1, CF_WIDTH)), whole((1, CF_WIDTH)), whole((1, CF_WIDTH))]
    args = [proj, proj, proj, proj, proj, sc_conv_l, cf_conv_l, cf_b, ln_g, ln_b]
    out_spec = pl.BlockSpec((seq_len, SC_WIDTH), lambda b: (b, 0))
    out_sds = jax.ShapeDtypeStruct((n_seq * seq_len, SC_WIDTH), BF16)
    scratch = [((3, seq_len + 2 * SUBLANES, SC_WIDTH), F32),
               ((SUBLANES, seq_len + 4 * SUBLANES, CF_WIDTH), F32),
               ((seq_len, CF_WIDTH), F32)]
    return pl.pallas_call(
        _make_conv_kernel(seq_len),
        grid=(n_seq,),
        in_specs=in_specs,
        out_specs=[out_spec, out_spec],
        out_shape=[out_sds, out_sds],
        scratch_shapes=[pltpu.VMEM(s, d) for s, d in scratch],
        compiler_params=pltpu.CompilerParams(
            dimension_semantics=("parallel",),
            vmem_limit_bytes=_vmem_limit(
                [((seq_len, SC_WIDTH), F32)] * 5 + [((seq_len, SC_WIDTH), BF16)] * 2,
                scratch=scratch,
                temps=[((seq_len, SC_WIDTH), F32)] * 4)),
        name="conv_branches_latent" if latent else "conv_branches_context",
    )(*args)


def _merge_kernel(ac_ref, al_ref, bc_ref, bl_ref, cc_ref, cl_ref, ga_ref, gb_ref, gc_ref,
                  bga_ref, bgb_ref, bgc_ref, wa_ref, wb_ref, wc_ref, o_ref):
    def merge(a_ref, b_ref, c_ref):
        br_a = jnp.dot(a_ref[...], wa_ref[...], preferred_element_type=F32)
        br_b = jnp.dot(b_ref[...], wb_ref[...], preferred_element_type=F32)
        br_c = jnp.dot(c_ref[...], wc_ref[...], preferred_element_type=F32)
        merged = (_sigmoid(ga_ref[...].astype(F32) + bga_ref[...]) * br_a
                  + _sigmoid(gb_ref[...].astype(F32) + bgb_ref[...]) * br_b
                  + _sigmoid(gc_ref[...].astype(F32) + bgc_ref[...]) * br_c)
        o_ref[...] = merged.astype(BF16)

    is_ctx = pl.program_id(0) < N_CTX // TM
    pl.when(is_ctx)(lambda: merge(ac_ref, bc_ref, cc_ref))
    pl.when(jnp.logical_not(is_ctx))(lambda: merge(al_ref, bl_ref, cl_ref))


def _gated_merge(attn_pair, b_pair, c_pair, gates, b_gate_l, w_da, w_sc, w_cf):
    gstep = D_MODEL // TN

    def gate(k):
        return pl.BlockSpec((TM, TN), lambda i, j: (i, k * gstep + j))

    def gbias(k):
        return pl.BlockSpec((1, TN), lambda i, j: (0, k * gstep + j))

    def wcol(rows):
        return pl.BlockSpec((rows, TN), lambda i, j: (0, j))

    return pl.pallas_call(
        _merge_kernel,
        grid=(N_TOK // TM, gstep),
        in_specs=[
            *_group_specs(TM, DA_WIDTH), *_group_specs(TM, SC_WIDTH), *_group_specs(TM, CF_WIDTH),
            gate(0), gate(1), gate(2), gbias(0), gbias(1), gbias(2),
            wcol(DA_WIDTH), wcol(SC_WIDTH), wcol(CF_WIDTH),
        ],
        out_specs=pl.BlockSpec((TM, TN), lambda i, j: (i, j)),
        out_shape=jax.ShapeDtypeStruct((N_TOK, D_MODEL), BF16),
        compiler_params=pltpu.CompilerParams(
            dimension_semantics=("arbitrary", "arbitrary"),
            vmem_limit_bytes=_vmem_limit(
                [((TM, DA_WIDTH), BF16), ((TM, SC_WIDTH), BF16), ((TM, CF_WIDTH), BF16)] * 2
                + [((TM, TN), BF16)] * 3 + [((DA_WIDTH, TN), BF16), ((SC_WIDTH, TN), BF16),
                                            ((CF_WIDTH, TN), BF16), ((TM, TN), BF16)],
                temps=[((TM, TN), F32)] * 8 + [((TM, D_MODEL), BF16)])),
        name="gated_merge",
    )(*attn_pair, *b_pair, *c_pair, gates, gates, gates, b_gate_l, b_gate_l, b_gate_l, w_da, w_sc, w_cf)


def _outproj_kernel(m_ref, w_ref, xc_ref, xl_ref, gate_ref, o_ref):
    x = _group_tile(xc_ref, xl_ref, TM)
    o_ref[...] = x + gate_ref[...] * jnp.dot(m_ref[...], w_ref[...], preferred_element_type=F32)


def _out_projection(merged, w_out, x_pair, mod4, layer):
    n_col = D_MODEL // TN
    gate_spec = pl.BlockSpec((None, None, 1, TN),
                             lambda i, j: (layer, _mod_group(i, TM), 0, 2 * n_col + j))
    return pl.pallas_call(
        _outproj_kernel,
        grid=(N_TOK // TM, n_col),
        in_specs=[
            pl.BlockSpec((TM, D_MODEL), lambda i, j: (i, 0)),
            pl.BlockSpec((D_MODEL, TN), lambda i, j: (0, j)),
            *_group_specs(TM, TN, n_col),
            gate_spec,
        ],
        out_specs=pl.BlockSpec((TM, TN), lambda i, j: (i, j)),
        out_shape=jax.ShapeDtypeStruct((N_TOK, D_MODEL), F32),
        compiler_params=pltpu.CompilerParams(
            dimension_semantics=("arbitrary", "arbitrary"),
            vmem_limit_bytes=_vmem_limit(
                [((TM, D_MODEL), BF16), ((D_MODEL, TN), BF16)] + [((TM, TN), F32)] * 3,
                temps=[((TM, TN), F32)] * 3)),
        name="out_projection",
    )(merged, w_out, *x_pair, mod4)


def _make_ffn_kernel(final_norm):
    def kernel(*refs):
        x_ref, g_ref, shift_ref, scale_ref, gate_ref, wu_ref, ww_ref, wo_ref = refs[:8]
        refs = refs[8:]
        if final_norm:
            gf_ref = refs[0]
            refs = refs[1:]
        y_ctx_ref, y_lat_ref, h_scr, acc_scr = refs
        i, j = pl.program_id(0), pl.program_id(1)

        @pl.when(j == 0)
        def _():
            _mod_norm_rows(h_scr, (x_ref,), TM_FFN, g_ref[...], shift_ref[...], scale_ref[...])
            acc_scr[...] = jnp.zeros_like(acc_scr)

        h = h_scr[...]
        u = jnp.dot(h, wu_ref[...], preferred_element_type=F32)
        w = jnp.dot(h, ww_ref[...], preferred_element_type=F32)
        act = (u * _sigmoid(u) * w).astype(BF16)
        acc_scr[...] += jnp.dot(act, wo_ref[...], preferred_element_type=F32)

        def result():
            y = x_ref[...] + gate_ref[...] * acc_scr[...]
            return _rms(y, gf_ref[...]) if final_norm else y

        last = j == pl.num_programs(1) - 1

        @pl.when(last & (i < N_CTX // TM_FFN))
        def _():
            y_ctx_ref[...] = result()

        @pl.when(last & (i >= N_CTX // TM_FFN))
        def _():
            y_lat_ref[...] = result()

    return kernel


def _swiglu(x, g, mod4, layer, w_ffn_in, w_ffn_out, g_final=None):
    final_norm = g_final is not None
    nh = FF_HIDDEN // TH_FFN
    n_ctx_tiles = N_CTX // TM_FFN
    row = pl.BlockSpec((1, D_MODEL), lambda i, j: (0, 0))
    in_specs = [
        pl.BlockSpec((TM_FFN, D_MODEL), lambda i, j: (i, 0)),
        row,
        _mod_spec(layer, 3, TM_FFN), _mod_spec(layer, 4, TM_FFN), _mod_spec(layer, 5, TM_FFN),
        pl.BlockSpec((D_MODEL, TH_FFN), lambda i, j: (0, j)),
        pl.BlockSpec((D_MODEL, TH_FFN), lambda i, j: (0, nh + j)),
        pl.BlockSpec((TH_FFN, D_MODEL), lambda i, j: (j, 0)),
    ]
    args = [x, g, mod4, mod4, mod4, w_ffn_in, w_ffn_in, w_ffn_out]
    if final_norm:
        in_specs.append(row)
        args.append(g_final)
    out_specs = [pl.BlockSpec((TM_FFN, D_MODEL), lambda i, j: (jnp.minimum(i, n_ctx_tiles - 1), 0)),
                 pl.BlockSpec((TM_FFN, D_MODEL), lambda i, j: (jnp.maximum(i - n_ctx_tiles, 0), 0))]
    out_shape = [jax.ShapeDtypeStruct((N_CTX, D_MODEL), F32), jax.ShapeDtypeStruct((N_LAT, D_MODEL), F32)]
    scratch = [((TM_FFN, D_MODEL), BF16), ((TM_FFN, D_MODEL), F32)]
    return pl.pallas_call(
        _make_ffn_kernel(final_norm),
        grid=(N_TOK // TM_FFN, nh),
        in_specs=in_specs,
        out_specs=out_specs,
        out_shape=out_shape,
        scratch_shapes=[pltpu.VMEM(s, d) for s, d in scratch],
        compiler_params=pltpu.CompilerParams(
            dimension_semantics=("arbitrary", "arbitrary"),
            vmem_limit_bytes=_vmem_limit(
                [((TM_FFN, D_MODEL), F32)] * 3 + [((D_MODEL, TH_FFN), BF16)] * 2
                + [((TH_FFN, D_MODEL), BF16)],
                scratch=scratch,
                temps=[((TM_FFN, TH_FFN), F32)] * 4 + [((TM_FFN, D_MODEL), F32)] * 2)),
        name="swiglu_final" if final_norm else "swiglu",
    )(*args)


def _rope_tables():
    rows = DEC_SEQ // GRID_W
    row_ids = jnp.repeat(jnp.arange(rows), GRID_W).astype(F32)
    col_ids = jnp.tile(jnp.arange(GRID_W), rows).astype(F32)
    n_freq = DA_HEAD_DIM // 4
    inv = ROPE_THETA ** (-jnp.arange(n_freq, dtype=F32) / n_freq)
    ang_r, ang_c = row_ids[:, None] * inv, col_ids[:, None] * inv
    zero = jnp.zeros_like(ang_r)
    cos = jnp.concatenate([jnp.cos(ang_r)] * 2 + [jnp.cos(ang_c)] * 2, axis=-1)
    sin_lo = jnp.concatenate([-jnp.sin(ang_r), zero, -jnp.sin(ang_c), zero], axis=-1)
    sin_hi = jnp.concatenate([zero, jnp.sin(ang_r), zero, jnp.sin(ang_c)], axis=-1)
    return tuple(jnp.tile(t, (1, LANES // DA_HEAD_DIM)) for t in (cos, sin_lo, sin_hi))


def kernel(x_prompt, x_sample, cache_k, cache_v, c, c_ctx, w_mod, b_mod, g_norm1, w_in, da_lambda, da_subln,
           w_da_out, sc_conv, w_sc_out, cf_conv, cf_conv_b, cf_ln_g, cf_ln_b, w_cf_out, b_gate, w_out,
           g_norm2, w_ffn_in, w_ffn_out, g_final):
    x_pair = (x_prompt.reshape(N_CTX, D_MODEL), x_sample.reshape(N_LAT, D_MODEL))
    cvec =jnp.concatenate([c_ctx[None, :], c, jnp.zeros((MOD_ROWS - 1 - DEC_BATCH, D_MODEL), F32)], axis=0)
    mod4 = _modulation(cvec, w_mod, b_mod).reshape(DEPTH, MOD_ROWS, 1, 6 * D_MODEL)
    rope = _rope_tables()
    cache_k = cache_k.reshape(DEC_BATCH, DEPTH, PAST_LEN * DA_HEADS, 2 * DA_HEAD_DIM)
    cache_v = cache_v.reshape(DEC_BATCH, DEPTH, PAST_LEN * DA_HEADS, DA_VDIM)

    new_kv = tuple(jnp.zeros((BATCH, DEPTH, SEQ, DA_WIDTH), F32) for _ in range(2))
    for l in range(DEPTH):
        h = _pre_norm(x_pair, g_norm1[l][None, :], mod4, l)
        proj, gates, new_kv = _in_projection(h, l, w_in, new_kv)

        subln = da_subln[l][None, :]
        attn_ctx, w_out_bf, w_da_bf, w_sc_bf, w_cf_bf = _diff_attention(
            proj, da_lambda[l], subln, l, latent=False, cast_weights=(w_out, w_da_out, w_sc_out, w_cf_out))
        attn_lat, w_ffn_in_bf, w_ffn_out_bf = _diff_attention(
            proj, da_lambda[l], subln, l, latent=True, cache=(cache_k, cache_v), rope=rope,
            cast_weights=(w_ffn_in, w_ffn_out))
        attn_pair = (attn_ctx, attn_lat)
        conv_args = (sc_conv[l], cf_conv[l], cf_conv_b[l][None, :], cf_ln_g[l][None, :], cf_ln_b[l][None, :])
        b_ctx, c_ctx_pre = _conv_branches(proj, *conv_args, latent=False)
        b_lat, c_lat = _conv_branches(proj, *conv_args, latent=True)

        merged = _gated_merge(attn_pair, (b_ctx, b_lat), (c_ctx_pre, c_lat), gates, b_gate[l][None, :],
                              w_da_bf, w_sc_bf, w_cf_bf)
        x_mid = _out_projection(merged, w_out_bf, x_pair, mod4, l)
        x_pair = _swiglu(x_mid, g_norm2[l][None, :], mod4, l, w_ffn_in_bf, w_ffn_out_bf,
                         g_final[None, :] if l == DEPTH - 1 else None)

    y_prompt, y_sample = x_pair
    new_k, new_v = new_kv
    return (y_prompt.reshape(BATCH, SEQ, D_MODEL), y_sample.reshape(DEC_BATCH, DEC_SEQ, D_MODEL),
            new_k.reshape(BATCH, DEPTH, SEQ, DA_HEADS, 2 * DA_HEAD_DIM),
            new_v.reshape(BATCH, DEPTH, SEQ, DA_HEADS, DA_VDIM))
```

```python
import functools
import math

import jax
import jax.numpy as jnp
from jax import lax
from jax.experimental import pallas as pl
from jax.experimental.pallas import tpu as pltpu

D_MODEL = 2048
BATCH = 16
SEQ = 256
DEPTH = 2
DEC_BATCH = 8
DEC_SEQ = 1024
PAST_LEN = 256
GRID_W = 64
DA_HEADS = 8
DA_HEAD_DIM = 64
DA_VDIM = 2 * DA_HEAD_DIM
DA_WIDTH = DA_HEADS * 2 * DA_HEAD_DIM
SC_WIDTH = 512
SC_K = 3
CF_WIDTH = 512
CF_K = 31
N_BRANCH = 3
FF_HIDDEN = -(-8 * D_MODEL // (3 * 256)) * 256
ROPE_THETA = 10000.0
EPS = 1e-6

OFF_K = DA_WIDTH
OFF_V = 2 * DA_WIDTH
OFF_SC = 3 * DA_WIDTH
OFF_CF = OFF_SC + 3 * SC_WIDTH
OFF_GATE = OFF_CF + 2 * CF_WIDTH
IN_COLS = OFF_GATE + N_BRANCH * D_MODEL

N_CTX = BATCH * SEQ
N_LAT = DEC_BATCH * DEC_SEQ
N_TOK = N_CTX + N_LAT
MOD_ROWS = 16

F32 = jnp.float32
BF16 = jnp.bfloat16

VMEM_BUDGET_V7X = 56 * 1024 * 1024
LANES = 128
SUBLANES = 8

TM_NORM = 512
TM_IN = 2048
TM_MERGE = 512
TM_OUT = 512
TN = 512
TM_FFN = 512
TH_FFN = 512
TQ_LAT = 1024
HEADS_LAT = 2
HEADS_CTX = 8
TQ_SUB = 256
LOG2_E = math.log2(math.e)
NORM_ROWS = 128
CONV_ROWS = 32


def _nbytes(shape, dtype):
    return math.prod(shape) * jnp.dtype(dtype).itemsize


def _vmem_limit(blocks, scratch=(), temps=()):
    total = 2 * sum(_nbytes(s, d) for s, d in blocks)
    total += sum(_nbytes(s, d) for s, d in scratch)
    total += sum(_nbytes(s, d) for s, d in temps)
    return min(total, VMEM_BUDGET_V7X)


def _mod_group(i, tm):
    return jnp.maximum(i * tm - N_CTX + DEC_SEQ, 0) // DEC_SEQ


def _rms(x, g):
    return x * lax.rsqrt(jnp.mean(x * x, axis=-1, keepdims=True) + EPS) * g


def _mod_norm_rows(o_ref, x_refs, n_rows, g, shift, scale):
    gain = g * (1.0 + scale)
    is_ctx = pl.program_id(0) < N_CTX // n_rows

    def load(rows):
        if len(x_refs) == 2:
            return jnp.where(is_ctx, x_refs[0][rows, :], x_refs[1][rows, :])
        return x_refs[0][rows, :]

    for c in range(n_rows // NORM_ROWS):
        rows = slice(c * NORM_ROWS, (c + 1) * NORM_ROWS)
        x = load(rows)
        r = lax.rsqrt(jnp.mean(x * x, axis=-1, keepdims=True) + EPS)
        o_ref[rows, :] = ((load(rows) * r) * gain + shift).astype(BF16)


def _sigmoid(x):
    return 0.5 * jnp.tanh(0.5 * x) + 0.5


def _mod_kernel(c_ref, w_ref, b_ref, o_ref):
    c = c_ref[...]
    s = (c * _sigmoid(c)).astype(BF16)
    o_ref[...] = jnp.dot(s, w_ref[...].astype(BF16), preferred_element_type=F32) + b_ref[...]


def _modulation(cvec, w_mod, b_mod):
    bn = 1024
    return pl.pallas_call(
        _mod_kernel,
        grid=(DEPTH, 6 * D_MODEL // bn),
        in_specs=[
            pl.BlockSpec((MOD_ROWS, D_MODEL), lambda l, j: (0, 0)),
            pl.BlockSpec((None, D_MODEL, bn), lambda l, j: (l, 0, j)),
            pl.BlockSpec((None, 1, bn), lambda l, j: (l, 0, j)),
        ],
        out_specs=pl.BlockSpec((None, MOD_ROWS, bn), lambda l, j: (l, 0, j)),
        out_shape=jax.ShapeDtypeStruct((DEPTH, MOD_ROWS, 6 * D_MODEL), F32),
        compiler_params=pltpu.CompilerParams(
            dimension_semantics=("parallel", "parallel"),
            vmem_limit_bytes=_vmem_limit(
                [((D_MODEL, bn), F32), ((MOD_ROWS, D_MODEL), F32), ((MOD_ROWS, bn), F32)],
                temps=[((D_MODEL, bn), BF16), ((D_MODEL, bn), F32)])),
        name="modulation",
    )(cvec, w_mod, b_mod.reshape(DEPTH, 1, 6 * D_MODEL))


def _mod_spec(layer, chunk, tm):
    return pl.BlockSpec((None, None, 1, D_MODEL),
                        lambda i, *_: (layer, _mod_group(i, tm), 0, chunk))


def _group_specs(tm, tn, n_col_tiles=None):
    nct = N_CTX // tm
    if n_col_tiles is None:
        return (pl.BlockSpec((tm, tn), lambda i, *_: (jnp.minimum(i, nct - 1), 0)),
                pl.BlockSpec((tm, tn), lambda i, *_: (jnp.maximum(i - nct, 0), 0)))
    last = n_col_tiles - 1
    return (pl.BlockSpec((tm, tn), lambda i, j: (jnp.minimum(i, nct - 1), jnp.where(i < nct, j, last))),
            pl.BlockSpec((tm, tn), lambda i, j: (jnp.maximum(i - nct, 0), jnp.where(i < nct, 0, j))))


def _group_tile(xc_ref, xl_ref, tm):
    return jnp.where(pl.program_id(0) < N_CTX // tm, xc_ref[...], xl_ref[...])


def _prenorm_kernel(xc_ref, xl_ref, g_ref, shift_ref, scale_ref, o_ref):
    _mod_norm_rows(o_ref, (xc_ref, xl_ref), TM_NORM, g_ref[...], shift_ref[...], scale_ref[...])


def _pre_norm(x_pair, g, mod4, layer):
    return pl.pallas_call(
        _prenorm_kernel,
        grid=(N_TOK // TM_NORM,),
        in_specs=[
            *_group_specs(TM_NORM, D_MODEL),
            pl.BlockSpec((1, D_MODEL), lambda i: (0, 0)),
            _mod_spec(layer, 0, TM_NORM),
            _mod_spec(layer, 1, TM_NORM),
        ],
        out_specs=pl.BlockSpec((TM_NORM, D_MODEL), lambda i: (i, 0)),
        out_shape=jax.ShapeDtypeStruct((N_TOK, D_MODEL), BF16),
        compiler_params=pltpu.CompilerParams(
            dimension_semantics=("arbitrary",),
            vmem_limit_bytes=_vmem_limit(
                [((TM_NORM, D_MODEL), F32)] * 2 + [((TM_NORM, D_MODEL), BF16)],
                temps=[((TM_NORM, D_MODEL), F32)] * 3)),
        name="pre_norm",
    )(*x_pair, g, mod4, mod4)


N_CTX_TILES = N_CTX // TM_IN
KV_TILES = DA_WIDTH // TN


MIX_COLS = OFF_GATE
GATE_COLS = N_BRANCH * D_MODEL


def _cast_weight_tile(w_ref, wbf_scr):
    @pl.when(pl.program_id(1) == 0)
    def _():
        wbf_scr[...] = w_ref[...].astype(BF16)


def _inproj_mix_kernel(h_ref, w_ref, nk_hbm, nv_hbm, o_ref, nk_ref, nv_ref, wbf_scr):
    del nk_hbm, nv_hbm
    j, i = pl.program_id(0), pl.program_id(1)
    _cast_weight_tile(w_ref, wbf_scr)
    o_ref[...] = jnp.dot(h_ref[...], wbf_scr[...], preferred_element_type=F32)

    @pl.when((i < N_CTX_TILES) & (j >= OFF_K // TN) & (j < OFF_V // TN))
    def _():
        nk_ref[...] = o_ref[...].reshape(nk_ref.shape)

    @pl.when((i < N_CTX_TILES) & (j >= OFF_V // TN) & (j < OFF_SC // TN))
    def _():
        nv_ref[...] = o_ref[...].reshape(nv_ref.shape)


def _inproj_gate_kernel(h_ref, wa_ref, wb_ref, o_ref, wa_scr, wb_scr):
    _cast_weight_tile(wa_ref, wa_scr)
    _cast_weight_tile(wb_ref, wb_scr)
    h = h_ref[...]
    o_ref[:, :TN] = jnp.dot(h, wa_scr[...], preferred_element_type=F32).astype(BF16)
    o_ref[:, TN:] = jnp.dot(h, wb_scr[...], preferred_element_type=F32).astype(BF16)


def _kv_cache_spec(layer, col0):
    def index(j, i):
        rel = j - col0
        row = jnp.where(rel < 0, 0,
                        jnp.where(rel >= KV_TILES, N_CTX_TILES - 1, jnp.minimum(i, N_CTX_TILES - 1)))
        return (row, layer, 0, jnp.clip(rel, 0, KV_TILES - 1))
    return pl.BlockSpec((TM_IN // SEQ, None, SEQ, TN), index)


def _in_projection(h, layer, w_in, new_kv):
    h_spec = pl.BlockSpec((TM_IN, D_MODEL), lambda j, i: (i, 0))
    kv_sds = jax.ShapeDtypeStruct((BATCH, DEPTH, SEQ, DA_WIDTH), F32)
    kv_block = ((TM_IN // SEQ, SEQ, TN), F32)
    hbm = pl.BlockSpec(memory_space=pl.ANY)
    mix, new_k, new_v = pl.pallas_call(
        _inproj_mix_kernel,
        grid=(MIX_COLS // TN, N_TOK // TM_IN),
        in_specs=[h_spec, pl.BlockSpec((None, D_MODEL, TN), lambda j, i: (layer, 0, j)), hbm, hbm],
        out_specs=[pl.BlockSpec((TM_IN, TN), lambda j, i: (i, j)),
                   _kv_cache_spec(layer, OFF_K // TN), _kv_cache_spec(layer, OFF_V // TN)],
        out_shape=[jax.ShapeDtypeStruct((N_TOK, MIX_COLS), F32), kv_sds, kv_sds],
        scratch_shapes=[pltpu.VMEM((D_MODEL, TN), BF16)],
        input_output_aliases={2: 1, 3: 2},
        compiler_params=pltpu.CompilerParams(
            dimension_semantics=("arbitrary", "arbitrary"),
            vmem_limit_bytes=_vmem_limit(
                [((TM_IN, D_MODEL), BF16), ((D_MODEL, TN), F32), ((TM_IN, TN), F32), kv_block, kv_block],
                scratch=[((D_MODEL, TN), BF16)], temps=[((TM_IN, TN), F32)])),
        name="in_projection_mix",
    )(h, w_in, *new_kv)
    gates = pl.pallas_call(
        _inproj_gate_kernel,
        grid=(GATE_COLS // (2 * TN), N_TOK // TM_IN),
        in_specs=[h_spec,
                  pl.BlockSpec((None, D_MODEL, TN), lambda j, i: (layer, 0, MIX_COLS // TN + 2 * j)),
                  pl.BlockSpec((None, D_MODEL, TN), lambda j, i: (layer, 0, MIX_COLS // TN + 2 * j + 1))],
        out_specs=pl.BlockSpec((TM_IN, 2 * TN), lambda j, i: (i, j)),
        out_shape=jax.ShapeDtypeStruct((N_TOK, GATE_COLS), BF16),
        scratch_shapes=[pltpu.VMEM((D_MODEL, TN), BF16)] * 2,
        compiler_params=pltpu.CompilerParams(
            dimension_semantics=("arbitrary", "arbitrary"),
            vmem_limit_bytes=_vmem_limit(
                [((TM_IN, D_MODEL), BF16), ((D_MODEL, 2 * TN), F32), ((TM_IN, 2 * TN), BF16)],
                scratch=[((D_MODEL, 2 * TN), BF16)],
                temps=[((TM_IN, 2 * TN), F32), ((TM_IN, 2 * TN), BF16)])),
        name="in_projection_gates",
    )(h, w_in, w_in)
    return mix, gates, (new_k, new_v)


def _rope(x, cos, sin_lo, sin_hi):
    return (x * cos + pltpu.roll(x, LANES - 16, 1) * sin_lo + pltpu.roll(x, 16, 1) * sin_hi)


def _make_attn_kernel(seq_len, tq, n_cache, n_heads, use_rope, lam_init, n_cast):
    def kernel(*refs):
        dl_ref, g_ref, q_ref, k_ref, v_ref = refs[:5]
        refs = refs[5:]
        if n_cache:
            ck_ref, cv_ref = refs[:2]
            refs = refs[2:]
        if use_rope:
            cos_ref, slo_ref, shi_ref = refs[:3]
            refs = refs[3:]
        cast_in, refs = refs[:n_cast], refs[n_cast:]
        o_ref, refs = refs[0], refs[1:]
        cast_out, refs = refs[:n_cast], refs[n_cast:]
        kk_scr, vv_scr = refs
        qb = pl.program_id(2)
        for src, dst in zip(cast_in, cast_out):
            dst[...] = src[...].astype(BF16)

        @pl.when(qb == 0)
        def _():
            for h in range(n_heads):
                cols = slice(h * LANES, (h + 1) * LANES)
                k = k_ref[:, cols]
                if use_rope:
                    k = _rope(k, cos_ref[...], slo_ref[...], shi_ref[...])
                if n_cache:
                    head = pl.program_id(1) * n_heads + h
                    rows = pl.ds(head, n_cache, stride=DA_HEADS)
                    kk_scr[h, 0:n_cache, :] = ck_ref[rows, :].astype(BF16)
                    vv_scr[h, 0:n_cache, 0:DA_VDIM] = cv_ref[rows, :].astype(BF16)
                kk_scr[h, n_cache:n_cache + seq_len, :] = k.astype(BF16)
                vv_scr[h, n_cache:n_cache + seq_len, 0:DA_VDIM] = v_ref[:, cols].astype(BF16)
                vv_scr[h, :, DA_VDIM:] = jnp.ones((n_cache + seq_len, DA_VDIM), BF16)

        dl = dl_ref[...]
        lam = (jnp.exp(jnp.sum(dl[0:1] * dl[1:2], axis=-1, keepdims=True))
               - jnp.exp(jnp.sum(dl[2:3] * dl[3:4], axis=-1, keepdims=True)) + lam_init)
        tsub = min(tq, TQ_SUB)
        lane = lax.broadcasted_iota(jnp.int32, (tsub, LANES), 1)

        def softmax_times_v(qm, kk, vv):
            s = lax.dot_general(qm.astype(BF16), kk, (((1,), (1,)), ((), ())), preferred_element_type=F32)
            e = jnp.exp2(s - jnp.max(s, axis=-1, keepdims=True)).astype(BF16)
            ev = jnp.dot(e, vv, preferred_element_type=F32)
            return ev[:, :DA_VDIM] / ev[:, DA_VDIM:]

        for h in range(n_heads):
            cols = slice(h * LANES, (h + 1) * LANES)
            kk, vv = kk_scr[h], vv_scr[h]
            for c in range(tq // tsub):
                q = q_ref[c * tsub:(c + 1) * tsub, cols]
                if use_rope:
                    rows = pl.ds(pl.multiple_of(qb * tq, tq) + c * tsub, tsub)
                    q = _rope(q, cos_ref[rows, :], slo_ref[rows, :], shi_ref[rows, :])
                q = q * (DA_HEAD_DIM ** -0.5 * LOG2_E)
                o1 = softmax_times_v(jnp.where(lane < DA_HEAD_DIM, q, 0.0), kk, vv)
                o2 = softmax_times_v(jnp.where(lane >= DA_HEAD_DIM, q, 0.0), kk, vv)
                o_ref[c * tsub:(c + 1) * tsub, cols] = (
                    _rms(o1 - lam * o2, g_ref[...]) * (1.0 - lam_init)).astype(BF16)

    return kernel


def _diff_attention(proj, da_lambda_l, subln_g, layer, *, latent, cache=None, rope=None, cast_weights=()):
    lam_init = 0.8 - 0.6 * math.exp(-0.3 * layer)
    if latent:
        n_seq, seq_len, tq, n_cache, row0, n_heads = DEC_BATCH, DEC_SEQ, TQ_LAT, PAST_LEN, N_CTX, HEADS_LAT
    else:
        n_seq, seq_len, tq, n_cache, row0, n_heads = BATCH, SEQ, SEQ, 0, 0, HEADS_CTX
    nq = seq_len // tq
    lk = n_cache + seq_len
    width = n_heads * LANES
    hk, hv = OFF_K // width, OFF_V // width

    in_specs = [
        pl.BlockSpec((4, DA_HEAD_DIM), lambda b, h, t: (0, 0)),
        pl.BlockSpec((1, DA_VDIM), lambda b, h, t: (0, 0)),
        pl.BlockSpec((tq, width), lambda b, h, t: (row0 // tq + b * nq + t, h)),
        pl.BlockSpec((seq_len, width), lambda b, h, t: (row0 // seq_len + b, hk + h)),
        pl.BlockSpec((seq_len, width), lambda b, h, t: (row0 // seq_len + b, hv + h)),
    ]
    args = [da_lambda_l, subln_g, proj, proj, proj]
    blocks = [((tq, width), F32), ((seq_len, width), F32), ((seq_len, width), F32), ((tq, width), BF16)]
    if latent:
        cache_k, cache_v = cache
        cache_block = (n_cache * DA_HEADS, LANES)
        in_specs += [pl.BlockSpec((None, None) + cache_block, lambda b, h, t: (b, layer, 0, 0))] * 2
        args += [cache_k, cache_v]
        in_specs += [pl.BlockSpec((seq_len, LANES), lambda b, h, t: (0, 0))] * 3
        args += list(rope)
        blocks += [(cache_block, F32)] * 2 + [((seq_len, LANES), F32)] * 3

    n_hgroups = DA_HEADS // n_heads
    n_steps = n_seq * n_hgroups * nq
    out_specs = [pl.BlockSpec((tq, width), lambda b, h, t: (b * nq + t, h))]
    out_shape = [jax.ShapeDtypeStruct((n_seq * seq_len, DA_WIDTH), BF16)]
    for w in cast_weights:
        _, rows, cols = w.shape
        slab = rows // n_steps
        assert slab * n_steps == rows and slab % 16 == 0, (w.shape, n_steps)

        def step(b, h, t):
            return (b * n_hgroups + h) * nq + t

        in_specs.append(pl.BlockSpec((None, slab, cols), lambda b, h, t: (layer, step(b, h, t), 0)))
        args.append(w)
        out_specs.append(pl.BlockSpec((slab, cols), lambda b, h, t: (step(b, h, t), 0)))
        out_shape.append(jax.ShapeDtypeStruct((rows, cols), BF16))
        blocks += [((slab, cols), F32), ((slab, cols), BF16)]

    tsub = min(tq, TQ_SUB)
    scratch = [((n_heads, lk, LANES), BF16), ((n_heads, lk, 2 * DA_VDIM), BF16)]
    return pl.pallas_call(
        _make_attn_kernel(seq_len, tq, n_cache, n_heads, latent, lam_init, len(cast_weights)),
        grid=(n_seq, n_hgroups, nq),
        in_specs=in_specs,
        out_specs=out_specs,
        out_shape=out_shape,
        scratch_shapes=[pltpu.VMEM(s, d) for s, d in scratch],
        compiler_params=pltpu.CompilerParams(
            dimension_semantics=("parallel", "parallel", "arbitrary"),
            vmem_limit_bytes=_vmem_limit(
                blocks, scratch=scratch,
                temps=[((tsub, lk), F32)] * 6 * 4 + [((tq, 2 * DA_VDIM), F32)] * 4)),
        name="diff_attention_latent" if latent else "diff_attention_context",
    )(*args)


def _fill_shifted(scr, value, pad, seq_len, shifts):
    width = value.shape[-1]
    n = seq_len + 2 * pad - SUBLANES
    scr[0, 0:pad, :] = jnp.zeros((pad, width), F32)
    scr[0, pad + seq_len:, :] = jnp.zeros((pad, width), F32)
    scr[0, pad:pad + seq_len, :] = value
    for s, shift in enumerate(shifts, start=1):
        scr[s, 0:n, :] = scr[0, shift:shift + n, :]


def _tap(scr, shifts, r0, offset):
    aligned, rem = offset - offset % SUBLANES, offset % SUBLANES
    slot = 0 if rem == 0 else 1 + shifts.index(rem)
    return scr[slot, pl.ds(r0 + aligned, CONV_ROWS), :]


def _make_conv_kernel(seq_len):
    pad3, pad31 = SUBLANES, 2 * SUBLANES
    shifts3 = (1, SUBLANES - 1)
    shifts31 = tuple(range(1, SUBLANES))

    def kernel(gb_ref, gc_ref, sx_ref, ca_ref, cb_ref, w3_ref, w31_ref, b31_ref, lng_ref, lnb_ref,
               bo_ref, co_ref, t_scr, u_scr, y_scr):
        _fill_shifted(t_scr, gc_ref[...] * sx_ref[...], pad3, seq_len, shifts3)
        _fill_shifted(u_scr, ca_ref[...] * _sigmoid(cb_ref[...]), pad31, seq_len, shifts31)

        def chunk(c, carry):
            r0 = pl.multiple_of(c * CONV_ROWS, CONV_ROWS)
            rows = pl.ds(r0, CONV_ROWS)
            y = _tap(t_scr, shifts3, r0, pad3 - SC_K // 2) * w3_ref[0:1, :]
            for k in range(1, SC_K):
                y = y + _tap(t_scr, shifts3, r0, pad3 - SC_K // 2 + k) * w3_ref[k:k + 1, :]
            bo_ref[rows, :] = (gb_ref[rows, :] * y).astype(BF16)

            acc = _tap(u_scr, shifts31, r0, pad31 - CF_K // 2) * w31_ref[0:1, :]
            for k in range(1, CF_K):
                acc = acc + _tap(u_scr, shifts31, r0, pad31 - CF_K // 2 + k) * w31_ref[k:k + 1, :]
            y_scr[rows, :] = acc + b31_ref[...]
            return carry

        lax.fori_loop(0, seq_len // CONV_ROWS, chunk, 0)

        y = y_scr[...]
        xc = y - jnp.mean(y, axis=-1, keepdims=True)
        yn = xc * lax.rsqrt(jnp.mean(xc * xc, axis=-1, keepdims=True) + EPS) * lng_ref[...] + lnb_ref[...]
        co_ref[...] = (yn * _sigmoid(yn)).astype(BF16)

    return kernel


def _conv_branches(proj, sc_conv_l, cf_conv_l, cf_b, ln_g, ln_b, *, latent):
    if latent:
        n_seq, seq_len, row0 = DEC_BATCH, DEC_SEQ, N_CTX
    else:
        n_seq, seq_len, row0 = BATCH, SEQ, 0
    rb = row0 // seq_len
    c0 = OFF_SC // SC_WIDTH

    def col(cidx):
        return pl.BlockSpec((seq_len, SC_WIDTH), lambda b: (rb + b, cidx))

    def whole(shape):
        return pl.BlockSpec(shape, lambda b: (0, 0))

    in_specs = [col(c0), col(c0 + 1), col(c0 + 2), col(c0 + 3), col(c0 + 4),
                whole((SC_K, SC_WIDTH)), whole((CF_K, CF_WIDTH)),
                whole((1, CF_WIDTH)), whole((1, CF_WIDTH)), whole((1, CF_WIDTH))]
    args = [proj, proj, proj, proj, proj, sc_conv_l, cf_conv_l, cf_b, ln_g, ln_b]
    out_spec = pl.BlockSpec((seq_len, SC_WIDTH), lambda b: (b, 0))
    out_sds = jax.ShapeDtypeStruct((n_seq * seq_len, SC_WIDTH), BF16)
    scratch = [((3, seq_len + 2 * SUBLANES, SC_WIDTH), F32),
               ((SUBLANES, seq_len + 4 * SUBLANES, CF_WIDTH), F32),
               ((seq_len, CF_WIDTH), F32)]
    return pl.pallas_call(
        _make_conv_kernel(seq_len),
        grid=(n_seq,),
        in_specs=in_specs,
        out_specs=[out_spec, out_spec],
        out_shape=[out_sds, out_sds],
        scratch_shapes=[pltpu.VMEM(s, d) for s, d in scratch],
        compiler_params=pltpu.CompilerParams(
            dimension_semantics=("parallel",),
            vmem_limit_bytes=_vmem_limit(
                [((seq_len, SC_WIDTH), F32)] * 5 + [((seq_len, SC_WIDTH), BF16)] * 2,
                scratch=scratch,
                temps=[((seq_len, SC_WIDTH), F32)] * 4)),
        name="conv_branches_latent" if latent else "conv_branches_context",
    )(*args)


def _merge_kernel(ac_ref, al_ref, bc_ref, bl_ref, cc_ref, cl_ref, ga_ref, gb_ref, gc_ref,
                  bga_ref, bgb_ref, bgc_ref, wa_ref, wb_ref, wc_ref, o_ref):
    def merge(a_ref, b_ref, c_ref):
        br_a = jnp.dot(a_ref[...], wa_ref[...], preferred_element_type=F32)
        br_b = jnp.dot(b_ref[...], wb_ref[...], preferred_element_type=F32)
        br_c = jnp.dot(c_ref[...], wc_ref[...], preferred_element_type=F32)
        merged = (_sigmoid(ga_ref[...].astype(F32) + bga_ref[...]) * br_a
                  + _sigmoid(gb_ref[...].astype(F32) + bgb_ref[...]) * br_b
                  + _sigmoid(gc_ref[...].astype(F32) + bgc_ref[...]) * br_c)
        o_ref[...] = merged.astype(BF16)

    is_ctx = pl.program_id(0) < N_CTX // TM_MERGE
    pl.when(is_ctx)(lambda: merge(ac_ref, bc_ref, cc_ref))
    pl.when(jnp.logical_not(is_ctx))(lambda: merge(al_ref, bl_ref, cl_ref))


def _gated_merge(attn_pair, b_pair, c_pair, gates, b_gate_l, w_da, w_sc, w_cf):
    tm = TM_MERGE

    def gate(k):
        return pl.BlockSpec((tm, D_MODEL), lambda i: (i, k))

    def gbias(k):
        return pl.BlockSpec((1, D_MODEL), lambda i: (0, k))

    def resident(rows):
        return pl.BlockSpec((rows, D_MODEL), lambda i: (0, 0))

    return pl.pallas_call(
        _merge_kernel,
        grid=(N_TOK // tm,),
        in_specs=[
            *_group_specs(tm, DA_WIDTH), *_group_specs(tm, SC_WIDTH), *_group_specs(tm, CF_WIDTH),
            gate(0), gate(1), gate(2), gbias(0), gbias(1), gbias(2),
            resident(DA_WIDTH), resident(SC_WIDTH), resident(CF_WIDTH),
        ],
        out_specs=pl.BlockSpec((tm, D_MODEL), lambda i: (i, 0)),
        out_shape=jax.ShapeDtypeStruct((N_TOK, D_MODEL), BF16),
        compiler_params=pltpu.CompilerParams(
            dimension_semantics=("arbitrary",),
            vmem_limit_bytes=_vmem_limit(
                [((tm, DA_WIDTH), BF16), ((tm, SC_WIDTH), BF16), ((tm, CF_WIDTH), BF16)] * 2
                + [((tm, D_MODEL), BF16)] * 4
                + [((DA_WIDTH, D_MODEL), BF16), ((SC_WIDTH, D_MODEL), BF16), ((CF_WIDTH, D_MODEL), BF16)],
                temps=[((tm, D_MODEL), F32)] * 8)),
        name="gated_merge",
    )(*attn_pair, *b_pair, *c_pair, gates, gates, gates, b_gate_l, b_gate_l, b_gate_l, w_da, w_sc, w_cf)


def _outproj_kernel(m_ref, w_ref, xc_ref, xl_ref, gate_ref, o_ref):
    x = _group_tile(xc_ref, xl_ref, TM_OUT)
    o_ref[...] = x + gate_ref[...] * jnp.dot(m_ref[...], w_ref[...], preferred_element_type=F32)


def _out_projection(merged, w_out, x_pair, mod4, layer):
    return pl.pallas_call(
        _outproj_kernel,
        grid=(N_TOK // TM_OUT,),
        in_specs=[
            pl.BlockSpec((TM_OUT, D_MODEL), lambda i: (i, 0)),
            pl.BlockSpec((D_MODEL, D_MODEL), lambda i: (0, 0)),
            *_group_specs(TM_OUT, D_MODEL),
            _mod_spec(layer, 2, TM_OUT),
        ],
        out_specs=pl.BlockSpec((TM_OUT, D_MODEL), lambda i: (i, 0)),
        out_shape=jax.ShapeDtypeStruct((N_TOK, D_MODEL), F32),
        compiler_params=pltpu.CompilerParams(
            dimension_semantics=("arbitrary",),
            vmem_limit_bytes=_vmem_limit(
                [((TM_OUT, D_MODEL), BF16), ((D_MODEL, D_MODEL), BF16)] + [((TM_OUT, D_MODEL), F32)] * 3,
                temps=[((TM_OUT, D_MODEL), F32)] * 2)),
        name="out_projection",
    )(merged, w_out, *x_pair, mod4)


def _make_ffn_kernel(final_norm):
    def kernel(*refs):
        x_ref, g_ref, shift_ref, scale_ref, gate_ref, wu_ref, ww_ref, wo_ref = refs[:8]
        refs = refs[8:]
        if final_norm:
            gf_ref = refs[0]
            refs = refs[1:]
        y_ctx_ref, y_lat_ref, h_scr, acc_scr = refs
        i, j = pl.program_id(0), pl.program_id(1)

        @pl.when(j == 0)
        def _():
            _mod_norm_rows(h_scr, (x_ref,), TM_FFN, g_ref[...], shift_ref[...], scale_ref[...])
            acc_scr[...] = jnp.zeros_like(acc_scr)

        h = h_scr[...]
        u = jnp.dot(h, wu_ref[...], preferred_element_type=F32)
        w = jnp.dot(h, ww_ref[...], preferred_element_type=F32)
        act = (u * _sigmoid(u) * w).astype(BF16)
        acc_scr[...] += jnp.dot(act, wo_ref[...], preferred_element_type=F32)

        def result():
            y = x_ref[...] + gate_ref[...] * acc_scr[...]
            return _rms(y, gf_ref[...]) if final_norm else y

        last = j == pl.num_programs(1) - 1

        @pl.when(last & (i < N_CTX // TM_FFN))
        def _():
            y_ctx_ref[...] = result()

        @pl.when(last & (i >= N_CTX // TM_FFN))
        def _():
            y_lat_ref[...] = result()

    return kernel


def _swiglu(x, g, mod4, layer, w_ffn_in, w_ffn_out, g_final=None):
    final_norm = g_final is not None
    nh = FF_HIDDEN // TH_FFN
    n_ctx_tiles = N_CTX // TM_FFN
    row = pl.BlockSpec((1, D_MODEL), lambda i, j: (0, 0))
    in_specs = [
        pl.BlockSpec((TM_FFN, D_MODEL), lambda i, j: (i, 0)),
        row,
        _mod_spec(layer, 3, TM_FFN), _mod_spec(layer, 4, TM_FFN), _mod_spec(layer, 5, TM_FFN),
        pl.BlockSpec((D_MODEL, TH_FFN), lambda i, j: (0, j)),
        pl.BlockSpec((D_MODEL, TH_FFN), lambda i, j: (0, nh + j)),
        pl.BlockSpec((TH_FFN, D_MODEL), lambda i, j: (j, 0)),
    ]
    args = [x, g, mod4, mod4, mod4, w_ffn_in, w_ffn_in, w_ffn_out]
    if final_norm:
        in_specs.append(row)
        args.append(g_final)
    out_specs = [pl.BlockSpec((TM_FFN, D_MODEL), lambda i, j: (jnp.minimum(i, n_ctx_tiles - 1), 0)),
                 pl.BlockSpec((TM_FFN, D_MODEL), lambda i, j: (jnp.maximum(i - n_ctx_tiles, 0), 0))]
    out_shape = [jax.ShapeDtypeStruct((N_CTX, D_MODEL), F32), jax.ShapeDtypeStruct((N_LAT, D_MODEL), F32)]
    scratch = [((TM_FFN, D_MODEL), BF16), ((TM_FFN, D_MODEL), F32)]
    return pl.pallas_call(
        _make_ffn_kernel(final_norm),
        grid=(N_TOK // TM_FFN, nh),
        in_specs=in_specs,
        out_specs=out_specs,
        out_shape=out_shape,
        scratch_shapes=[pltpu.VMEM(s, d) for s, d in scratch],
        compiler_params=pltpu.CompilerParams(
            dimension_semantics=("arbitrary", "arbitrary"),
            vmem_limit_bytes=_vmem_limit(
                [((TM_FFN, D_MODEL), F32)] * 3 + [((D_MODEL, TH_FFN), BF16)] * 2
                + [((TH_FFN, D_MODEL), BF16)],
                scratch=scratch,
                temps=[((TM_FFN, TH_FFN), F32)] * 4 + [((TM_FFN, D_MODEL), F32)] * 2)),
        name="swiglu_final" if final_norm else "swiglu",
    )(*args)


def _rope_tables():
    rows = DEC_SEQ // GRID_W
    row_ids = jnp.repeat(jnp.arange(rows), GRID_W).astype(F32)
    col_ids = jnp.tile(jnp.arange(GRID_W), rows).astype(F32)
    n_freq = DA_HEAD_DIM // 4
    inv = ROPE_THETA ** (-jnp.arange(n_freq, dtype=F32) / n_freq)
    ang_r, ang_c = row_ids[:, None] * inv, col_ids[:, None] * inv
    zero = jnp.zeros_like(ang_r)
    cos = jnp.concatenate([jnp.cos(ang_r)] * 2 + [jnp.cos(ang_c)] * 2, axis=-1)
    sin_lo = jnp.concatenate([-jnp.sin(ang_r), zero, -jnp.sin(ang_c), zero], axis=-1)
    sin_hi = jnp.concatenate([zero, jnp.sin(ang_r), zero, jnp.sin(ang_c)], axis=-1)
    return tuple(jnp.tile(t, (1, LANES // DA_HEAD_DIM)) for t in (cos, sin_lo, sin_hi))


def kernel(x_prompt, x_sample, cache_k, cache_v, c, c_ctx, w_mod, b_mod, g_norm1, w_in, da_lambda, da_subln,
           w_da_out, sc_conv, w_sc_out, cf_conv, cf_conv_b, cf_ln_g, cf_ln_b, w_cf_out, b_gate, w_out,
           g_norm2, w_ffn_in, w_ffn_out, g_final):
    x_pair = (x_prompt.reshape(N_CTX, D_MODEL), x_sample.reshape(N_LAT, D_MODEL))
    cvec =jnp.concatenate([c_ctx[None, :], c, jnp.zeros((MOD_ROWS - 1 - DEC_BATCH, D_MODEL), F32)], axis=0)
    mod4 = _modulation(cvec, w_mod, b_mod).reshape(DEPTH, MOD_ROWS, 1, 6 * D_MODEL)
    rope = _rope_tables()
    cache_k = cache_k.reshape(DEC_BATCH, DEPTH, PAST_LEN * DA_HEADS, 2 * DA_HEAD_DIM)
    cache_v = cache_v.reshape(DEC_BATCH, DEPTH, PAST_LEN * DA_HEADS, DA_VDIM)

    new_kv = tuple(jnp.zeros((BATCH, DEPTH, SEQ, DA_WIDTH), F32) for _ in range(2))
    for l in range(DEPTH):
        h = _pre_norm(x_pair, g_norm1[l][None, :], mod4, l)
        proj, gates, new_kv = _in_projection(h, l, w_in, new_kv)

        subln = da_subln[l][None, :]
        attn_ctx, w_out_bf, w_da_bf, w_sc_bf, w_cf_bf = _diff_attention(
            proj, da_lambda[l], subln, l, latent=False, cast_weights=(w_out, w_da_out, w_sc_out, w_cf_out))
        attn_lat, w_ffn_in_bf, w_ffn_out_bf = _diff_attention(
            proj, da_lambda[l], subln, l, latent=True, cache=(cache_k, cache_v), rope=rope,
            cast_weights=(w_ffn_in, w_ffn_out))
        attn_pair = (attn_ctx, attn_lat)
        conv_args = (sc_conv[l], cf_conv[l], cf_conv_b[l][None, :], cf_ln_g[l][None, :], cf_ln_b[l][None, :])
        b_ctx, c_ctx_pre = _conv_branches(proj, *conv_args, latent=False)
        b_lat, c_lat = _conv_branches(proj, *conv_args, latent=True)

        merged = _gated_merge(attn_pair, (b_ctx, b_lat), (c_ctx_pre, c_lat), gates, b_gate[l][None, :],
                              w_da_bf, w_sc_bf, w_cf_bf)
        x_mid = _out_projection(merged, w_out_bf, x_pair, mod4, l)
        x_pair = _swiglu(x_mid, g_norm2[l][None, :], mod4, l, w_ffn_in_bf, w_ffn_out_bf,
                         g_final[None, :] if l == DEPTH - 1 else None)

    y_prompt, y_sample = x_pair
    new_k, new_v = new_kv
    return (y_prompt.reshape(BATCH, SEQ, D_MODEL), y_sample.reshape(DEC_BATCH, DEC_SEQ, D_MODEL),
            new_k.reshape(BATCH, DEPTH, SEQ, DA_HEADS, 2 * DA_HEAD_DIM),
            new_v.reshape(BATCH, DEPTH, SEQ, DA_HEADS, DA_VDIM))
```

```python
import math

import jax
import jax.numpy as jnp
from jax import lax
from jax.experimental import pallas as pl
from jax.experimental.pallas import tpu as pltpu

D_MODEL = 2048
BATCH = 16
SEQ = 256
DEPTH = 2
DEC_BATCH = 8
DEC_SEQ = 1024
PAST_LEN = 256
GRID_W = 64
DA_HEADS = 8
DA_HEAD_DIM = 64
DA_VDIM = 2 * DA_HEAD_DIM
DA_WIDTH = DA_HEADS * 2 * DA_HEAD_DIM
SC_WIDTH = 512
SC_K = 3
CF_WIDTH = 512
CF_K = 31
N_BRANCH = 3
FF_HIDDEN = -(-8 * D_MODEL // (3 * 256)) * 256
ROPE_THETA = 10000.0
EPS = 1e-6

OFF_K = DA_WIDTH
OFF_V = 2 * DA_WIDTH
OFF_SC = 3 * DA_WIDTH
OFF_CF = OFF_SC + 3 * SC_WIDTH
OFF_GATE = OFF_CF + 2 * CF_WIDTH
IN_COLS = OFF_GATE + N_BRANCH * D_MODEL

N_CTX = BATCH * SEQ
N_LAT = DEC_BATCH * DEC_SEQ
N_TOK = N_CTX + N_LAT
MOD_ROWS = 16

F32 = jnp.float32
BF16 = jnp.bfloat16

VMEM_BUDGET_V7X = 56 * 1024 * 1024
LANES = 128
SUBLANES = 8

TM_NORM = 512
TM_IN = 256
TM_MERGE = 512
TM_OUT = 512
TM_FFN = 512
TH_FFN = 512
TQ_LAT = 1024
HEADS_LAT = 2
HEADS_CTX = 8
TQ_SUB = 256
LOG2_E = math.log2(math.e)
NORM_ROWS = 128
CONV_ROWS = 32


def _nbytes(shape, dtype):
    return math.prod(shape) * jnp.dtype(dtype).itemsize


def _vmem_limit(blocks, scratch=(), temps=()):
    total = 2 * sum(_nbytes(s, d) for s, d in blocks)
    total += sum(_nbytes(s, d) for s, d in scratch)
    total += sum(_nbytes(s, d) for s, d in temps)
    return min(total, VMEM_BUDGET_V7X)


def _mod_group(i, tm):
    return jnp.maximum(i * tm - N_CTX + DEC_SEQ, 0) // DEC_SEQ


def _rms(x, g):
    return x * lax.rsqrt(jnp.mean(x * x, axis=-1, keepdims=True) + EPS) * g


def _mod_norm_rows(o_ref, x_refs, n_rows, g, shift, scale):
    gain = g * (1.0 + scale)
    is_ctx = pl.program_id(0) < N_CTX // n_rows

    def load(rows):
        if len(x_refs) == 2:
            return jnp.where(is_ctx, x_refs[0][rows, :], x_refs[1][rows, :])
        return x_refs[0][rows, :]

    for c in range(n_rows // NORM_ROWS):
        rows = slice(c * NORM_ROWS, (c + 1) * NORM_ROWS)
        x = load(rows)
        r = lax.rsqrt(jnp.mean(x * x, axis=-1, keepdims=True) + EPS)
        o_ref[rows, :] = ((load(rows) * r) * gain + shift).astype(BF16)


def _sigmoid(x):
    return 0.5 * jnp.tanh(0.5 * x) + 0.5


def _mod_kernel(c_ref, w_ref, b_ref, o_ref):
    c = c_ref[...]
    s = (c * _sigmoid(c)).astype(BF16)
    o_ref[...] = jnp.dot(s, w_ref[...].astype(BF16), preferred_element_type=F32) + b_ref[...]


def _modulation(cvec, w_mod, b_mod):
    bn = 1024
    return pl.pallas_call(
        _mod_kernel,
        grid=(DEPTH, 6 * D_MODEL // bn),
        in_specs=[
            pl.BlockSpec((MOD_ROWS, D_MODEL), lambda l, j: (0, 0)),
            pl.BlockSpec((None, D_MODEL, bn), lambda l, j: (l, 0, j)),
            pl.BlockSpec((None, 1, bn), lambda l, j: (l, 0, j)),
        ],
        out_specs=pl.BlockSpec((None, MOD_ROWS, bn), lambda l, j: (l, 0, j)),
        out_shape=jax.ShapeDtypeStruct((DEPTH, MOD_ROWS, 6 * D_MODEL), F32),
        compiler_params=pltpu.CompilerParams(
            dimension_semantics=("parallel", "parallel"),
            vmem_limit_bytes=_vmem_limit(
                [((D_MODEL, bn), F32), ((MOD_ROWS, D_MODEL), F32), ((MOD_ROWS, bn), F32)],
                temps=[((D_MODEL, bn), BF16), ((D_MODEL, bn), F32)])),
        name="modulation",
    )(cvec, w_mod, b_mod.reshape(DEPTH, 1, 6 * D_MODEL))


def _mod_spec(layer, chunk, tm):
    return pl.BlockSpec((None, None, 1, D_MODEL),
                        lambda i, *_: (layer, _mod_group(i, tm), 0, chunk))


def _group_specs(tm, tn, n_col_tiles=None):
    nct = N_CTX // tm
    if n_col_tiles is None:
        return (pl.BlockSpec((tm, tn), lambda i, *_: (jnp.minimum(i, nct - 1), 0)),
                pl.BlockSpec((tm, tn), lambda i, *_: (jnp.maximum(i - nct, 0), 0)))
    last = n_col_tiles - 1
    return (pl.BlockSpec((tm, tn), lambda i, j: (jnp.minimum(i, nct - 1), jnp.where(i < nct, j, last))),
            pl.BlockSpec((tm, tn), lambda i, j: (jnp.maximum(i - nct, 0), jnp.where(i < nct, 0, j))))


def _group_tile(xc_ref, xl_ref, tm):
    return jnp.where(pl.program_id(0) < N_CTX // tm, xc_ref[...], xl_ref[...])


def _prenorm_kernel(xc_ref, xl_ref, g_ref, shift_ref, scale_ref, o_ref):
    _mod_norm_rows(o_ref, (xc_ref, xl_ref), TM_NORM, g_ref[...], shift_ref[...], scale_ref[...])


def _pre_norm(x_pair, g, mod4, layer):
    return pl.pallas_call(
        _prenorm_kernel,
        grid=(N_TOK // TM_NORM,),
        in_specs=[
            *_group_specs(TM_NORM, D_MODEL),
            pl.BlockSpec((1, D_MODEL), lambda i: (0, 0)),
            _mod_spec(layer, 0, TM_NORM),
            _mod_spec(layer, 1, TM_NORM),
        ],
        out_specs=pl.BlockSpec((TM_NORM, D_MODEL), lambda i: (i, 0)),
        out_shape=jax.ShapeDtypeStruct((N_TOK, D_MODEL), BF16),
        compiler_params=pltpu.CompilerParams(
            dimension_semantics=("arbitrary",),
            vmem_limit_bytes=_vmem_limit(
                [((TM_NORM, D_MODEL), F32)] * 2 + [((TM_NORM, D_MODEL), BF16)],
                temps=[((TM_NORM, D_MODEL), F32)] * 3)),
        name="pre_norm",
    )(*x_pair, g, mod4, mod4)


MIX_COLS = OFF_GATE
GATE_COLS = N_BRANCH * D_MODEL


def _inproj_mix_kernel(h_ref, w_ref, nk_hbm, nv_hbm, o_ref, nk_ref, nv_ref):
    del nk_hbm, nv_hbm
    o_ref[...] = jnp.dot(h_ref[...], w_ref[...], preferred_element_type=F32)

    @pl.when(pl.program_id(0) < N_CTX // TM_IN)
    def _():
        nk_ref[...] = o_ref[:, OFF_K:OFF_V]
        nv_ref[...] = o_ref[:, OFF_V:OFF_SC]


def _inproj_gate_kernel(h_ref, w_ref, o_ref):
    o_ref[...] = jnp.dot(h_ref[...], w_ref[...], preferred_element_type=F32).astype(BF16)


def _in_projection(h, layer, w_mix, w_gate, new_kv):
    assert TM_IN == SEQ
    h_spec = pl.BlockSpec((TM_IN, D_MODEL), lambda i: (i, 0))
    kv_sds = jax.ShapeDtypeStruct((BATCH, DEPTH, SEQ, DA_WIDTH), F32)
    kv_spec = pl.BlockSpec((None, None, SEQ, DA_WIDTH), lambda i: (jnp.minimum(i, BATCH - 1), layer, 0, 0))
    hbm = pl.BlockSpec(memory_space=pl.ANY)
    mix, new_k, new_v = pl.pallas_call(
        _inproj_mix_kernel,
        grid=(N_TOK // TM_IN,),
        in_specs=[h_spec, pl.BlockSpec((D_MODEL, MIX_COLS), lambda i: (0, 0)), hbm, hbm],
        out_specs=[pl.BlockSpec((TM_IN, MIX_COLS), lambda i: (i, 0)), kv_spec, kv_spec],
        out_shape=[jax.ShapeDtypeStruct((N_TOK, MIX_COLS), F32), kv_sds, kv_sds],
        input_output_aliases={2: 1, 3: 2},
        compiler_params=pltpu.CompilerParams(
            dimension_semantics=("arbitrary",),
            vmem_limit_bytes=_vmem_limit(
                [((TM_IN, D_MODEL), BF16), ((TM_IN, MIX_COLS), F32)] + [((SEQ, DA_WIDTH), F32)] * 2,
                scratch=[((D_MODEL, MIX_COLS), BF16)], temps=[((TM_IN, MIX_COLS), F32)])),
        name="in_projection_mix",
    )(h, w_mix, *new_kv)
    gates = pl.pallas_call(
        _inproj_gate_kernel,
        grid=(N_TOK // TM_IN,),
        in_specs=[h_spec, pl.BlockSpec((D_MODEL, GATE_COLS), lambda i: (0, 0))],
        out_specs=pl.BlockSpec((TM_IN, GATE_COLS), lambda i: (i, 0)),
        out_shape=jax.ShapeDtypeStruct((N_TOK, GATE_COLS), BF16),
        compiler_params=pltpu.CompilerParams(
            dimension_semantics=("arbitrary",),
            vmem_limit_bytes=_vmem_limit(
                [((TM_IN, D_MODEL), BF16), ((TM_IN, GATE_COLS), BF16)],
                scratch=[((D_MODEL, GATE_COLS), BF16)],
                temps=[((TM_IN, GATE_COLS), F32), ((TM_IN, GATE_COLS), BF16)])),
        name="in_projection_gates",
    )(h, w_gate)
    return mix, gates, (new_k, new_v)


def _rope(x, cos, sin_lo, sin_hi):
    return (x * cos + pltpu.roll(x, LANES - 16, 1) * sin_lo + pltpu.roll(x, 16, 1) * sin_hi)


def _make_attn_kernel(seq_len, tq, n_cache, n_heads, use_rope, lam_init, cast_splits):
    n_cast = len(cast_splits)
    n_cast_out = n_cast + sum(s is not None for s in cast_splits)

    def kernel(*refs):
        dl_ref, g_ref, q_ref, k_ref, v_ref = refs[:5]
        refs = refs[5:]
        if n_cache:
            ck_ref, cv_ref = refs[:2]
            refs = refs[2:]
        if use_rope:
            cos_ref, slo_ref, shi_ref = refs[:3]
            refs = refs[3:]
        cast_in, refs = refs[:n_cast], refs[n_cast:]
        o_ref, refs = refs[0], refs[1:]
        cast_out, refs = list(refs[:n_cast_out]), refs[n_cast_out:]
        kk_scr, vv_scr = refs
        qb = pl.program_id(2)
        for src, split in zip(cast_in, cast_splits):
            if split is None:
                cast_out.pop(0)[...] = src[...].astype(BF16)
            else:
                cast_out.pop(0)[...] = src[:, :split].astype(BF16)
                cast_out.pop(0)[...] = src[:, split:].astype(BF16)

        @pl.when(qb == 0)
        def _():
            for h in range(n_heads):
                cols = slice(h * LANES, (h + 1) * LANES)
                k = k_ref[:, cols]
                if use_rope:
                    k = _rope(k, cos_ref[...], slo_ref[...], shi_ref[...])
                if n_cache:
                    head = pl.program_id(1) * n_heads + h
                    rows = pl.ds(head, n_cache, stride=DA_HEADS)
                    kk_scr[h, 0:n_cache, :] = ck_ref[rows, :].astype(BF16)
                    vv_scr[h, 0:n_cache, 0:DA_VDIM] = cv_ref[rows, :].astype(BF16)
                kk_scr[h, n_cache:n_cache + seq_len, :] = k.astype(BF16)
                vv_scr[h, n_cache:n_cache + seq_len, 0:DA_VDIM] = v_ref[:, cols].astype(BF16)
                vv_scr[h, :, DA_VDIM:] = jnp.ones((n_cache + seq_len, DA_VDIM), BF16)

        dl = dl_ref[...]
        lam = (jnp.exp(jnp.sum(dl[0:1] * dl[1:2], axis=-1, keepdims=True))
               - jnp.exp(jnp.sum(dl[2:3] * dl[3:4], axis=-1, keepdims=True)) + lam_init)
        tsub = min(tq, TQ_SUB)
        lane = lax.broadcasted_iota(jnp.int32, (tsub, LANES), 1)

        def softmax_times_v(qm, kk, vv):
            s = lax.dot_general(qm.astype(BF16), kk, (((1,), (1,)), ((), ())), preferred_element_type=F32)
            e = jnp.exp2(s - jnp.max(s, axis=-1, keepdims=True)).astype(BF16)
            ev = jnp.dot(e, vv, preferred_element_type=F32)
            return ev[:, :DA_VDIM] / ev[:, DA_VDIM:]

        for h in range(n_heads):
            cols = slice(h * LANES, (h + 1) * LANES)
            kk, vv = kk_scr[h], vv_scr[h]
            for c in range(tq // tsub):
                q = q_ref[c * tsub:(c + 1) * tsub, cols]
                if use_rope:
                    rows = pl.ds(pl.multiple_of(qb * tq, tq) + c * tsub, tsub)
                    q = _rope(q, cos_ref[rows, :], slo_ref[rows, :], shi_ref[rows, :])
                q = q * (DA_HEAD_DIM ** -0.5 * LOG2_E)
                o1 = softmax_times_v(jnp.where(lane < DA_HEAD_DIM, q, 0.0), kk, vv)
                o2 = softmax_times_v(jnp.where(lane >= DA_HEAD_DIM, q, 0.0), kk, vv)
                o_ref[c * tsub:(c + 1) * tsub, cols] = (
                    _rms(o1 - lam * o2, g_ref[...]) * (1.0 - lam_init)).astype(BF16)

    return kernel


def _diff_attention(proj, da_lambda_l, subln_g, layer, *, latent, cache=None, rope=None, cast_weights=()):
    lam_init = 0.8 - 0.6 * math.exp(-0.3 * layer)
    if latent:
        n_seq, seq_len, tq, n_cache, row0, n_heads = DEC_BATCH, DEC_SEQ, TQ_LAT, PAST_LEN, N_CTX, HEADS_LAT
    else:
        n_seq, seq_len, tq, n_cache, row0, n_heads = BATCH, SEQ, SEQ, 0, 0, HEADS_CTX
    nq = seq_len // tq
    lk = n_cache + seq_len
    width = n_heads * LANES
    hk, hv = OFF_K // width, OFF_V // width

    in_specs = [
        pl.BlockSpec((4, DA_HEAD_DIM), lambda b, h, t: (0, 0)),
        pl.BlockSpec((1, DA_VDIM), lambda b, h, t: (0, 0)),
        pl.BlockSpec((tq, width), lambda b, h, t: (row0 // tq + b * nq + t, h)),
        pl.BlockSpec((seq_len, width), lambda b, h, t: (row0 // seq_len + b, hk + h)),
        pl.BlockSpec((seq_len, width), lambda b, h, t: (row0 // seq_len + b, hv + h)),
    ]
    args = [da_lambda_l, subln_g, proj, proj, proj]
    blocks = [((tq, width), F32), ((seq_len, width), F32), ((seq_len, width), F32), ((tq, width), BF16)]
    if latent:
        cache_k, cache_v = cache
        cache_block = (n_cache * DA_HEADS, LANES)
        in_specs += [pl.BlockSpec((None, None) + cache_block, lambda b, h, t: (b, layer, 0, 0))] * 2
        args += [cache_k, cache_v]
        in_specs += [pl.BlockSpec((seq_len, LANES), lambda b, h, t: (0, 0))] * 3
        args += list(rope)
        blocks += [(cache_block, F32)] * 2 + [((seq_len, LANES), F32)] * 3

    n_hgroups = DA_HEADS // n_heads
    n_steps = n_seq * n_hgroups * nq
    out_specs = [pl.BlockSpec((tq, width), lambda b, h, t: (b * nq + t, h))]
    out_shape = [jax.ShapeDtypeStruct((n_seq * seq_len, DA_WIDTH), BF16)]
    def step(b, h, t):
        return (b * n_hgroups + h) * nq + t

    for w, w_slab, split in cast_weights:
        _, rows, cols = w.shape
        slab = rows // n_steps
        assert slab * n_steps == rows and slab % 16 == 0, (w.shape, n_steps)
        in_specs.append(pl.BlockSpec((None, slab, cols), lambda b, h, t, w_slab=w_slab: (w_slab, step(b, h, t), 0)))
        args.append(w)
        blocks.append(((slab, cols), F32))
        for width_out in ((cols,) if split is None else (split, cols - split)):
            out_specs.append(pl.BlockSpec((slab, width_out), lambda b, h, t: (step(b, h, t), 0)))
            out_shape.append(jax.ShapeDtypeStruct((rows, width_out), BF16))
            blocks.append(((slab, width_out), BF16))

    tsub = min(tq, TQ_SUB)
    scratch = [((n_heads, lk, LANES), BF16), ((n_heads, lk, 2 * DA_VDIM), BF16)]
    return pl.pallas_call(
        _make_attn_kernel(seq_len, tq, n_cache, n_heads, latent, lam_init,
                          tuple(split for _, _, split in cast_weights)),
        grid=(n_seq, n_hgroups, nq),
        in_specs=in_specs,
        out_specs=out_specs,
        out_shape=out_shape,
        scratch_shapes=[pltpu.VMEM(s, d) for s, d in scratch],
        compiler_params=pltpu.CompilerParams(
            dimension_semantics=("parallel", "parallel", "arbitrary"),
            vmem_limit_bytes=_vmem_limit(
                blocks, scratch=scratch,
                temps=[((tsub, lk), F32)] * 6 * 4 + [((tq, 2 * DA_VDIM), F32)] * 4)),
        name="diff_attention_latent" if latent else "diff_attention_context",
    )(*args)


def _fill_shifted(scr, value, pad, seq_len, shifts):
    width = value.shape[-1]
    n = seq_len + 2 * pad - SUBLANES
    scr[0, 0:pad, :] = jnp.zeros((pad, width), F32)
    scr[0, pad + seq_len:, :] = jnp.zeros((pad, width), F32)
    scr[0, pad:pad + seq_len, :] = value
    for s, shift in enumerate(shifts, start=1):
        scr[s, 0:n, :] = scr[0, shift:shift + n, :]


def _tap(scr, shifts, r0, offset):
    aligned, rem = offset - offset % SUBLANES, offset % SUBLANES
    slot = 0 if rem == 0 else 1 + shifts.index(rem)
    return scr[slot, pl.ds(r0 + aligned, CONV_ROWS), :]


def _make_conv_kernel(seq_len):
    pad3, pad31 = SUBLANES, 2 * SUBLANES
    shifts3 = (1, SUBLANES - 1)
    shifts31 = tuple(range(1, SUBLANES))

    def kernel(gb_ref, gc_ref, sx_ref, ca_ref, cb_ref, w3_ref, w31_ref, b31_ref, lng_ref, lnb_ref,
               bo_ref, co_ref, t_scr, u_scr, y_scr):
        _fill_shifted(t_scr, gc_ref[...] * sx_ref[...], pad3, seq_len, shifts3)
        _fill_shifted(u_scr, ca_ref[...] * _sigmoid(cb_ref[...]), pad31, seq_len, shifts31)

        def chunk(c, carry):
            r0 = pl.multiple_of(c * CONV_ROWS, CONV_ROWS)
            rows = pl.ds(r0, CONV_ROWS)
            y = _tap(t_scr, shifts3, r0, pad3 - SC_K // 2) * w3_ref[0:1, :]
            for k in range(1, SC_K):
                y = y + _tap(t_scr, shifts3, r0, pad3 - SC_K // 2 + k) * w3_ref[k:k + 1, :]
            bo_ref[rows, :] = (gb_ref[rows, :] * y).astype(BF16)

            acc = _tap(u_scr, shifts31, r0, pad31 - CF_K // 2) * w31_ref[0:1, :]
            for k in range(1, CF_K):
                acc = acc + _tap(u_scr, shifts31, r0, pad31 - CF_K // 2 + k) * w31_ref[k:k + 1, :]
            y_scr[rows, :] = acc + b31_ref[...]
            return carry

        lax.fori_loop(0, seq_len // CONV_ROWS, chunk, 0)

        y = y_scr[...]
        xc = y - jnp.mean(y, axis=-1, keepdims=True)
        yn = xc * lax.rsqrt(jnp.mean(xc * xc, axis=-1, keepdims=True) + EPS) * lng_ref[...] + lnb_ref[...]
        co_ref[...] = (yn * _sigmoid(yn)).astype(BF16)

    return kernel


def _conv_branches(proj, sc_conv_l, cf_conv_l, cf_b, ln_g, ln_b, *, latent):
    if latent:
        n_seq, seq_len, row0 = DEC_BATCH, DEC_SEQ, N_CTX
    else:
        n_seq, seq_len, row0 = BATCH, SEQ, 0
    rb = row0 // seq_len
    c0 = OFF_SC // SC_WIDTH

    def col(cidx):
        return pl.BlockSpec((seq_len, SC_WIDTH), lambda b: (rb + b, cidx))

    def whole(shape):
        return pl.BlockSpec(shape, lambda b: (0, 0))

    in_specs = [col(c0), col(c0 + 1), col(c0 + 2), col(c0 + 3), col(c0 + 4),
                whole((SC_K, SC_WIDTH)), whole((CF_K, CF_WIDTH)),
                whole((1, CF_WIDTH)), whole((1, CF_WIDTH)), whole((1, CF_WIDTH))]
    args = [proj, proj, proj, proj, proj, sc_conv_l, cf_conv_l, cf_b, ln_g, ln_b]
    out_spec = pl.BlockSpec((seq_len, SC_WIDTH), lambda b: (b, 0))
    out_sds = jax.ShapeDtypeStruct((n_seq * seq_len, SC_WIDTH), BF16)
    scratch = [((3, seq_len + 2 * SUBLANES, SC_WIDTH), F32),
               ((SUBLANES, seq_len + 4 * SUBLANES, CF_WIDTH), F32),
               ((seq_len, CF_WIDTH), F32)]
    return pl.pallas_call(
        _make_conv_kernel(seq_len),
        grid=(n_seq,),
        in_specs=in_specs,
        out_specs=[out_spec, out_spec],
        out_shape=[out_sds, out_sds],
        scratch_shapes=[pltpu.VMEM(s, d) for s, d in scratch],
        compiler_params=pltpu.CompilerParams(
            dimension_semantics=("parallel",),
            vmem_limit_bytes=_vmem_limit(
                [((seq_len, SC_WIDTH), F32)] * 5 + [((seq_len, SC_WIDTH), BF16)] * 2,
                scratch=scratch,
                temps=[((seq_len, SC_WIDTH), F32)] * 4)),
        name="conv_branches_latent" if latent else "conv_branches_context",
    )(*args)


def _merge_kernel(ac_ref, al_ref, bc_ref, bl_ref, cc_ref, cl_ref, ga_ref, gb_ref, gc_ref,
                  bga_ref, bgb_ref, bgc_ref, wa_ref, wb_ref, wc_ref, o_ref):
    def merge(a_ref, b_ref, c_ref):
        br_a = jnp.dot(a_ref[...], wa_ref[...], preferred_element_type=F32)
        br_b = jnp.dot(b_ref[...], wb_ref[...], preferred_element_type=F32)
        br_c = jnp.dot(c_ref[...], wc_ref[...], preferred_element_type=F32)
        merged = (_sigmoid(ga_ref[...].astype(F32) + bga_ref[...]) * br_a
                  + _sigmoid(gb_ref[...].astype(F32) + bgb_ref[...]) * br_b
                  + _sigmoid(gc_ref[...].astype(F32) + bgc_ref[...]) * br_c)
        o_ref[...] = merged.astype(BF16)

    is_ctx = pl.program_id(0) < N_CTX // TM_MERGE
    pl.when(is_ctx)(lambda: merge(ac_ref, bc_ref, cc_ref))
    pl.when(jnp.logical_not(is_ctx))(lambda: merge(al_ref, bl_ref, cl_ref))


def _gated_merge(attn_pair, b_pair, c_pair, gates, b_gate_l, w_da, w_sc, w_cf):
    tm = TM_MERGE

    def gate(k):
        return pl.BlockSpec((tm, D_MODEL), lambda i: (i, k))

    def gbias(k):
        return pl.BlockSpec((1, D_MODEL), lambda i: (0, k))

    def resident(rows):
        return pl.BlockSpec((rows, D_MODEL), lambda i: (0, 0))

    return pl.pallas_call(
        _merge_kernel,
        grid=(N_TOK // tm,),
        in_specs=[
            *_group_specs(tm, DA_WIDTH), *_group_specs(tm, SC_WIDTH), *_group_specs(tm, CF_WIDTH),
            gate(0), gate(1), gate(2), gbias(0), gbias(1), gbias(2),
            resident(DA_WIDTH), resident(SC_WIDTH), resident(CF_WIDTH),
        ],
        out_specs=pl.BlockSpec((tm, D_MODEL), lambda i: (i, 0)),
        out_shape=jax.ShapeDtypeStruct((N_TOK, D_MODEL), BF16),
        compiler_params=pltpu.CompilerParams(
            dimension_semantics=("arbitrary",),
            vmem_limit_bytes=_vmem_limit(
                [((tm, DA_WIDTH), BF16), ((tm, SC_WIDTH), BF16), ((tm, CF_WIDTH), BF16)] * 2
                + [((tm, D_MODEL), BF16)] * 4
                + [((DA_WIDTH, D_MODEL), BF16), ((SC_WIDTH, D_MODEL), BF16), ((CF_WIDTH, D_MODEL), BF16)],
                temps=[((tm, D_MODEL), F32)] * 8)),
        name="gated_merge",
    )(*attn_pair, *b_pair, *c_pair, gates, gates, gates, b_gate_l, b_gate_l, b_gate_l, w_da, w_sc, w_cf)


def _outproj_kernel(m_ref, w_ref, xc_ref, xl_ref, gate_ref, o_ref):
    x = _group_tile(xc_ref, xl_ref, TM_OUT)
    o_ref[...] = x + gate_ref[...] * jnp.dot(m_ref[...], w_ref[...], preferred_element_type=F32)


def _out_projection(merged, w_out, x_pair, mod4, layer):
    return pl.pallas_call(
        _outproj_kernel,
        grid=(N_TOK // TM_OUT,),
        in_specs=[
            pl.BlockSpec((TM_OUT, D_MODEL), lambda i: (i, 0)),
            pl.BlockSpec((D_MODEL, D_MODEL), lambda i: (0, 0)),
            *_group_specs(TM_OUT, D_MODEL),
            _mod_spec(layer, 2, TM_OUT),
        ],
        out_specs=pl.BlockSpec((TM_OUT, D_MODEL), lambda i: (i, 0)),
        out_shape=jax.ShapeDtypeStruct((N_TOK, D_MODEL), F32),
        compiler_params=pltpu.CompilerParams(
            dimension_semantics=("arbitrary",),
            vmem_limit_bytes=_vmem_limit(
                [((TM_OUT, D_MODEL), BF16), ((D_MODEL, D_MODEL), BF16)] + [((TM_OUT, D_MODEL), F32)] * 3,
                temps=[((TM_OUT, D_MODEL), F32)] * 2)),
        name="out_projection",
    )(merged, w_out, *x_pair, mod4)


def _make_ffn_kernel(final_norm):
    def kernel(*refs):
        x_ref, g_ref, shift_ref, scale_ref, gate_ref, wu_ref, ww_ref, wo_ref = refs[:8]
        refs = refs[8:]
        if final_norm:
            gf_ref = refs[0]
            refs = refs[1:]
        y_ctx_ref, y_lat_ref, h_scr, acc_scr = refs
        i, j = pl.program_id(0), pl.program_id(1)

        @pl.when(j == 0)
        def _():
            _mod_norm_rows(h_scr, (x_ref,), TM_FFN, g_ref[...], shift_ref[...], scale_ref[...])
            acc_scr[...] = jnp.zeros_like(acc_scr)

        h = h_scr[...]
        u = jnp.dot(h, wu_ref[...], preferred_element_type=F32)
        w = jnp.dot(h, ww_ref[...], preferred_element_type=F32)
        act = (u * _sigmoid(u) * w).astype(BF16)
        acc_scr[...] += jnp.dot(act, wo_ref[...], preferred_element_type=F32)

        def result():
            y = x_ref[...] + gate_ref[...] * acc_scr[...]
            return _rms(y, gf_ref[...]) if final_norm else y

        last = j == pl.num_programs(1) - 1

        @pl.when(last & (i < N_CTX // TM_FFN))
        def _():
            y_ctx_ref[...] = result()

        @pl.when(last & (i >= N_CTX // TM_FFN))
        def _():
            y_lat_ref[...] = result()

    return kernel


def _swiglu(x, g, mod4, layer, w_ffn_in, w_ffn_out, g_final=None):
    final_norm = g_final is not None
    nh = FF_HIDDEN // TH_FFN
    n_ctx_tiles = N_CTX // TM_FFN
    row = pl.BlockSpec((1, D_MODEL), lambda i, j: (0, 0))
    in_specs = [
        pl.BlockSpec((TM_FFN, D_MODEL), lambda i, j: (i, 0)),
        row,
        _mod_spec(layer, 3, TM_FFN), _mod_spec(layer, 4, TM_FFN), _mod_spec(layer, 5, TM_FFN),
        pl.BlockSpec((D_MODEL, TH_FFN), lambda i, j: (0, j)),
        pl.BlockSpec((D_MODEL, TH_FFN), lambda i, j: (0, nh + j)),
        pl.BlockSpec((TH_FFN, D_MODEL), lambda i, j: (j, 0)),
    ]
    args = [x, g, mod4, mod4, mod4, w_ffn_in, w_ffn_in, w_ffn_out]
    if final_norm:
        in_specs.append(row)
        args.append(g_final)
    out_specs = [pl.BlockSpec((TM_FFN, D_MODEL), lambda i, j: (jnp.minimum(i, n_ctx_tiles - 1), 0)),
                 pl.BlockSpec((TM_FFN, D_MODEL), lambda i, j: (jnp.maximum(i - n_ctx_tiles, 0), 0))]
    out_shape = [jax.ShapeDtypeStruct((N_CTX, D_MODEL), F32), jax.ShapeDtypeStruct((N_LAT, D_MODEL), F32)]
    scratch = [((TM_FFN, D_MODEL), BF16), ((TM_FFN, D_MODEL), F32)]
    return pl.pallas_call(
        _make_ffn_kernel(final_norm),
        grid=(N_TOK // TM_FFN, nh),
        in_specs=in_specs,
        out_specs=out_specs,
        out_shape=out_shape,
        scratch_shapes=[pltpu.VMEM(s, d) for s, d in scratch],
        compiler_params=pltpu.CompilerParams(
            dimension_semantics=("arbitrary", "arbitrary"),
            vmem_limit_bytes=_vmem_limit(
                [((TM_FFN, D_MODEL), F32)] * 3 + [((D_MODEL, TH_FFN), BF16)] * 2
                + [((TH_FFN, D_MODEL), BF16)],
                scratch=scratch,
                temps=[((TM_FFN, TH_FFN), F32)] * 4 + [((TM_FFN, D_MODEL), F32)] * 2)),
        name="swiglu_final" if final_norm else "swiglu",
    )(*args)


def _rope_tables():
    rows = DEC_SEQ // GRID_W
    row_ids = jnp.repeat(jnp.arange(rows), GRID_W).astype(F32)
    col_ids = jnp.tile(jnp.arange(GRID_W), rows).astype(F32)
    n_freq = DA_HEAD_DIM // 4
    inv = ROPE_THETA ** (-jnp.arange(n_freq, dtype=F32) / n_freq)
    ang_r, ang_c = row_ids[:, None] * inv, col_ids[:, None] * inv
    zero = jnp.zeros_like(ang_r)
    cos = jnp.concatenate([jnp.cos(ang_r)] * 2 + [jnp.cos(ang_c)] * 2, axis=-1)
    sin_lo = jnp.concatenate([-jnp.sin(ang_r), zero, -jnp.sin(ang_c), zero], axis=-1)
    sin_hi = jnp.concatenate([zero, jnp.sin(ang_r), zero, jnp.sin(ang_c)], axis=-1)
    return tuple(jnp.tile(t, (1, LANES // DA_HEAD_DIM)) for t in (cos, sin_lo, sin_hi))


def kernel(x_prompt, x_sample, cache_k, cache_v, c, c_ctx, w_mod, b_mod, g_norm1, w_in, da_lambda, da_subln,
           w_da_out, sc_conv, w_sc_out, cf_conv, cf_conv_b, cf_ln_g, cf_ln_b, w_cf_out, b_gate, w_out,
           g_norm2, w_ffn_in, w_ffn_out, g_final):
    x_pair = (x_prompt.reshape(N_CTX, D_MODEL), x_sample.reshape(N_LAT, D_MODEL))
    cvec =jnp.concatenate([c_ctx[None, :], c, jnp.zeros((MOD_ROWS - 1 - DEC_BATCH, D_MODEL), F32)], axis=0)
    mod4 = _modulation(cvec, w_mod, b_mod).reshape(DEPTH, MOD_ROWS, 1, 6 * D_MODEL)
    rope = _rope_tables()
    cache_k = cache_k.reshape(DEC_BATCH, DEPTH, PAST_LEN * DA_HEADS, 2 * DA_HEAD_DIM)
    cache_v = cache_v.reshape(DEC_BATCH, DEPTH, PAST_LEN * DA_HEADS, DA_VDIM)

    new_kv = tuple(jnp.zeros((BATCH, DEPTH, SEQ, DA_WIDTH), F32) for _ in range(2))
    w_mix_bf, w_gate_bf = w_in[0, :, :MIX_COLS].astype(BF16), w_in[0, :, MIX_COLS:].astype(BF16)
    for l in range(DEPTH):
        h = _pre_norm(x_pair, g_norm1[l][None, :], mod4, l)
        proj, gates, new_kv = _in_projection(h, l, w_mix_bf, w_gate_bf, new_kv)

        subln = da_subln[l][None, :]
        attn_ctx, w_out_bf, w_da_bf, w_sc_bf, w_cf_bf = _diff_attention(
            proj, da_lambda[l], subln, l, latent=False,
            cast_weights=[(w, l, None) for w in (w_out, w_da_out, w_sc_out, w_cf_out)])
        casts = [(w_ffn_in, l, None), (w_ffn_out, l, None)]
        if l + 1 < DEPTH:
            casts.append((w_in, l + 1, MIX_COLS))
        attn_lat, w_ffn_in_bf, w_ffn_out_bf, *next_w_in = _diff_attention(
            proj, da_lambda[l], subln, l, latent=True, cache=(cache_k, cache_v), rope=rope,
            cast_weights=casts)
        if next_w_in:
            w_mix_bf, w_gate_bf = next_w_in
        attn_pair = (attn_ctx, attn_lat)
        conv_args = (sc_conv[l], cf_conv[l], cf_conv_b[l][None, :], cf_ln_g[l][None, :], cf_ln_b[l][None, :])
        b_ctx, c_ctx_pre = _conv_branches(proj, *conv_args, latent=False)
        b_lat, c_lat = _conv_branches(proj, *conv_args, latent=True)

        merged = _gated_merge(attn_pair, (b_ctx, b_lat), (c_ctx_pre, c_lat), gates, b_gate[l][None, :],
                              w_da_bf, w_sc_bf, w_cf_bf)
        x_mid = _out_projection(merged, w_out_bf, x_pair, mod4, l)
        x_pair = _swiglu(x_mid, g_norm2[l][None, :], mod4, l, w_ffn_in_bf, w_ffn_out_bf,
                         g_final[None, :] if l == DEPTH - 1 else None)

    y_prompt, y_sample = x_pair
    new_k, new_v = new_kv
    return (y_prompt.reshape(BATCH, SEQ, D_MODEL), y_sample.reshape(DEC_BATCH, DEC_SEQ, D_MODEL),
            new_k.reshape(BATCH, DEPTH, SEQ, DA_HEADS, 2 * DA_HEAD_DIM),
            new_v.reshape(BATCH, DEPTH, SEQ, DA_HEADS, DA_VDIM))
```

```python
import math

import jax
import jax.numpy as jnp
from jax import lax
from jax.experimental import pallas as pl
from jax.experimental.pallas import tpu as pltpu

D_MODEL = 2048
BATCH = 16
SEQ = 256
DEPTH = 2
DEC_BATCH = 8
DEC_SEQ = 1024
PAST_LEN = 256
GRID_W = 64
DA_HEADS = 8
DA_HEAD_DIM = 64
DA_VDIM = 2 * DA_HEAD_DIM
DA_WIDTH = DA_HEADS * 2 * DA_HEAD_DIM
SC_WIDTH = 512
SC_K = 3
CF_WIDTH = 512
CF_K = 31
N_BRANCH = 3
FF_HIDDEN = -(-8 * D_MODEL // (3 * 256)) * 256
ROPE_THETA = 10000.0
EPS = 1e-6

OFF_K = DA_WIDTH
OFF_V = 2 * DA_WIDTH
OFF_SC = 3 * DA_WIDTH
OFF_CF = OFF_SC + 3 * SC_WIDTH
OFF_GATE = OFF_CF + 2 * CF_WIDTH
IN_COLS = OFF_GATE + N_BRANCH * D_MODEL

N_CTX = BATCH * SEQ
N_LAT = DEC_BATCH * DEC_SEQ
N_TOK = N_CTX + N_LAT
MOD_ROWS = 16

F32 = jnp.float32
BF16 = jnp.bfloat16

VMEM_BUDGET_V7X = 56 * 1024 * 1024
LANES = 128
SUBLANES = 8

TM_NORM = 512
TM_IN = 2048
TN = 512
TM_MERGE = 512
TM_OUT = 512
TM_FFN = 512
TH_FFN = 512
TQ_LAT = 1024
HEADS_LAT = 2
HEADS_CTX = 8
TQ_SUB = 256
LOG2_E = math.log2(math.e)
NORM_ROWS = 128
CONV_ROWS = 32


def _nbytes(shape, dtype):
    return math.prod(shape) * jnp.dtype(dtype).itemsize


def _vmem_limit(blocks, scratch=(), temps=()):
    total = 2 * sum(_nbytes(s, d) for s, d in blocks)
    total += sum(_nbytes(s, d) for s, d in scratch)
    total += sum(_nbytes(s, d) for s, d in temps)
    return min(total, VMEM_BUDGET_V7X)


def _mod_group(i, tm):
    return jnp.maximum(i * tm - N_CTX + DEC_SEQ, 0) // DEC_SEQ


def _rms(x, g):
    return x * lax.rsqrt(jnp.mean(x * x, axis=-1, keepdims=True) + EPS) * g


def _mod_norm_rows(o_ref, x_refs, n_rows, g, shift, scale):
    gain = g * (1.0 + scale)
    is_ctx = pl.program_id(0) < N_CTX // n_rows

    def load(rows):
        if len(x_refs) == 2:
            return jnp.where(is_ctx, x_refs[0][rows, :], x_refs[1][rows, :])
        return x_refs[0][rows, :]

    for c in range(n_rows // NORM_ROWS):
        rows = slice(c * NORM_ROWS, (c + 1) * NORM_ROWS)
        x = load(rows)
        r = lax.rsqrt(jnp.mean(x * x, axis=-1, keepdims=True) + EPS)
        o_ref[rows, :] = ((load(rows) * r) * gain + shift).astype(BF16)


def _sigmoid(x):
    return 0.5 * jnp.tanh(0.5 * x) + 0.5


def _mod_kernel(c_ref, w_ref, b_ref, o_ref):
    c = c_ref[...]
    s = (c * _sigmoid(c)).astype(BF16)
    o_ref[...] = jnp.dot(s, w_ref[...].astype(BF16), preferred_element_type=F32) + b_ref[...]


def _modulation(cvec, w_mod, b_mod):
    bn = 1024
    return pl.pallas_call(
        _mod_kernel,
        grid=(DEPTH, 6 * D_MODEL // bn),
        in_specs=[
            pl.BlockSpec((MOD_ROWS, D_MODEL), lambda l, j: (0, 0)),
            pl.BlockSpec((None, D_MODEL, bn), lambda l, j: (l, 0, j)),
            pl.BlockSpec((None, 1, bn), lambda l, j: (l, 0, j)),
        ],
        out_specs=pl.BlockSpec((None, MOD_ROWS, bn), lambda l, j: (l, 0, j)),
        out_shape=jax.ShapeDtypeStruct((DEPTH, MOD_ROWS, 6 * D_MODEL), F32),
        compiler_params=pltpu.CompilerParams(
            dimension_semantics=("parallel", "parallel"),
            vmem_limit_bytes=_vmem_limit(
                [((D_MODEL, bn), F32), ((MOD_ROWS, D_MODEL), F32), ((MOD_ROWS, bn), F32)],
                temps=[((D_MODEL, bn), BF16), ((D_MODEL, bn), F32)])),
        name="modulation",
    )(cvec, w_mod, b_mod.reshape(DEPTH, 1, 6 * D_MODEL))


def _mod_spec(layer, chunk, tm):
    return pl.BlockSpec((None, None, 1, D_MODEL),
                        lambda i, *_: (layer, _mod_group(i, tm), 0, chunk))


def _group_specs(tm, tn, n_col_tiles=None):
    nct = N_CTX // tm
    if n_col_tiles is None:
        return (pl.BlockSpec((tm, tn), lambda i, *_: (jnp.minimum(i, nct - 1), 0)),
                pl.BlockSpec((tm, tn), lambda i, *_: (jnp.maximum(i - nct, 0), 0)))
    last = n_col_tiles - 1
    return (pl.BlockSpec((tm, tn), lambda i, j: (jnp.minimum(i, nct - 1), jnp.where(i < nct, j, last))),
            pl.BlockSpec((tm, tn), lambda i, j: (jnp.maximum(i - nct, 0), jnp.where(i < nct, 0, j))))


def _group_tile(xc_ref, xl_ref, tm):
    return jnp.where(pl.program_id(0) < N_CTX // tm, xc_ref[...], xl_ref[...])


def _prenorm_kernel(xc_ref, xl_ref, g_ref, shift_ref, scale_ref, o_ref):
    _mod_norm_rows(o_ref, (xc_ref, xl_ref), TM_NORM, g_ref[...], shift_ref[...], scale_ref[...])


def _pre_norm(x_pair, g, mod4, layer):
    return pl.pallas_call(
        _prenorm_kernel,
        grid=(N_TOK // TM_NORM,),
        in_specs=[
            *_group_specs(TM_NORM, D_MODEL),
            pl.BlockSpec((1, D_MODEL), lambda i: (0, 0)),
            _mod_spec(layer, 0, TM_NORM),
            _mod_spec(layer, 1, TM_NORM),
        ],
        out_specs=pl.BlockSpec((TM_NORM, D_MODEL), lambda i: (i, 0)),
        out_shape=jax.ShapeDtypeStruct((N_TOK, D_MODEL), BF16),
        compiler_params=pltpu.CompilerParams(
            dimension_semantics=("arbitrary",),
            vmem_limit_bytes=_vmem_limit(
                [((TM_NORM, D_MODEL), F32)] * 2 + [((TM_NORM, D_MODEL), BF16)],
                temps=[((TM_NORM, D_MODEL), F32)] * 3)),
        name="pre_norm",
    )(*x_pair, g, mod4, mod4)


N_CTX_TILES = N_CTX // TM_IN
KV_TILES = DA_WIDTH // TN
MIX_COLS = OFF_GATE
GATE_COLS = N_BRANCH * D_MODEL


def _cast_weight_tile(w_ref, wbf_scr):
    @pl.when(pl.program_id(1) == 0)
    def _():
        wbf_scr[...] = w_ref[...].astype(BF16)


def _inproj_mix_kernel(h_ref, w_ref, nk_hbm, nv_hbm, o_ref, nk_ref, nv_ref, wbf_scr):
    del nk_hbm, nv_hbm
    j, i = pl.program_id(0), pl.program_id(1)
    _cast_weight_tile(w_ref, wbf_scr)
    o_ref[...] = jnp.dot(h_ref[...], wbf_scr[...], preferred_element_type=F32)

    @pl.when((i < N_CTX_TILES) & (j >= OFF_K // TN) & (j < OFF_V // TN))
    def _():
        nk_ref[...] = o_ref[...].reshape(nk_ref.shape)

    @pl.when((i < N_CTX_TILES) & (j >= OFF_V // TN) & (j < OFF_SC // TN))
    def _():
        nv_ref[...] = o_ref[...].reshape(nv_ref.shape)


def _inproj_gate_kernel(h_ref, wa_ref, wb_ref, o_ref, wa_scr, wb_scr):
    _cast_weight_tile(wa_ref, wa_scr)
    _cast_weight_tile(wb_ref, wb_scr)
    h = h_ref[...]
    o_ref[:, :TN] = jnp.dot(h, wa_scr[...], preferred_element_type=F32).astype(BF16)
    o_ref[:, TN:] = jnp.dot(h, wb_scr[...], preferred_element_type=F32).astype(BF16)


def _kv_cache_spec(layer, col0):
    def index(j, i):
        rel = j - col0
        row = jnp.where(rel < 0, 0,
                        jnp.where(rel >= KV_TILES, N_CTX_TILES - 1, jnp.minimum(i, N_CTX_TILES - 1)))
        return (row, layer, 0, jnp.clip(rel, 0, KV_TILES - 1))
    return pl.BlockSpec((TM_IN // SEQ, None, SEQ, TN), index)


def _in_projection(h, layer, w_in, new_kv):
    h_spec = pl.BlockSpec((TM_IN, D_MODEL), lambda j, i: (i, 0))
    kv_sds = jax.ShapeDtypeStruct((BATCH, DEPTH, SEQ, DA_WIDTH), F32)
    kv_block = ((TM_IN // SEQ, SEQ, TN), F32)
    hbm = pl.BlockSpec(memory_space=pl.ANY)
    mix, new_k, new_v = pl.pallas_call(
        _inproj_mix_kernel,
        grid=(MIX_COLS // TN, N_TOK // TM_IN),
        in_specs=[h_spec, pl.BlockSpec((None, D_MODEL, TN), lambda j, i: (layer, 0, j)), hbm, hbm],
        out_specs=[pl.BlockSpec((TM_IN, TN), lambda j, i: (i, j)),
                   _kv_cache_spec(layer, OFF_K // TN), _kv_cache_spec(layer, OFF_V // TN)],
        out_shape=[jax.ShapeDtypeStruct((N_TOK, MIX_COLS), F32), kv_sds, kv_sds],
        scratch_shapes=[pltpu.VMEM((D_MODEL, TN), BF16)],
        input_output_aliases={2: 1, 3: 2},
        compiler_params=pltpu.CompilerParams(
            dimension_semantics=("arbitrary", "arbitrary"),
            vmem_limit_bytes=_vmem_limit(
                [((TM_IN, D_MODEL), BF16), ((D_MODEL, TN), F32), ((TM_IN, TN), F32), kv_block, kv_block],
                scratch=[((D_MODEL, TN), BF16)], temps=[((TM_IN, TN), F32)])),
        name="in_projection_mix",
    )(h, w_in, *new_kv)
    gates = pl.pallas_call(
        _inproj_gate_kernel,
        grid=(GATE_COLS // (2 * TN), N_TOK // TM_IN),
        in_specs=[h_spec,
                  pl.BlockSpec((None, D_MODEL, TN), lambda j, i: (layer, 0, MIX_COLS // TN + 2 * j)),
                  pl.BlockSpec((None, D_MODEL, TN), lambda j, i: (layer, 0, MIX_COLS // TN + 2 * j + 1))],
        out_specs=pl.BlockSpec((TM_IN, 2 * TN), lambda j, i: (i, j)),
        out_shape=jax.ShapeDtypeStruct((N_TOK, GATE_COLS), BF16),
        scratch_shapes=[pltpu.VMEM((D_MODEL, TN), BF16)] * 2,
        compiler_params=pltpu.CompilerParams(
            dimension_semantics=("arbitrary", "arbitrary"),
            vmem_limit_bytes=_vmem_limit(
                [((TM_IN, D_MODEL), BF16), ((D_MODEL, 2 * TN), F32), ((TM_IN, 2 * TN), BF16)],
                scratch=[((D_MODEL, 2 * TN), BF16)],
                temps=[((TM_IN, 2 * TN), F32), ((TM_IN, 2 * TN), BF16)])),
        name="in_projection_gates",
    )(h, w_in, w_in)
    return mix, gates, (new_k, new_v)


def _rope(x, cos, sin_lo, sin_hi):
    return (x * cos + pltpu.roll(x, LANES - 16, 1) * sin_lo + pltpu.roll(x, 16, 1) * sin_hi)


def _make_attn_kernel(seq_len, tq, n_cache, n_heads, use_rope, lam_init, n_cast):
    def kernel(*refs):
        dl_ref, g_ref, q_ref, k_ref, v_ref = refs[:5]
        refs = refs[5:]
        if n_cache:
            ck_ref, cv_ref = refs[:2]
            refs = refs[2:]
        if use_rope:
            cos_ref, slo_ref, shi_ref = refs[:3]
            refs = refs[3:]
        cast_in, refs = refs[:n_cast], refs[n_cast:]
        o_ref, refs = refs[0], refs[1:]
        cast_out, refs = refs[:n_cast], refs[n_cast:]
        kk_scr, vv_scr = refs
        qb = pl.program_id(2)
        for src, dst in zip(cast_in, cast_out):
            dst[...] = src[...].astype(BF16)

        @pl.when(qb == 0)
        def _():
            for h in range(n_heads):
                cols = slice(h * LANES, (h + 1) * LANES)
                k = k_ref[:, cols]
                if use_rope:
                    k = _rope(k, cos_ref[...], slo_ref[...], shi_ref[...])
                if n_cache:
                    head = pl.program_id(1) * n_heads + h
                    rows = pl.ds(head, n_cache, stride=DA_HEADS)
                    kk_scr[h, 0:n_cache, :] = ck_ref[rows, :].astype(BF16)
                    vv_scr[h, 0:n_cache, 0:DA_VDIM] = cv_ref[rows, :].astype(BF16)
                kk_scr[h, n_cache:n_cache + seq_len, :] = k.astype(BF16)
                vv_scr[h, n_cache:n_cache + seq_len, 0:DA_VDIM] = v_ref[:, cols].astype(BF16)
                vv_scr[h, :, DA_VDIM:] = jnp.ones((n_cache + seq_len, DA_VDIM), BF16)

        dl = dl_ref[...]
        lam = (jnp.exp(jnp.sum(dl[0:1] * dl[1:2], axis=-1, keepdims=True))
               - jnp.exp(jnp.sum(dl[2:3] * dl[3:4], axis=-1, keepdims=True)) + lam_init)
        tsub = min(tq, TQ_SUB)
        lane = lax.broadcasted_iota(jnp.int32, (tsub, LANES), 1)

        def softmax_times_v(qm, kk, vv):
            s = lax.dot_general(qm.astype(BF16), kk, (((1,), (1,)), ((), ())), preferred_element_type=F32)
            e = jnp.exp2(s - jnp.max(s, axis=-1, keepdims=True)).astype(BF16)
            ev = jnp.dot(e, vv, preferred_element_type=F32)
            return ev[:, :DA_VDIM] / ev[:, DA_VDIM:]

        for h in range(n_heads):
            cols = slice(h * LANES, (h + 1) * LANES)
            kk, vv = kk_scr[h], vv_scr[h]
            for c in range(tq // tsub):
                q = q_ref[c * tsub:(c + 1) * tsub, cols]
                if use_rope:
                    rows = pl.ds(pl.multiple_of(qb * tq, tq) + c * tsub, tsub)
                    q = _rope(q, cos_ref[rows, :], slo_ref[rows, :], shi_ref[rows, :])
                q = q * (DA_HEAD_DIM ** -0.5 * LOG2_E)
                o1 = softmax_times_v(jnp.where(lane < DA_HEAD_DIM, q, 0.0), kk, vv)
                o2 = softmax_times_v(jnp.where(lane >= DA_HEAD_DIM, q, 0.0), kk, vv)
                o_ref[c * tsub:(c + 1) * tsub, cols] = (
                    _rms(o1 - lam * o2, g_ref[...]) * (1.0 - lam_init)).astype(BF16)

    return kernel


def _diff_attention(proj, da_lambda_l, subln_g, layer, *, latent, cache=None, rope=None, cast_weights=()):
    lam_init = 0.8 - 0.6 * math.exp(-0.3 * layer)
    if latent:
        n_seq, seq_len, tq, n_cache, row0, n_heads = DEC_BATCH, DEC_SEQ, TQ_LAT, PAST_LEN, N_CTX, HEADS_LAT
    else:
        n_seq, seq_len, tq, n_cache, row0, n_heads = BATCH, SEQ, SEQ, 0, 0, HEADS_CTX
    nq = seq_len // tq
    lk = n_cache + seq_len
    width = n_heads * LANES
    hk, hv = OFF_K // width, OFF_V // width

    in_specs = [
        pl.BlockSpec((4, DA_HEAD_DIM), lambda b, h, t: (0, 0)),
        pl.BlockSpec((1, DA_VDIM), lambda b, h, t: (0, 0)),
        pl.BlockSpec((tq, width), lambda b, h, t: (row0 // tq + b * nq + t, h)),
        pl.BlockSpec((seq_len, width), lambda b, h, t: (row0 // seq_len + b, hk + h)),
        pl.BlockSpec((seq_len, width), lambda b, h, t: (row0 // seq_len + b, hv + h)),
    ]
    args = [da_lambda_l, subln_g, proj, proj, proj]
    blocks = [((tq, width), F32), ((seq_len, width), F32), ((seq_len, width), F32), ((tq, width), BF16)]
    if latent:
        cache_k, cache_v = cache
        cache_block = (n_cache * DA_HEADS, LANES)
        in_specs += [pl.BlockSpec((None, None) + cache_block, lambda b, h, t: (b, layer, 0, 0))] * 2
        args += [cache_k, cache_v]
        in_specs += [pl.BlockSpec((seq_len, LANES), lambda b, h, t: (0, 0))] * 3
        args += list(rope)
        blocks += [(cache_block, F32)] * 2 + [((seq_len, LANES), F32)] * 3

    n_hgroups = DA_HEADS // n_heads
    n_steps = n_seq * n_hgroups * nq
    out_specs = [pl.BlockSpec((tq, width), lambda b, h, t: (b * nq + t, h))]
    out_shape = [jax.ShapeDtypeStruct((n_seq * seq_len, DA_WIDTH), BF16)]
    def step(b, h, t):
        return (b * n_hgroups + h) * nq + t

    for w, w_slab in cast_weights:
        _, rows, cols = w.shape
        slab = rows // n_steps
        assert slab * n_steps == rows and slab % 16 == 0, (w.shape, n_steps)
        in_specs.append(pl.BlockSpec((None, slab, cols), lambda b, h, t, w_slab=w_slab: (w_slab, step(b, h, t), 0)))
        args.append(w)
        out_specs.append(pl.BlockSpec((slab, cols), lambda b, h, t: (step(b, h, t), 0)))
        out_shape.append(jax.ShapeDtypeStruct((rows, cols), BF16))
        blocks += [((slab, cols), F32), ((slab, cols), BF16)]

    tsub = min(tq, TQ_SUB)
    scratch = [((n_heads, lk, LANES), BF16), ((n_heads, lk, 2 * DA_VDIM), BF16)]
    return pl.pallas_call(
        _make_attn_kernel(seq_len, tq, n_cache, n_heads, latent, lam_init, len(cast_weights)),
        grid=(n_seq, n_hgroups, nq),
        in_specs=in_specs,
        out_specs=out_specs,
        out_shape=out_shape,
        scratch_shapes=[pltpu.VMEM(s, d) for s, d in scratch],
        compiler_params=pltpu.CompilerParams(
            dimension_semantics=("parallel", "parallel", "arbitrary"),
            vmem_limit_bytes=_vmem_limit(
                blocks, scratch=scratch,
                temps=[((tsub, lk), F32)] * 6 * 4 + [((tq, 2 * DA_VDIM), F32)] * 4)),
        name="diff_attention_latent" if latent else "diff_attention_context",
    )(*args)


def _fill_shifted(scr, value, pad, seq_len, shifts):
    width = value.shape[-1]
    n = seq_len + 2 * pad - SUBLANES
    scr[0, 0:pad, :] = jnp.zeros((pad, width), F32)
    scr[0, pad + seq_len:, :] = jnp.zeros((pad, width), F32)
    scr[0, pad:pad + seq_len, :] = value
    for s, shift in enumerate(shifts, start=1):
        scr[s, 0:n, :] = scr[0, shift:shift + n, :]


def _tap(scr, shifts, r0, offset):
    aligned, rem = offset - offset % SUBLANES, offset % SUBLANES
    slot = 0 if rem == 0 else 1 + shifts.index(rem)
    return scr[slot, pl.ds(r0 + aligned, CONV_ROWS), :]


def _make_conv_kernel(seq_len):
    pad3, pad31 = SUBLANES, 2 * SUBLANES
    shifts3 = (1, SUBLANES - 1)
    shifts31 = tuple(range(1, SUBLANES))

    def kernel(gb_ref, gc_ref, sx_ref, ca_ref, cb_ref, w3_ref, w31_ref, b31_ref, lng_ref, lnb_ref,
               bo_ref, co_ref, t_scr, u_scr, y_scr):
        _fill_shifted(t_scr, gc_ref[...] * sx_ref[...], pad3, seq_len, shifts3)
        _fill_shifted(u_scr, ca_ref[...] * _sigmoid(cb_ref[...]), pad31, seq_len, shifts31)

        def chunk(c, carry):
            r0 = pl.multiple_of(c * CONV_ROWS, CONV_ROWS)
            rows = pl.ds(r0, CONV_ROWS)
            y = _tap(t_scr, shifts3, r0, pad3 - SC_K // 2) * w3_ref[0:1, :]
            for k in range(1, SC_K):
                y = y + _tap(t_scr, shifts3, r0, pad3 - SC_K // 2 + k) * w3_ref[k:k + 1, :]
            bo_ref[rows, :] = (gb_ref[rows, :] * y).astype(BF16)

            acc = _tap(u_scr, shifts31, r0, pad31 - CF_K // 2) * w31_ref[0:1, :]
            for k in range(1, CF_K):
                acc = acc + _tap(u_scr, shifts31, r0, pad31 - CF_K // 2 + k) * w31_ref[k:k + 1, :]
            y_scr[rows, :] = acc + b31_ref[...]
            return carry

        lax.fori_loop(0, seq_len // CONV_ROWS, chunk, 0)

        y = y_scr[...]
        xc = y - jnp.mean(y, axis=-1, keepdims=True)
        yn = xc * lax.rsqrt(jnp.mean(xc * xc, axis=-1, keepdims=True) + EPS) * lng_ref[...] + lnb_ref[...]
        co_ref[...] = (yn * _sigmoid(yn)).astype(BF16)

    return kernel


def _conv_branches(proj, sc_conv_l, cf_conv_l, cf_b, ln_g, ln_b, *, latent):
    if latent:
        n_seq, seq_len, row0 = DEC_BATCH, DEC_SEQ, N_CTX
    else:
        n_seq, seq_len, row0 = BATCH, SEQ, 0
    rb = row0 // seq_len
    c0 = OFF_SC // SC_WIDTH

    def col(cidx):
        return pl.BlockSpec((seq_len, SC_WIDTH), lambda b: (rb + b, cidx))

    def whole(shape):
        return pl.BlockSpec(shape, lambda b: (0, 0))

    in_specs = [col(c0), col(c0 + 1), col(c0 + 2), col(c0 + 3), col(c0 + 4),
                whole((SC_K, SC_WIDTH)), whole((CF_K, CF_WIDTH)),
                whole((1, CF_WIDTH)), whole((1, CF_WIDTH)), whole((1, CF_WIDTH))]
    args = [proj, proj, proj, proj, proj, sc_conv_l, cf_conv_l, cf_b, ln_g, ln_b]
    out_spec = pl.BlockSpec((seq_len, SC_WIDTH), lambda b: (b, 0))
    out_sds = jax.ShapeDtypeStruct((n_seq * seq_len, SC_WIDTH), BF16)
    scratch = [((3, seq_len + 2 * SUBLANES, SC_WIDTH), F32),
               ((SUBLANES, seq_len + 4 * SUBLANES, CF_WIDTH), F32),
               ((seq_len, CF_WIDTH), F32)]
    return pl.pallas_call(
        _make_conv_kernel(seq_len),
        grid=(n_seq,),
        in_specs=in_specs,
        out_specs=[out_spec, out_spec],
        out_shape=[out_sds, out_sds],
        scratch_shapes=[pltpu.VMEM(s, d) for s, d in scratch],
        compiler_params=pltpu.CompilerParams(
            dimension_semantics=("parallel",),
            vmem_limit_bytes=_vmem_limit(
                [((seq_len, SC_WIDTH), F32)] * 5 + [((seq_len, SC_WIDTH), BF16)] * 2,
                scratch=scratch,
                temps=[((seq_len, SC_WIDTH), F32)] * 4)),
        name="conv_branches_latent" if latent else "conv_branches_context",
    )(*args)


def _merge_kernel(ac_ref, al_ref, bc_ref, bl_ref, cc_ref, cl_ref, ga_ref, gb_ref, gc_ref,
                  bga_ref, bgb_ref, bgc_ref, wa_ref, wb_ref, wc_ref, o_ref):
    def merge(a_ref, b_ref, c_ref):
        br_a = jnp.dot(a_ref[...], wa_ref[...], preferred_element_type=F32)
        br_b = jnp.dot(b_ref[...], wb_ref[...], preferred_element_type=F32)
        br_c = jnp.dot(c_ref[...], wc_ref[...], preferred_element_type=F32)
        merged = (_sigmoid(ga_ref[...].astype(F32) + bga_ref[...]) * br_a
                  + _sigmoid(gb_ref[...].astype(F32) + bgb_ref[...]) * br_b
                  + _sigmoid(gc_ref[...].astype(F32) + bgc_ref[...]) * br_c)
        o_ref[...] = merged.astype(BF16)

    is_ctx = pl.program_id(0) < N_CTX // TM_MERGE
    pl.when(is_ctx)(lambda: merge(ac_ref, bc_ref, cc_ref))
    pl.when(jnp.logical_not(is_ctx))(lambda: merge(al_ref, bl_ref, cl_ref))


def _gated_merge(attn_pair, b_pair, c_pair, gates, b_gate_l, w_da, w_sc, w_cf):
    tm = TM_MERGE

    def gate(k):
        return pl.BlockSpec((tm, D_MODEL), lambda i: (i, k))

    def gbias(k):
        return pl.BlockSpec((1, D_MODEL), lambda i: (0, k))

    def resident(rows):
        return pl.BlockSpec((rows, D_MODEL), lambda i: (0, 0))

    return pl.pallas_call(
        _merge_kernel,
        grid=(N_TOK // tm,),
        in_specs=[
            *_group_specs(tm, DA_WIDTH), *_group_specs(tm, SC_WIDTH), *_group_specs(tm, CF_WIDTH),
            gate(0), gate(1), gate(2), gbias(0), gbias(1), gbias(2),
            resident(DA_WIDTH), resident(SC_WIDTH), resident(CF_WIDTH),
        ],
        out_specs=pl.BlockSpec((tm, D_MODEL), lambda i: (i, 0)),
        out_shape=jax.ShapeDtypeStruct((N_TOK, D_MODEL), BF16),
        compiler_params=pltpu.CompilerParams(
            dimension_semantics=("arbitrary",),
            vmem_limit_bytes=_vmem_limit(
                [((tm, DA_WIDTH), BF16), ((tm, SC_WIDTH), BF16), ((tm, CF_WIDTH), BF16)] * 2
                + [((tm, D_MODEL), BF16)] * 4
                + [((DA_WIDTH, D_MODEL), BF16), ((SC_WIDTH, D_MODEL), BF16), ((CF_WIDTH, D_MODEL), BF16)],
                temps=[((tm, D_MODEL), F32)] * 8)),
        name="gated_merge",
    )(*attn_pair, *b_pair, *c_pair, gates, gates, gates, b_gate_l, b_gate_l, b_gate_l, w_da, w_sc, w_cf)


def _outproj_kernel(m_ref, w_ref, xc_ref, xl_ref, gate_ref, o_ref):
    x = _group_tile(xc_ref, xl_ref, TM_OUT)
    o_ref[...] = x + gate_ref[...] * jnp.dot(m_ref[...], w_ref[...], preferred_element_type=F32)


def _out_projection(merged, w_out, x_pair, mod4, layer):
    return pl.pallas_call(
        _outproj_kernel,
        grid=(N_TOK // TM_OUT,),
        in_specs=[
            pl.BlockSpec((TM_OUT, D_MODEL), lambda i: (i, 0)),
            pl.BlockSpec((D_MODEL, D_MODEL), lambda i: (0, 0)),
            *_group_specs(TM_OUT, D_MODEL),
            _mod_spec(layer, 2, TM_OUT),
        ],
        out_specs=pl.BlockSpec((TM_OUT, D_MODEL), lambda i: (i, 0)),
        out_shape=jax.ShapeDtypeStruct((N_TOK, D_MODEL), F32),
        compiler_params=pltpu.CompilerParams(
            dimension_semantics=("arbitrary",),
            vmem_limit_bytes=_vmem_limit(
                [((TM_OUT, D_MODEL), BF16), ((D_MODEL, D_MODEL), BF16)] + [((TM_OUT, D_MODEL), F32)] * 3,
                temps=[((TM_OUT, D_MODEL), F32)] * 2)),
        name="out_projection",
    )(merged, w_out, *x_pair, mod4)


def _make_ffn_kernel(final_norm):
    def kernel(*refs):
        x_ref, g_ref, shift_ref, scale_ref, gate_ref, wu_ref, ww_ref, wo_ref = refs[:8]
        refs = refs[8:]
        if final_norm:
            gf_ref = refs[0]
            refs = refs[1:]
        y_ctx_ref, y_lat_ref, h_scr, acc_scr = refs
        i, j = pl.program_id(0), pl.program_id(1)

        @pl.when(j == 0)
        def _():
            _mod_norm_rows(h_scr, (x_ref,), TM_FFN, g_ref[...], shift_ref[...], scale_ref[...])
            acc_scr[...] = jnp.zeros_like(acc_scr)

        h = h_scr[...]
        u = jnp.dot(h, wu_ref[...], preferred_element_type=F32)
        w = jnp.dot(h, ww_ref[...], preferred_element_type=F32)
        act = (u * _sigmoid(u) * w).astype(BF16)
        acc_scr[...] += jnp.dot(act, wo_ref[...], preferred_element_type=F32)

        def result():
            y = x_ref[...] + gate_ref[...] * acc_scr[...]
            return _rms(y, gf_ref[...]) if final_norm else y

        last = j == pl.num_programs(1) - 1

        @pl.when(last & (i < N_CTX // TM_FFN))
        def _():
            y_ctx_ref[...] = result()

        @pl.when(last & (i >= N_CTX // TM_FFN))
        def _():
            y_lat_ref[...] = result()

    return kernel


def _swiglu(x, g, mod4, layer, w_ffn_in, w_ffn_out, g_final=None):
    final_norm = g_final is not None
    nh = FF_HIDDEN // TH_FFN
    n_ctx_tiles = N_CTX // TM_FFN
    row = pl.BlockSpec((1, D_MODEL), lambda i, j: (0, 0))
    in_specs = [
        pl.BlockSpec((TM_FFN, D_MODEL), lambda i, j: (i, 0)),
        row,
        _mod_spec(layer, 3, TM_FFN), _mod_spec(layer, 4, TM_FFN), _mod_spec(layer, 5, TM_FFN),
        pl.BlockSpec((D_MODEL, TH_FFN), lambda i, j: (0, j)),
        pl.BlockSpec((D_MODEL, TH_FFN), lambda i, j: (0, nh + j)),
        pl.BlockSpec((TH_FFN, D_MODEL), lambda i, j: (j, 0)),
    ]
    args = [x, g, mod4, mod4, mod4, w_ffn_in, w_ffn_in, w_ffn_out]
    if final_norm:
        in_specs.append(row)
        args.append(g_final)
    out_specs = [pl.BlockSpec((TM_FFN, D_MODEL), lambda i, j: (jnp.minimum(i, n_ctx_tiles - 1), 0)),
                 pl.BlockSpec((TM_FFN, D_MODEL), lambda i, j: (jnp.maximum(i - n_ctx_tiles, 0), 0))]
    out_shape = [jax.ShapeDtypeStruct((N_CTX, D_MODEL), F32), jax.ShapeDtypeStruct((N_LAT, D_MODEL), F32)]
    scratch = [((TM_FFN, D_MODEL), BF16), ((TM_FFN, D_MODEL), F32)]
    return pl.pallas_call(
        _make_ffn_kernel(final_norm),
        grid=(N_TOK // TM_FFN, nh),
        in_specs=in_specs,
        out_specs=out_specs,
        out_shape=out_shape,
        scratch_shapes=[pltpu.VMEM(s, d) for s, d in scratch],
        compiler_params=pltpu.CompilerParams(
            dimension_semantics=("arbitrary", "arbitrary"),
            vmem_limit_bytes=_vmem_limit(
                [((TM_FFN, D_MODEL), F32)] * 3 + [((D_MODEL, TH_FFN), BF16)] * 2
                + [((TH_FFN, D_MODEL), BF16)],
                scratch=scratch,
                temps=[((TM_FFN, TH_FFN), F32)] * 4 + [((TM_FFN, D_MODEL), F32)] * 2)),
        name="swiglu_final" if final_norm else "swiglu",
    )(*args)


def _rope_tables():
    rows = DEC_SEQ // GRID_W
    row_ids = jnp.repeat(jnp.arange(rows), GRID_W).astype(F32)
    col_ids = jnp.tile(jnp.arange(GRID_W), rows).astype(F32)
    n_freq = DA_HEAD_DIM // 4
    inv = ROPE_THETA ** (-jnp.arange(n_freq, dtype=F32) / n_freq)
    ang_r, ang_c = row_ids[:, None] * inv, col_ids[:, None] * inv
    zero = jnp.zeros_like(ang_r)
    cos = jnp.concatenate([jnp.cos(ang_r)] * 2 + [jnp.cos(ang_c)] * 2, axis=-1)
    sin_lo = jnp.concatenate([-jnp.sin(ang_r), zero, -jnp.sin(ang_c), zero], axis=-1)
    sin_hi = jnp.concatenate([zero, jnp.sin(ang_r), zero, jnp.sin(ang_c)], axis=-1)
    return tuple(jnp.tile(t, (1, LANES // DA_HEAD_DIM)) for t in (cos, sin_lo, sin_hi))


def kernel(x_prompt, x_sample, cache_k, cache_v, c, c_ctx, w_mod, b_mod, g_norm1, w_in, da_lambda, da_subln,
           w_da_out, sc_conv, w_sc_out, cf_conv, cf_conv_b, cf_ln_g, cf_ln_b, w_cf_out, b_gate, w_out,
           g_norm2, w_ffn_in, w_ffn_out, g_final):
    x_pair = (x_prompt.reshape(N_CTX, D_MODEL), x_sample.reshape(N_LAT, D_MODEL))
    cvec =jnp.concatenate([c_ctx[None, :], c, jnp.zeros((MOD_ROWS - 1 - DEC_BATCH, D_MODEL), F32)], axis=0)
    mod4 = _modulation(cvec, w_mod, b_mod).reshape(DEPTH, MOD_ROWS, 1, 6 * D_MODEL)
    rope = _rope_tables()
    cache_k = cache_k.reshape(DEC_BATCH, DEPTH, PAST_LEN * DA_HEADS, 2 * DA_HEAD_DIM)
    cache_v = cache_v.reshape(DEC_BATCH, DEPTH, PAST_LEN * DA_HEADS, DA_VDIM)

    new_kv = tuple(jnp.zeros((BATCH, DEPTH, SEQ, DA_WIDTH), F32) for _ in range(2))
    for l in range(DEPTH):
        h = _pre_norm(x_pair, g_norm1[l][None, :], mod4, l)
        proj, gates, new_kv = _in_projection(h, l, w_in, new_kv)

        subln = da_subln[l][None, :]
        attn_ctx, w_out_bf, w_da_bf, w_sc_bf, w_cf_bf = _diff_attention(
            proj, da_lambda[l], subln, l, latent=False,
            cast_weights=[(w, l) for w in (w_out, w_da_out, w_sc_out, w_cf_out)])
        attn_lat, w_ffn_in_bf, w_ffn_out_bf = _diff_attention(
            proj, da_lambda[l], subln, l, latent=True, cache=(cache_k, cache_v), rope=rope,
            cast_weights=[(w_ffn_in, l), (w_ffn_out, l)])
        attn_pair = (attn_ctx, attn_lat)
        conv_args = (sc_conv[l], cf_conv[l], cf_conv_b[l][None, :], cf_ln_g[l][None, :], cf_ln_b[l][None, :])
        b_ctx, c_ctx_pre = _conv_branches(proj, *conv_args, latent=False)
        b_lat, c_lat = _conv_branches(proj, *conv_args, latent=True)

        merged = _gated_merge(attn_pair, (b_ctx, b_lat), (c_ctx_pre, c_lat), gates, b_gate[l][None, :],
                              w_da_bf, w_sc_bf, w_cf_bf)
        x_mid = _out_projection(merged, w_out_bf, x_pair, mod4, l)
        x_pair = _swiglu(x_mid, g_norm2[l][None, :], mod4, l, w_ffn_in_bf, w_ffn_out_bf,
                         g_final[None, :] if l == DEPTH - 1 else None)

    y_prompt, y_sample = x_pair
    new_k, new_v = new_kv
    return (y_prompt.reshape(BATCH, SEQ, D_MODEL), y_sample.reshape(DEC_BATCH, DEC_SEQ, D_MODEL),
            new_k.reshape(BATCH, DEPTH, SEQ, DA_HEADS, 2 * DA_HEAD_DIM),
            new_v.reshape(BATCH, DEPTH, SEQ, DA_HEADS, DA_VDIM))
```

```python
import math

import jax
import jax.numpy as jnp
from jax import lax
from jax.experimental import pallas as pl
from jax.experimental.pallas import tpu as pltpu

D_MODEL = 2048
BATCH = 16
SEQ = 256
DEPTH = 2
DEC_BATCH = 8
DEC_SEQ = 1024
PAST_LEN = 256
GRID_W = 64
DA_HEADS = 8
DA_HEAD_DIM = 64
DA_VDIM = 2 * DA_HEAD_DIM
DA_WIDTH = DA_HEADS * 2 * DA_HEAD_DIM
SC_WIDTH = 512
SC_K = 3
CF_WIDTH = 512
CF_K = 31
N_BRANCH = 3
FF_HIDDEN = -(-8 * D_MODEL // (3 * 256)) * 256
ROPE_THETA = 10000.0
EPS = 1e-6

OFF_K = DA_WIDTH
OFF_V = 2 * DA_WIDTH
OFF_SC = 3 * DA_WIDTH
OFF_CF = OFF_SC + 3 * SC_WIDTH
OFF_GATE = OFF_CF + 2 * CF_WIDTH
IN_COLS = OFF_GATE + N_BRANCH * D_MODEL

N_CTX = BATCH * SEQ
N_LAT = DEC_BATCH * DEC_SEQ
N_TOK = N_CTX + N_LAT
MOD_ROWS = 16

F32 = jnp.float32
BF16 = jnp.bfloat16

VMEM_BUDGET_V7X = 56 * 1024 * 1024
LANES = 128
SUBLANES = 8

TM_NORM = 512
TM_IN = 2048
TN = 512
TM_MERGE = 512
TM_OUT = 512
TM_FFN = 512
TH_FFN = 512
TQ_LAT = 1024
HEADS_LAT = 4
HEADS_CTX = 8
TQ_SUB = 256
LOG2_E = math.log2(math.e)
NORM_ROWS = 128
CONV_ROWS = 32


def _nbytes(shape, dtype):
    return math.prod(shape) * jnp.dtype(dtype).itemsize


def _vmem_limit(blocks, scratch=(), temps=()):
    total = 2 * sum(_nbytes(s, d) for s, d in blocks)
    total += sum(_nbytes(s, d) for s, d in scratch)
    total += sum(_nbytes(s, d) for s, d in temps)
    return min(total, VMEM_BUDGET_V7X)


def _mod_group(i, tm):
    return jnp.maximum(i * tm - N_CTX + DEC_SEQ, 0) // DEC_SEQ


def _rms(x, g):
    return x * lax.rsqrt(jnp.mean(x * x, axis=-1, keepdims=True) + EPS) * g


def _mod_norm_rows(o_ref, x_refs, n_rows, g, shift, scale):
    gain = g * (1.0 + scale)
    is_ctx = pl.program_id(0) < N_CTX // n_rows

    def load(rows):
        if len(x_refs) == 2:
            return jnp.where(is_ctx, x_refs[0][rows, :], x_refs[1][rows, :])
        return x_refs[0][rows, :]

    for c in range(n_rows // NORM_ROWS):
        rows = slice(c * NORM_ROWS, (c + 1) * NORM_ROWS)
        x = load(rows)
        r = lax.rsqrt(jnp.mean(x * x, axis=-1, keepdims=True) + EPS)
        o_ref[rows, :] = ((load(rows) * r) * gain + shift).astype(BF16)


def _sigmoid(x):
    return 0.5 * jnp.tanh(0.5 * x) + 0.5


def _mod_kernel(c_ref, w_ref, b_ref, o_ref):
    c = c_ref[...]
    s = (c * _sigmoid(c)).astype(BF16)
    o_ref[...] = jnp.dot(s, w_ref[...].astype(BF16), preferred_element_type=F32) + b_ref[...]


def _modulation(cvec, w_mod, b_mod):
    bn = 1024
    return pl.pallas_call(
        _mod_kernel,
        grid=(DEPTH, 6 * D_MODEL // bn),
        in_specs=[
            pl.BlockSpec((MOD_ROWS, D_MODEL), lambda l, j: (0, 0)),
            pl.BlockSpec((None, D_MODEL, bn), lambda l, j: (l, 0, j)),
            pl.BlockSpec((None, 1, bn), lambda l, j: (l, 0, j)),
        ],
        out_specs=pl.BlockSpec((None, MOD_ROWS, bn), lambda l, j: (l, 0, j)),
        out_shape=jax.ShapeDtypeStruct((DEPTH, MOD_ROWS, 6 * D_MODEL), F32),
        compiler_params=pltpu.CompilerParams(
            dimension_semantics=("parallel", "parallel"),
            vmem_limit_bytes=_vmem_limit(
                [((D_MODEL, bn), F32), ((MOD_ROWS, D_MODEL), F32), ((MOD_ROWS, bn), F32)],
                temps=[((D_MODEL, bn), BF16), ((D_MODEL, bn), F32)])),
        name="modulation",
    )(cvec, w_mod, b_mod.reshape(DEPTH, 1, 6 * D_MODEL))


def _mod_spec(layer, chunk, tm):
    return pl.BlockSpec((None, None, 1, D_MODEL),
                        lambda i, *_: (layer, _mod_group(i, tm), 0, chunk))


def _group_specs(tm, tn, n_col_tiles=None):
    nct = N_CTX // tm
    if n_col_tiles is None:
        return (pl.BlockSpec((tm, tn), lambda i, *_: (jnp.minimum(i, nct - 1), 0)),
                pl.BlockSpec((tm, tn), lambda i, *_: (jnp.maximum(i - nct, 0), 0)))
    last = n_col_tiles - 1
    return (pl.BlockSpec((tm, tn), lambda i, j: (jnp.minimum(i, nct - 1), jnp.where(i < nct, j, last))),
            pl.BlockSpec((tm, tn), lambda i, j: (jnp.maximum(i - nct, 0), jnp.where(i < nct, 0, j))))


def _group_tile(xc_ref, xl_ref, tm):
    return jnp.where(pl.program_id(0) < N_CTX // tm, xc_ref[...], xl_ref[...])


def _prenorm_kernel(xc_ref, xl_ref, g_ref, shift_ref, scale_ref, o_ref):
    _mod_norm_rows(o_ref, (xc_ref, xl_ref), TM_NORM, g_ref[...], shift_ref[...], scale_ref[...])


def _pre_norm(x_pair, g, mod4, layer):
    return pl.pallas_call(
        _prenorm_kernel,
        grid=(N_TOK // TM_NORM,),
        in_specs=[
            *_group_specs(TM_NORM, D_MODEL),
            pl.BlockSpec((1, D_MODEL), lambda i: (0, 0)),
            _mod_spec(layer, 0, TM_NORM),
            _mod_spec(layer, 1, TM_NORM),
        ],
        out_specs=pl.BlockSpec((TM_NORM, D_MODEL), lambda i: (i, 0)),
        out_shape=jax.ShapeDtypeStruct((N_TOK, D_MODEL), BF16),
        compiler_params=pltpu.CompilerParams(
            dimension_semantics=("arbitrary",),
            vmem_limit_bytes=_vmem_limit(
                [((TM_NORM, D_MODEL), F32)] * 2 + [((TM_NORM, D_MODEL), BF16)],
                temps=[((TM_NORM, D_MODEL), F32)] * 3)),
        name="pre_norm",
    )(*x_pair, g, mod4, mod4)


N_CTX_TILES = N_CTX // TM_IN
KV_TILES = DA_WIDTH // TN
MIX_COLS = OFF_GATE
GATE_COLS = N_BRANCH * D_MODEL


def _cast_weight_tile(w_ref, wbf_scr):
    @pl.when(pl.program_id(1) == 0)
    def _():
        wbf_scr[...] = w_ref[...].astype(BF16)


def _inproj_mix_kernel(h_ref, w_ref, nk_hbm, nv_hbm, o_ref, nk_ref, nv_ref, wbf_scr):
    del nk_hbm, nv_hbm
    j, i = pl.program_id(0), pl.program_id(1)
    _cast_weight_tile(w_ref, wbf_scr)
    o_ref[...] = jnp.dot(h_ref[...], wbf_scr[...], preferred_element_type=F32)

    @pl.when((i < N_CTX_TILES) & (j >= OFF_K // TN) & (j < OFF_V // TN))
    def _():
        nk_ref[...] = o_ref[...].reshape(nk_ref.shape)

    @pl.when((i < N_CTX_TILES) & (j >= OFF_V // TN) & (j < OFF_SC // TN))
    def _():
        nv_ref[...] = o_ref[...].reshape(nv_ref.shape)


def _inproj_gate_kernel(h_ref, wa_ref, wb_ref, o_ref, wa_scr, wb_scr):
    _cast_weight_tile(wa_ref, wa_scr)
    _cast_weight_tile(wb_ref, wb_scr)
    h = h_ref[...]
    o_ref[:, :TN] = jnp.dot(h, wa_scr[...], preferred_element_type=F32).astype(BF16)
    o_ref[:, TN:] = jnp.dot(h, wb_scr[...], preferred_element_type=F32).astype(BF16)


def _kv_cache_spec(layer, col0):
    def index(j, i):
        rel = j - col0
        row = jnp.where(rel < 0, 0,
                        jnp.where(rel >= KV_TILES, N_CTX_TILES - 1, jnp.minimum(i, N_CTX_TILES - 1)))
        return (row, layer, 0, jnp.clip(rel, 0, KV_TILES - 1))
    return pl.BlockSpec((TM_IN // SEQ, None, SEQ, TN), index)


def _in_projection(h, layer, w_in, new_kv):
    h_spec = pl.BlockSpec((TM_IN, D_MODEL), lambda j, i: (i, 0))
    kv_sds = jax.ShapeDtypeStruct((BATCH, DEPTH, SEQ, DA_WIDTH), F32)
    kv_block = ((TM_IN // SEQ, SEQ, TN), F32)
    hbm = pl.BlockSpec(memory_space=pl.ANY)
    mix, new_k, new_v = pl.pallas_call(
        _inproj_mix_kernel,
        grid=(MIX_COLS // TN, N_TOK // TM_IN),
        in_specs=[h_spec, pl.BlockSpec((None, D_MODEL, TN), lambda j, i: (layer, 0, j)), hbm, hbm],
        out_specs=[pl.BlockSpec((TM_IN, TN), lambda j, i: (i, j)),
                   _kv_cache_spec(layer, OFF_K // TN), _kv_cache_spec(layer, OFF_V // TN)],
        out_shape=[jax.ShapeDtypeStruct((N_TOK, MIX_COLS), F32), kv_sds, kv_sds],
        scratch_shapes=[pltpu.VMEM((D_MODEL, TN), BF16)],
        input_output_aliases={2: 1, 3: 2},
        compiler_params=pltpu.CompilerParams(
            dimension_semantics=("arbitrary", "arbitrary"),
            vmem_limit_bytes=_vmem_limit(
                [((TM_IN, D_MODEL), BF16), ((D_MODEL, TN), F32), ((TM_IN, TN), F32), kv_block, kv_block],
                scratch=[((D_MODEL, TN), BF16)], temps=[((TM_IN, TN), F32)])),
        name="in_projection_mix",
    )(h, w_in, *new_kv)
    gates = pl.pallas_call(
        _inproj_gate_kernel,
        grid=(GATE_COLS // (2 * TN), N_TOK // TM_IN),
        in_specs=[h_spec,
                  pl.BlockSpec((None, D_MODEL, TN), lambda j, i: (layer, 0, MIX_COLS // TN + 2 * j)),
                  pl.BlockSpec((None, D_MODEL, TN), lambda j, i: (layer, 0, MIX_COLS // TN + 2 * j + 1))],
        out_specs=pl.BlockSpec((TM_IN, 2 * TN), lambda j, i: (i, j)),
        out_shape=jax.ShapeDtypeStruct((N_TOK, GATE_COLS), BF16),
        scratch_shapes=[pltpu.VMEM((D_MODEL, TN), BF16)] * 2,
        compiler_params=pltpu.CompilerParams(
            dimension_semantics=("arbitrary", "arbitrary"),
            vmem_limit_bytes=_vmem_limit(
                [((TM_IN, D_MODEL), BF16), ((D_MODEL, 2 * TN), F32), ((TM_IN, 2 * TN), BF16)],
                scratch=[((D_MODEL, 2 * TN), BF16)],
                temps=[((TM_IN, 2 * TN), F32), ((TM_IN, 2 * TN), BF16)])),
        name="in_projection_gates",
    )(h, w_in, w_in)
    return mix, gates, (new_k, new_v)


def _rope(x, cos, sin_lo, sin_hi):
    return (x * cos + pltpu.roll(x, LANES - 16, 1) * sin_lo + pltpu.roll(x, 16, 1) * sin_hi)


def _make_attn_kernel(seq_len, tq, n_cache, n_heads, use_rope, lam_init, n_cast):
    def kernel(*refs):
        dl_ref, g_ref, q_ref, k_ref, v_ref = refs[:5]
        refs = refs[5:]
        if n_cache:
            ck_ref, cv_ref = refs[:2]
            refs = refs[2:]
        if use_rope:
            cos_ref, slo_ref, shi_ref = refs[:3]
            refs = refs[3:]
        cast_in, refs = refs[:n_cast], refs[n_cast:]
        o_ref, refs = refs[0], refs[1:]
        cast_out, refs = refs[:n_cast], refs[n_cast:]
        kk_scr, vv_scr = refs
        qb = pl.program_id(2)
        for src, dst in zip(cast_in, cast_out):
            dst[...] = src[...].astype(BF16)

        @pl.when(qb == 0)
        def _():
            for h in range(n_heads):
                cols = slice(h * LANES, (h + 1) * LANES)
                k = k_ref[:, cols]
                if use_rope:
                    k = _rope(k, cos_ref[...], slo_ref[...], shi_ref[...])
                if n_cache:
                    head = pl.program_id(1) * n_heads + h
                    rows = pl.ds(head, n_cache, stride=DA_HEADS)
                    kk_scr[h, 0:n_cache, :] = ck_ref[rows, :].astype(BF16)
                    vv_scr[h, 0:n_cache, 0:DA_VDIM] = cv_ref[rows, :].astype(BF16)
                kk_scr[h, n_cache:n_cache + seq_len, :] = k.astype(BF16)
                vv_scr[h, n_cache:n_cache + seq_len, 0:DA_VDIM] = v_ref[:, cols].astype(BF16)
                vv_scr[h, :, DA_VDIM:] = jnp.ones((n_cache + seq_len, DA_VDIM), BF16)

        dl = dl_ref[...]
        lam = (jnp.exp(jnp.sum(dl[0:1] * dl[1:2], axis=-1, keepdims=True))
               - jnp.exp(jnp.sum(dl[2:3] * dl[3:4], axis=-1, keepdims=True)) + lam_init)
        tsub = min(tq, TQ_SUB)
        lane = lax.broadcasted_iota(jnp.int32, (tsub, LANES), 1)

        def softmax_times_v(qm, kk, vv):
            s = lax.dot_general(qm.astype(BF16), kk, (((1,), (1,)), ((), ())), preferred_element_type=F32)
            e = jnp.exp2(s - jnp.max(s, axis=-1, keepdims=True)).astype(BF16)
            ev = jnp.dot(e, vv, preferred_element_type=F32)
            return ev[:, :DA_VDIM] / ev[:, DA_VDIM:]

        for h in range(n_heads):
            cols = slice(h * LANES, (h + 1) * LANES)
            kk, vv = kk_scr[h], vv_scr[h]
            for c in range(tq // tsub):
                q = q_ref[c * tsub:(c + 1) * tsub, cols]
                if use_rope:
                    rows = pl.ds(pl.multiple_of(qb * tq, tq) + c * tsub, tsub)
                    q = _rope(q, cos_ref[rows, :], slo_ref[rows, :], shi_ref[rows, :])
                q = q * (DA_HEAD_DIM ** -0.5 * LOG2_E)
                o1 = softmax_times_v(jnp.where(lane < DA_HEAD_DIM, q, 0.0), kk, vv)
                o2 = softmax_times_v(jnp.where(lane >= DA_HEAD_DIM, q, 0.0), kk, vv)
                o_ref[c * tsub:(c + 1) * tsub, cols] = (
                    _rms(o1 - lam * o2, g_ref[...]) * (1.0 - lam_init)).astype(BF16)

    return kernel


def _diff_attention(proj, da_lambda_l, subln_g, layer, *, latent, cache=None, rope=None, cast_weights=()):
    lam_init = 0.8 - 0.6 * math.exp(-0.3 * layer)
    if latent:
        n_seq, seq_len, tq, n_cache, row0, n_heads = DEC_BATCH, DEC_SEQ, TQ_LAT, PAST_LEN, N_CTX, HEADS_LAT
    else:
        n_seq, seq_len, tq, n_cache, row0, n_heads = BATCH, SEQ, SEQ, 0, 0, HEADS_CTX
    nq = seq_len // tq
    lk = n_cache + seq_len
    width = n_heads * LANES
    hk, hv = OFF_K // width, OFF_V // width

    in_specs = [
        pl.BlockSpec((4, DA_HEAD_DIM), lambda b, h, t: (0, 0)),
        pl.BlockSpec((1, DA_VDIM), lambda b, h, t: (0, 0)),
        pl.BlockSpec((tq, width), lambda b, h, t: (row0 // tq + b * nq + t, h)),
        pl.BlockSpec((seq_len, width), lambda b, h, t: (row0 // seq_len + b, hk + h)),
        pl.BlockSpec((seq_len, width), lambda b, h, t: (row0 // seq_len + b, hv + h)),
    ]
    args = [da_lambda_l, subln_g, proj, proj, proj]
    blocks = [((tq, width), F32), ((seq_len, width), F32), ((seq_len, width), F32), ((tq, width), BF16)]
    if latent:
        cache_k, cache_v = cache
        cache_block = (n_cache * DA_HEADS, LANES)
        in_specs += [pl.BlockSpec((None, None) + cache_block, lambda b, h, t: (b, layer, 0, 0))] * 2
        args += [cache_k, cache_v]
        in_specs += [pl.BlockSpec((seq_len, LANES), lambda b, h, t: (0, 0))] * 3
        args += list(rope)
        blocks += [(cache_block, F32)] * 2 + [((seq_len, LANES), F32)] * 3

    n_hgroups = DA_HEADS // n_heads
    n_steps = n_seq * n_hgroups * nq
    out_specs = [pl.BlockSpec((tq, width), lambda b, h, t: (b * nq + t, h))]
    out_shape = [jax.ShapeDtypeStruct((n_seq * seq_len, DA_WIDTH), BF16)]
    def step(b, h, t):
        return (b * n_hgroups + h) * nq + t

    for w, w_slab in cast_weights:
        _, rows, cols = w.shape
        slab = rows // n_steps
        assert slab * n_steps == rows and slab % 16 == 0, (w.shape, n_steps)
        in_specs.append(pl.BlockSpec((None, slab, cols), lambda b, h, t, w_slab=w_slab: (w_slab, step(b, h, t), 0)))
        args.append(w)
        out_specs.append(pl.BlockSpec((slab, cols), lambda b, h, t: (step(b, h, t), 0)))
        out_shape.append(jax.ShapeDtypeStruct((rows, cols), BF16))
        blocks += [((slab, cols), F32), ((slab, cols), BF16)]

    tsub = min(tq, TQ_SUB)
    scratch = [((n_heads, lk, LANES), BF16), ((n_heads, lk, 2 * DA_VDIM), BF16)]
    return pl.pallas_call(
        _make_attn_kernel(seq_len, tq, n_cache, n_heads, latent, lam_init, len(cast_weights)),
        grid=(n_seq, n_hgroups, nq),
        in_specs=in_specs,
        out_specs=out_specs,
        out_shape=out_shape,
        scratch_shapes=[pltpu.VMEM(s, d) for s, d in scratch],
        compiler_params=pltpu.CompilerParams(
            dimension_semantics=("parallel", "parallel", "arbitrary"),
            vmem_limit_bytes=_vmem_limit(
                blocks, scratch=scratch,
                temps=[((tsub, lk), F32)] * 6 * 4 + [((tq, 2 * DA_VDIM), F32)] * 4)),
        name="diff_attention_latent" if latent else "diff_attention_context",
    )(*args)


def _fill_shifted(scr, value, pad, seq_len, shifts):
    width = value.shape[-1]
    n = seq_len + 2 * pad - SUBLANES
    scr[0, 0:pad, :] = jnp.zeros((pad, width), F32)
    scr[0, pad + seq_len:, :] = jnp.zeros((pad, width), F32)
    scr[0, pad:pad + seq_len, :] = value
    for s, shift in enumerate(shifts, start=1):
        scr[s, 0:n, :] = scr[0, shift:shift + n, :]


def _tap(scr, shifts, r0, offset):
    aligned, rem = offset - offset % SUBLANES, offset % SUBLANES
    slot = 0 if rem == 0 else 1 + shifts.index(rem)
    return scr[slot, pl.ds(r0 + aligned, CONV_ROWS), :]


def _make_conv_kernel(seq_len):
    pad3, pad31 = SUBLANES, 2 * SUBLANES
    shifts3 = (1, SUBLANES - 1)
    shifts31 = tuple(range(1, SUBLANES))

    def kernel(gb_ref, gc_ref, sx_ref, ca_ref, cb_ref, w3_ref, w31_ref, b31_ref, lng_ref, lnb_ref,
               bo_ref, co_ref, t_scr, u_scr, y_scr):
        _fill_shifted(t_scr, gc_ref[...] * sx_ref[...], pad3, seq_len, shifts3)
        _fill_shifted(u_scr, ca_ref[...] * _sigmoid(cb_ref[...]), pad31, seq_len, shifts31)

        def chunk(c, carry):
            r0 = pl.multiple_of(c * CONV_ROWS, CONV_ROWS)
            rows = pl.ds(r0, CONV_ROWS)
            y = _tap(t_scr, shifts3, r0, pad3 - SC_K // 2) * w3_ref[0:1, :]
            for k in range(1, SC_K):
                y = y + _tap(t_scr, shifts3, r0, pad3 - SC_K // 2 + k) * w3_ref[k:k + 1, :]
            bo_ref[rows, :] = (gb_ref[rows, :] * y).astype(BF16)

            acc = _tap(u_scr, shifts31, r0, pad31 - CF_K // 2) * w31_ref[0:1, :]
            for k in range(1, CF_K):
                acc = acc + _tap(u_scr, shifts31, r0, pad31 - CF_K // 2 + k) * w31_ref[k:k + 1, :]
            y_scr[rows, :] = acc + b31_ref[...]
            return carry

        lax.fori_loop(0, seq_len // CONV_ROWS, chunk, 0)

        y = y_scr[...]
        xc = y - jnp.mean(y, axis=-1, keepdims=True)
        yn = xc * lax.rsqrt(jnp.mean(xc * xc, axis=-1, keepdims=True) + EPS) * lng_ref[...] + lnb_ref[...]
        co_ref[...] = (yn * _sigmoid(yn)).astype(BF16)

    return kernel


def _conv_branches(proj, sc_conv_l, cf_conv_l, cf_b, ln_g, ln_b, *, latent):
    if latent:
        n_seq, seq_len, row0 = DEC_BATCH, DEC_SEQ, N_CTX
    else:
        n_seq, seq_len, row0 = BATCH, SEQ, 0
    rb = row0 // seq_len
    c0 = OFF_SC // SC_WIDTH

    def col(cidx):
        return pl.BlockSpec((seq_len, SC_WIDTH), lambda b: (rb + b, cidx))

    def whole(shape):
        return pl.BlockSpec(shape, lambda b: (0, 0))

    in_specs = [col(c0), col(c0 + 1), col(c0 + 2), col(c0 + 3), col(c0 + 4),
                whole((SC_K, SC_WIDTH)), whole((CF_K, CF_WIDTH)),
                whole((1, CF_WIDTH)), whole((1, CF_WIDTH)), whole((1, CF_WIDTH))]
    args = [proj, proj, proj, proj, proj, sc_conv_l, cf_conv_l, cf_b, ln_g, ln_b]
    out_spec = pl.BlockSpec((seq_len, SC_WIDTH), lambda b: (b, 0))
    out_sds = jax.ShapeDtypeStruct((n_seq * seq_len, SC_WIDTH), BF16)
    scratch = [((3, seq_len + 2 * SUBLANES, SC_WIDTH), F32),
               ((SUBLANES, seq_len + 4 * SUBLANES, CF_WIDTH), F32),
               ((seq_len, CF_WIDTH), F32)]
    return pl.pallas_call(
        _make_conv_kernel(seq_len),
        grid=(n_seq,),
        in_specs=in_specs,
        out_specs=[out_spec, out_spec],
        out_shape=[out_sds, out_sds],
        scratch_shapes=[pltpu.VMEM(s, d) for s, d in scratch],
        compiler_params=pltpu.CompilerParams(
            dimension_semantics=("parallel",),
            vmem_limit_bytes=_vmem_limit(
                [((seq_len, SC_WIDTH), F32)] * 5 + [((seq_len, SC_WIDTH), BF16)] * 2,
                scratch=scratch,
                temps=[((seq_len, SC_WIDTH), F32)] * 4)),
        name="conv_branches_latent" if latent else "conv_branches_context",
    )(*args)


def _merge_kernel(ac_ref, al_ref, bc_ref, bl_ref, cc_ref, cl_ref, ga_ref, gb_ref, gc_ref,
                  bga_ref, bgb_ref, bgc_ref, wa_ref, wb_ref, wc_ref, o_ref):
    def merge(a_ref, b_ref, c_ref):
        br_a = jnp.dot(a_ref[...], wa_ref[...], preferred_element_type=F32)
        br_b = jnp.dot(b_ref[...], wb_ref[...], preferred_element_type=F32)
        br_c = jnp.dot(c_ref[...], wc_ref[...], preferred_element_type=F32)
        merged = (_sigmoid(ga_ref[...].astype(F32) + bga_ref[...]) * br_a
                  + _sigmoid(gb_ref[...].astype(F32) + bgb_ref[...]) * br_b
                  + _sigmoid(gc_ref[...].astype(F32) + bgc_ref[...]) * br_c)
        o_ref[...] = merged.astype(BF16)

    is_ctx = pl.program_id(0) < N_CTX // TM_MERGE
    pl.when(is_ctx)(lambda: merge(ac_ref, bc_ref, cc_ref))
    pl.when(jnp.logical_not(is_ctx))(lambda: merge(al_ref, bl_ref, cl_ref))


def _gated_merge(attn_pair, b_pair, c_pair, gates, b_gate_l, w_da, w_sc, w_cf):
    tm = TM_MERGE

    def gate(k):
        return pl.BlockSpec((tm, D_MODEL), lambda i: (i, k))

    def gbias(k):
        return pl.BlockSpec((1, D_MODEL), lambda i: (0, k))

    def resident(rows):
        return pl.BlockSpec((rows, D_MODEL), lambda i: (0, 0))

    return pl.pallas_call(
        _merge_kernel,
        grid=(N_TOK // tm,),
        in_specs=[
            *_group_specs(tm, DA_WIDTH), *_group_specs(tm, SC_WIDTH), *_group_specs(tm, CF_WIDTH),
            gate(0), gate(1), gate(2), gbias(0), gbias(1), gbias(2),
            resident(DA_WIDTH), resident(SC_WIDTH), resident(CF_WIDTH),
        ],
        out_specs=pl.BlockSpec((tm, D_MODEL), lambda i: (i, 0)),
        out_shape=jax.ShapeDtypeStruct((N_TOK, D_MODEL), BF16),
        compiler_params=pltpu.CompilerParams(
            dimension_semantics=("arbitrary",),
            vmem_limit_bytes=_vmem_limit(
                [((tm, DA_WIDTH), BF16), ((tm, SC_WIDTH), BF16), ((tm, CF_WIDTH), BF16)] * 2
                + [((tm, D_MODEL), BF16)] * 4
                + [((DA_WIDTH, D_MODEL), BF16), ((SC_WIDTH, D_MODEL), BF16), ((CF_WIDTH, D_MODEL), BF16)],
                temps=[((tm, D_MODEL), F32)] * 8)),
        name="gated_merge",
    )(*attn_pair, *b_pair, *c_pair, gates, gates, gates, b_gate_l, b_gate_l, b_gate_l, w_da, w_sc, w_cf)


def _outproj_kernel(m_ref, w_ref, xc_ref, xl_ref, gate_ref, o_ref):
    x = _group_tile(xc_ref, xl_ref, TM_OUT)
    o_ref[...] = x + gate_ref[...] * jnp.dot(m_ref[...], w_ref[...], preferred_element_type=F32)


def _out_projection(merged, w_out, x_pair, mod4, layer):
    return pl.pallas_call(
        _outproj_kernel,
        grid=(N_TOK // TM_OUT,),
        in_specs=[
            pl.BlockSpec((TM_OUT, D_MODEL), lambda i: (i, 0)),
            pl.BlockSpec((D_MODEL, D_MODEL), lambda i: (0, 0)),
            *_group_specs(TM_OUT, D_MODEL),
            _mod_spec(layer, 2, TM_OUT),
        ],
        out_specs=pl.BlockSpec((TM_OUT, D_MODEL), lambda i: (i, 0)),
        out_shape=jax.ShapeDtypeStruct((N_TOK, D_MODEL), F32),
        compiler_params=pltpu.CompilerParams(
            dimension_semantics=("arbitrary",),
            vmem_limit_bytes=_vmem_limit(
                [((TM_OUT, D_MODEL), BF16), ((D_MODEL, D_MODEL), BF16)] + [((TM_OUT, D_MODEL), F32)] * 3,
                temps=[((TM_OUT, D_MODEL), F32)] * 2)),
        name="out_projection",
    )(merged, w_out, *x_pair, mod4)


def _make_ffn_kernel(final_norm):
    def kernel(*refs):
        x_ref, g_ref, shift_ref, scale_ref, gate_ref, wu_ref, ww_ref, wo_ref = refs[:8]
        refs = refs[8:]
        if final_norm:
            gf_ref = refs[0]
            refs = refs[1:]
        y_ctx_ref, y_lat_ref, h_scr, acc_scr = refs
        i, j = pl.program_id(0), pl.program_id(1)

        @pl.when(j == 0)
        def _():
            _mod_norm_rows(h_scr, (x_ref,), TM_FFN, g_ref[...], shift_ref[...], scale_ref[...])
            acc_scr[...] = jnp.zeros_like(acc_scr)

        h = h_scr[...]
        u = jnp.dot(h, wu_ref[...], preferred_element_type=F32)
        w = jnp.dot(h, ww_ref[...], preferred_element_type=F32)
        act = (u * _sigmoid(u) * w).astype(BF16)
        acc_scr[...] += jnp.dot(act, wo_ref[...], preferred_element_type=F32)

        def result():
            y = x_ref[...] + gate_ref[...] * acc_scr[...]
            return _rms(y, gf_ref[...]) if final_norm else y

        last = j == pl.num_programs(1) - 1

        @pl.when(last & (i < N_CTX // TM_FFN))
        def _():
            y_ctx_ref[...] = result()

        @pl.when(last & (i >= N_CTX // TM_FFN))
        def _():
            y_lat_ref[...] = result()

    return kernel


def _swiglu(x, g, mod4, layer, w_ffn_in, w_ffn_out, g_final=None):
    final_norm = g_final is not None
    nh = FF_HIDDEN // TH_FFN
    n_ctx_tiles = N_CTX // TM_FFN
    row = pl.BlockSpec((1, D_MODEL), lambda i, j: (0, 0))
    in_specs = [
        pl.BlockSpec((TM_FFN, D_MODEL), lambda i, j: (i, 0)),
        row,
        _mod_spec(layer, 3, TM_FFN), _mod_spec(layer, 4, TM_FFN), _mod_spec(layer, 5, TM_FFN),
        pl.BlockSpec((D_MODEL, TH_FFN), lambda i, j: (0, j)),
        pl.BlockSpec((D_MODEL, TH_FFN), lambda i, j: (0, nh + j)),
        pl.BlockSpec((TH_FFN, D_MODEL), lambda i, j: (j, 0)),
    ]
    args = [x, g, mod4, mod4, mod4, w_ffn_in, w_ffn_in, w_ffn_out]
    if final_norm:
        in_specs.append(row)
        args.append(g_final)
    out_specs = [pl.BlockSpec((TM_FFN, D_MODEL), lambda i, j: (jnp.minimum(i, n_ctx_tiles - 1), 0)),
                 pl.BlockSpec((TM_FFN, D_MODEL), lambda i, j: (jnp.maximum(i - n_ctx_tiles, 0), 0))]
    out_shape = [jax.ShapeDtypeStruct((N_CTX, D_MODEL), F32), jax.ShapeDtypeStruct((N_LAT, D_MODEL), F32)]
    scratch = [((TM_FFN, D_MODEL), BF16), ((TM_FFN, D_MODEL), F32)]
    return pl.pallas_call(
        _make_ffn_kernel(final_norm),
        grid=(N_TOK // TM_FFN, nh),
        in_specs=in_specs,
        out_specs=out_specs,
        out_shape=out_shape,
        scratch_shapes=[pltpu.VMEM(s, d) for s, d in scratch],
        compiler_params=pltpu.CompilerParams(
            dimension_semantics=("arbitrary", "arbitrary"),
            vmem_limit_bytes=_vmem_limit(
                [((TM_FFN, D_MODEL), F32)] * 3 + [((D_MODEL, TH_FFN), BF16)] * 2
                + [((TH_FFN, D_MODEL), BF16)],
                scratch=scratch,
                temps=[((TM_FFN, TH_FFN), F32)] * 4 + [((TM_FFN, D_MODEL), F32)] * 2)),
        name="swiglu_final" if final_norm else "swiglu",
    )(*args)


def _rope_tables():
    rows = DEC_SEQ // GRID_W
    row_ids = jnp.repeat(jnp.arange(rows), GRID_W).astype(F32)
    col_ids = jnp.tile(jnp.arange(GRID_W), rows).astype(F32)
    n_freq = DA_HEAD_DIM // 4
    inv = ROPE_THETA ** (-jnp.arange(n_freq, dtype=F32) / n_freq)
    ang_r, ang_c = row_ids[:, None] * inv, col_ids[:, None] * inv
    zero = jnp.zeros_like(ang_r)
    cos = jnp.concatenate([jnp.cos(ang_r)] * 2 + [jnp.cos(ang_c)] * 2, axis=-1)
    sin_lo = jnp.concatenate([-jnp.sin(ang_r), zero, -jnp.sin(ang_c), zero], axis=-1)
    sin_hi = jnp.concatenate([zero, jnp.sin(ang_r), zero, jnp.sin(ang_c)], axis=-1)
    return tuple(jnp.tile(t, (1, LANES // DA_HEAD_DIM)) for t in (cos, sin_lo, sin_hi))


def kernel(x_prompt, x_sample, cache_k, cache_v, c, c_ctx, w_mod, b_mod, g_norm1, w_in, da_lambda, da_subln,
           w_da_out, sc_conv, w_sc_out, cf_conv, cf_conv_b, cf_ln_g, cf_ln_b, w_cf_out, b_gate, w_out,
           g_norm2, w_ffn_in, w_ffn_out, g_final):
    x_pair = (x_prompt.reshape(N_CTX, D_MODEL), x_sample.reshape(N_LAT, D_MODEL))
    cvec =jnp.concatenate([c_ctx[None, :], c, jnp.zeros((MOD_ROWS - 1 - DEC_BATCH, D_MODEL), F32)], axis=0)
    mod4 = _modulation(cvec, w_mod, b_mod).reshape(DEPTH, MOD_ROWS, 1, 6 * D_MODEL)
    rope = _rope_tables()
    cache_k = cache_k.reshape(DEC_BATCH, DEPTH, PAST_LEN * DA_HEADS, 2 * DA_HEAD_DIM)
    cache_v = cache_v.reshape(DEC_BATCH, DEPTH, PAST_LEN * DA_HEADS, DA_VDIM)

    new_kv = tuple(jnp.zeros((BATCH, DEPTH, SEQ, DA_WIDTH), F32) for _ in range(2))
    for l in range(DEPTH):
        h = _pre_norm(x_pair, g_norm1[l][None, :], mod4, l)
        proj, gates, new_kv = _in_projection(h, l, w_in, new_kv)

        subln = da_subln[l][None, :]
        attn_ctx, w_out_bf, w_da_bf, w_sc_bf, w_cf_bf = _diff_attention(
            proj, da_lambda[l], subln, l, latent=False,
            cast_weights=[(w, l) for w in (w_out, w_da_out, w_sc_out, w_cf_out)])
        attn_lat, w_ffn_in_bf, w_ffn_out_bf = _diff_attention(
            proj, da_lambda[l], subln, l, latent=True, cache=(cache_k, cache_v), rope=rope,
            cast_weights=[(w_ffn_in, l), (w_ffn_out, l)])
        attn_pair = (attn_ctx, attn_lat)
        conv_args = (sc_conv[l], cf_conv[l], cf_conv_b[l][None, :], cf_ln_g[l][None, :], cf_ln_b[l][None, :])
        b_ctx, c_ctx_pre = _conv_branches(proj, *conv_args, latent=False)
        b_lat, c_lat = _conv_branches(proj, *conv_args, latent=True)

        merged = _gated_merge(attn_pair, (b_ctx, b_lat), (c_ctx_pre, c_lat), gates, b_gate[l][None, :],
                              w_da_bf, w_sc_bf, w_cf_bf)
        x_mid = _out_projection(merged, w_out_bf, x_pair, mod4, l)
        x_pair = _swiglu(x_mid, g_norm2[l][None, :], mod4, l, w_ffn_in_bf, w_ffn_out_bf,
                         g_final[None, :] if l == DEPTH - 1 else None)

    y_prompt, y_sample = x_pair
    new_k, new_v = new_kv
    return (y_prompt.reshape(BATCH, SEQ, D_MODEL), y_sample.reshape(DEC_BATCH, DEC_SEQ, D_MODEL),
            new_k.reshape(BATCH, DEPTH, SEQ, DA_HEADS, 2 * DA_HEAD_DIM),
            new_v.reshape(BATCH, DEPTH, SEQ, DA_HEADS, DA_VDIM))
```

```python
import math

import jax
import jax.numpy as jnp
from jax import lax
from jax.experimental import pallas as pl
from jax.experimental.pallas import tpu as pltpu

D_MODEL = 2048
BATCH = 16
SEQ = 256
DEPTH = 2
DEC_BATCH = 8
DEC_SEQ = 1024
PAST_LEN = 256
GRID_W = 64
DA_HEADS = 8
DA_HEAD_DIM = 64
DA_VDIM = 2 * DA_HEAD_DIM
DA_WIDTH = DA_HEADS * 2 * DA_HEAD_DIM
SC_WIDTH = 512
SC_K = 3
CF_WIDTH = 512
CF_K = 31
N_BRANCH = 3
FF_HIDDEN = -(-8 * D_MODEL // (3 * 256)) * 256
ROPE_THETA = 10000.0
EPS = 1e-6

OFF_K = DA_WIDTH
OFF_V = 2 * DA_WIDTH
OFF_SC = 3 * DA_WIDTH
OFF_CF = OFF_SC + 3 * SC_WIDTH
OFF_GATE = OFF_CF + 2 * CF_WIDTH
IN_COLS = OFF_GATE + N_BRANCH * D_MODEL

N_CTX = BATCH * SEQ
N_LAT = DEC_BATCH * DEC_SEQ
N_TOK = N_CTX + N_LAT
MOD_ROWS = 16

F32 = jnp.float32
BF16 = jnp.bfloat16

VMEM_BUDGET_V7X = 56 * 1024 * 1024
LANES = 128
SUBLANES = 8

TM_NORM = 512
TM_IN = 2048
TN = 512
TM_MERGE = 512
TM_OUT = 512
TM_FFN = 512
TH_FFN = 512
TQ_LAT = 1024
HEADS_LAT = 2
HEADS_CTX = 8
TQ_SUB = 256
LOG2_E = math.log2(math.e)
NORM_ROWS = 128
CONV_ROWS = 32


def _nbytes(shape, dtype):
    return math.prod(shape) * jnp.dtype(dtype).itemsize


def _vmem_limit(blocks, scratch=(), temps=()):
    total = 2 * sum(_nbytes(s, d) for s, d in blocks)
    total += sum(_nbytes(s, d) for s, d in scratch)
    total += sum(_nbytes(s, d) for s, d in temps)
    return min(total, VMEM_BUDGET_V7X)


def _mod_group(i, tm):
    return jnp.maximum(i * tm - N_CTX + DEC_SEQ, 0) // DEC_SEQ


def _rms(x, g):
    return x * lax.rsqrt(jnp.mean(x * x, axis=-1, keepdims=True) + EPS) * g


def _mod_norm_rows(o_ref, x_refs, n_rows, g, shift, scale):
    gain = g * (1.0 + scale)
    is_ctx = pl.program_id(0) < N_CTX // n_rows

    def load(rows):
        if len(x_refs) == 2:
            return jnp.where(is_ctx, x_refs[0][rows, :], x_refs[1][rows, :])
        return x_refs[0][rows, :]

    for c in range(n_rows // NORM_ROWS):
        rows = slice(c * NORM_ROWS, (c + 1) * NORM_ROWS)
        x = load(rows)
        r = lax.rsqrt(jnp.mean(x * x, axis=-1, keepdims=True) + EPS)
        o_ref[rows, :] = ((load(rows) * r) * gain + shift).astype(BF16)


def _sigmoid(x):
    return 0.5 * jnp.tanh(0.5 * x) + 0.5


def _mod_kernel(c_ref, w_ref, b_ref, o_ref):
    c = c_ref[...]
    s = (c * _sigmoid(c)).astype(BF16)
    o_ref[...] = jnp.dot(s, w_ref[...].astype(BF16), preferred_element_type=F32) + b_ref[...]


def _modulation(cvec, w_mod, b_mod):
    bn = 1024
    return pl.pallas_call(
        _mod_kernel,
        grid=(DEPTH, 6 * D_MODEL // bn),
        in_specs=[
            pl.BlockSpec((MOD_ROWS, D_MODEL), lambda l, j: (0, 0)),
            pl.BlockSpec((None, D_MODEL, bn), lambda l, j: (l, 0, j)),
            pl.BlockSpec((None, 1, bn), lambda l, j: (l, 0, j)),
        ],
        out_specs=pl.BlockSpec((None, MOD_ROWS, bn), lambda l, j: (l, 0, j)),
        out_shape=jax.ShapeDtypeStruct((DEPTH, MOD_ROWS, 6 * D_MODEL), F32),
        compiler_params=pltpu.CompilerParams(
            dimension_semantics=("parallel", "parallel"),
            vmem_limit_bytes=_vmem_limit(
                [((D_MODEL, bn), F32), ((MOD_ROWS, D_MODEL), F32), ((MOD_ROWS, bn), F32)],
                temps=[((D_MODEL, bn), BF16), ((D_MODEL, bn), F32)])),
        name="modulation",
    )(cvec, w_mod, b_mod.reshape(DEPTH, 1, 6 * D_MODEL))


def _mod_spec(layer, chunk, tm):
    return pl.BlockSpec((None, None, 1, D_MODEL),
                        lambda i, *_: (layer, _mod_group(i, tm), 0, chunk))


def _group_specs(tm, tn, n_col_tiles=None):
    nct = N_CTX // tm
    if n_col_tiles is None:
        return (pl.BlockSpec((tm, tn), lambda i, *_: (jnp.minimum(i, nct - 1), 0)),
                pl.BlockSpec((tm, tn), lambda i, *_: (jnp.maximum(i - nct, 0), 0)))
    last = n_col_tiles - 1
    return (pl.BlockSpec((tm, tn), lambda i, j: (jnp.minimum(i, nct - 1), jnp.where(i < nct, j, last))),
            pl.BlockSpec((tm, tn), lambda i, j: (jnp.maximum(i - nct, 0), jnp.where(i < nct, 0, j))))


def _group_tile(xc_ref, xl_ref, tm):
    return jnp.where(pl.program_id(0) < N_CTX // tm, xc_ref[...], xl_ref[...])


def _prenorm_kernel(xc_ref, xl_ref, g_ref, shift_ref, scale_ref, o_ref):
    _mod_norm_rows(o_ref, (xc_ref, xl_ref), TM_NORM, g_ref[...], shift_ref[...], scale_ref[...])


def _pre_norm(x_pair, g, mod4, layer):
    return pl.pallas_call(
        _prenorm_kernel,
        grid=(N_TOK // TM_NORM,),
        in_specs=[
            *_group_specs(TM_NORM, D_MODEL),
            pl.BlockSpec((1, D_MODEL), lambda i: (0, 0)),
            _mod_spec(layer, 0, TM_NORM),
            _mod_spec(layer, 1, TM_NORM),
        ],
        out_specs=pl.BlockSpec((TM_NORM, D_MODEL), lambda i: (i, 0)),
        out_shape=jax.ShapeDtypeStruct((N_TOK, D_MODEL), BF16),
        compiler_params=pltpu.CompilerParams(
            dimension_semantics=("arbitrary",),
            vmem_limit_bytes=_vmem_limit(
                [((TM_NORM, D_MODEL), F32)] * 2 + [((TM_NORM, D_MODEL), BF16)],
                temps=[((TM_NORM, D_MODEL), F32)] * 3)),
        name="pre_norm",
    )(*x_pair, g, mod4, mod4)


N_CTX_TILES = N_CTX // TM_IN
KV_TILES = DA_WIDTH // TN
MIX_COLS = OFF_GATE
GATE_COLS = N_BRANCH * D_MODEL


def _cast_weight_tile(w_ref, wbf_scr):
    @pl.when(pl.program_id(1) == 0)
    def _():
        wbf_scr[...] = w_ref[...].astype(BF16)


def _inproj_mix_kernel(h_ref, w_ref, nk_hbm, nv_hbm, o_ref, nk_ref, nv_ref, wbf_scr):
    del nk_hbm, nv_hbm
    j, i = pl.program_id(0), pl.program_id(1)
    _cast_weight_tile(w_ref, wbf_scr)
    o_ref[...] = jnp.dot(h_ref[...], wbf_scr[...], preferred_element_type=F32)

    @pl.when((i < N_CTX_TILES) & (j >= OFF_K // TN) & (j < OFF_V // TN))
    def _():
        nk_ref[...] = o_ref[...].reshape(nk_ref.shape)

    @pl.when((i < N_CTX_TILES) & (j >= OFF_V // TN) & (j < OFF_SC // TN))
    def _():
        nv_ref[...] = o_ref[...].reshape(nv_ref.shape)


def _inproj_gate_kernel(h_ref, wa_ref, wb_ref, o_ref, wa_scr, wb_scr):
    _cast_weight_tile(wa_ref, wa_scr)
    _cast_weight_tile(wb_ref, wb_scr)
    h = h_ref[...]
    o_ref[:, :TN] = jnp.dot(h, wa_scr[...], preferred_element_type=F32).astype(BF16)
    o_ref[:, TN:] = jnp.dot(h, wb_scr[...], preferred_element_type=F32).astype(BF16)


def _kv_cache_spec(layer, col0):
    def index(j, i):
        rel = j - col0
        row = jnp.where(rel < 0, 0,
                        jnp.where(rel >= KV_TILES, N_CTX_TILES - 1, jnp.minimum(i, N_CTX_TILES - 1)))
        return (row, layer, 0, jnp.clip(rel, 0, KV_TILES - 1))
    return pl.BlockSpec((TM_IN // SEQ, None, SEQ, TN), index)


def _in_projection(h, layer, w_in, new_kv):
    h_spec = pl.BlockSpec((TM_IN, D_MODEL), lambda j, i: (i, 0))
    kv_sds = jax.ShapeDtypeStruct((BATCH, DEPTH, SEQ, DA_WIDTH), F32)
    kv_block = ((TM_IN // SEQ, SEQ, TN), F32)
    hbm = pl.BlockSpec(memory_space=pl.ANY)
    mix, new_k, new_v = pl.pallas_call(
        _inproj_mix_kernel,
        grid=(MIX_COLS // TN, N_TOK // TM_IN),
        in_specs=[h_spec, pl.BlockSpec((None, D_MODEL, TN), lambda j, i: (layer, 0, j)), hbm, hbm],
        out_specs=[pl.BlockSpec((TM_IN, TN), lambda j, i: (i, j)),
                   _kv_cache_spec(layer, OFF_K // TN), _kv_cache_spec(layer, OFF_V // TN)],
        out_shape=[jax.ShapeDtypeStruct((N_TOK, MIX_COLS), F32), kv_sds, kv_sds],
        scratch_shapes=[pltpu.VMEM((D_MODEL, TN), BF16)],
        input_output_aliases={2: 1, 3: 2},
        compiler_params=pltpu.CompilerParams(
            dimension_semantics=("arbitrary", "arbitrary"),
            vmem_limit_bytes=_vmem_limit(
                [((TM_IN, D_MODEL), BF16), ((D_MODEL, TN), F32), ((TM_IN, TN), F32), kv_block, kv_block],
                scratch=[((D_MODEL, TN), BF16)], temps=[((TM_IN, TN), F32)])),
        name="in_projection_mix",
    )(h, w_in, *new_kv)
    gates = pl.pallas_call(
        _inproj_gate_kernel,
        grid=(GATE_COLS // (2 * TN), N_TOK // TM_IN),
        in_specs=[h_spec,
                  pl.BlockSpec((None, D_MODEL, TN), lambda j, i: (layer, 0, MIX_COLS // TN + 2 * j)),
                  pl.BlockSpec((None, D_MODEL, TN), lambda j, i: (layer, 0, MIX_COLS // TN + 2 * j + 1))],
        out_specs=pl.BlockSpec((TM_IN, 2 * TN), lambda j, i: (i, j)),
        out_shape=jax.ShapeDtypeStruct((N_TOK, GATE_COLS), BF16),
        scratch_shapes=[pltpu.VMEM((D_MODEL, TN), BF16)] * 2,
        compiler_params=pltpu.CompilerParams(
            dimension_semantics=("arbitrary", "arbitrary"),
            vmem_limit_bytes=_vmem_limit(
                [((TM_IN, D_MODEL), BF16), ((D_MODEL, 2 * TN), F32), ((TM_IN, 2 * TN), BF16)],
                scratch=[((D_MODEL, 2 * TN), BF16)],
                temps=[((TM_IN, 2 * TN), F32), ((TM_IN, 2 * TN), BF16)])),
        name="in_projection_gates",
    )(h, w_in, w_in)
    return mix, gates, (new_k, new_v)


def _rope(x, cos, sin_lo, sin_hi):
    return (x * cos + pltpu.roll(x, LANES - 16, 1) * sin_lo + pltpu.roll(x, 16, 1) * sin_hi)


def _make_attn_kernel(seq_len, tq, n_cache, n_heads, use_rope, lam_init, n_cast):
    def kernel(*refs):
        dl_ref, g_ref, q_ref, k_ref, v_ref = refs[:5]
        refs = refs[5:]
        if n_cache:
            ck_ref, cv_ref = refs[:2]
            refs = refs[2:]
        if use_rope:
            cos_ref, slo_ref, shi_ref = refs[:3]
            refs = refs[3:]
        cast_in, refs = refs[:n_cast], refs[n_cast:]
        o_ref, refs = refs[0], refs[1:]
        cast_out, refs = refs[:n_cast], refs[n_cast:]
        kk_scr, vv_scr = refs
        qb = pl.program_id(2)
        for src, dst in zip(cast_in, cast_out):
            dst[...] = src[...].astype(BF16)

        @pl.when(qb == 0)
        def _():
            for h in range(n_heads):
                cols = slice(h * LANES, (h + 1) * LANES)
                k = k_ref[:, cols]
                if use_rope:
                    k = _rope(k, cos_ref[...], slo_ref[...], shi_ref[...])
                if n_cache:
                    head = pl.program_id(1) * n_heads + h
                    rows = pl.ds(head, n_cache, stride=DA_HEADS)
                    kk_scr[h, 0:n_cache, :] = ck_ref[rows, :].astype(BF16)
                    vv_scr[h, 0:n_cache, 0:DA_VDIM] = cv_ref[rows, :].astype(BF16)
                kk_scr[h, n_cache:n_cache + seq_len, :] = k.astype(BF16)
                vv_scr[h, n_cache:n_cache + seq_len, 0:DA_VDIM] = v_ref[:, cols].astype(BF16)
                vv_scr[h, :, DA_VDIM:] = jnp.ones((n_cache + seq_len, DA_VDIM), BF16)

        dl = dl_ref[...]
        lam = (jnp.exp(jnp.sum(dl[0:1] * dl[1:2], axis=-1, keepdims=True))
               - jnp.exp(jnp.sum(dl[2:3] * dl[3:4], axis=-1, keepdims=True)) + lam_init)
        tsub = min(tq, TQ_SUB)
        lane = lax.broadcasted_iota(jnp.int32, (tsub, LANES), 1)

        def softmax_times_v(qm, kk, vv):
            s = lax.dot_general(qm.astype(BF16), kk, (((1,), (1,)), ((), ())), preferred_element_type=F32)
            e = jnp.exp2(s - jnp.max(s, axis=-1, keepdims=True)).astype(BF16)
            ev = jnp.dot(e, vv, preferred_element_type=F32)
            return ev[:, :DA_VDIM] / ev[:, DA_VDIM:]

        for h in range(n_heads):
            cols = slice(h * LANES, (h + 1) * LANES)
            kk, vv = kk_scr[h], vv_scr[h]
            for c in range(tq // tsub):
                q = q_ref[c * tsub:(c + 1) * tsub, cols]
                if use_rope:
                    rows = pl.ds(pl.multiple_of(qb * tq, tq) + c * tsub, tsub)
                    q = _rope(q, cos_ref[rows, :], slo_ref[rows, :], shi_ref[rows, :])
                q = q * (DA_HEAD_DIM ** -0.5 * LOG2_E)
                o1 = softmax_times_v(jnp.where(lane < DA_HEAD_DIM, q, 0.0), kk, vv)
                o2 = softmax_times_v(jnp.where(lane >= DA_HEAD_DIM, q, 0.0), kk, vv)
                o_ref[c * tsub:(c + 1) * tsub, cols] = (
                    _rms(o1 - lam * o2, g_ref[...]) * (1.0 - lam_init)).astype(BF16)

    return kernel


def _diff_attention(proj, da_lambda_l, subln_g, layer, *, latent, cache=None, rope=None, cast_weights=()):
    lam_init = 0.8 - 0.6 * math.exp(-0.3 * layer)
    if latent:
        n_seq, seq_len, tq, n_cache, row0, n_heads = DEC_BATCH, DEC_SEQ, TQ_LAT, PAST_LEN, N_CTX, HEADS_LAT
    else:
        n_seq, seq_len, tq, n_cache, row0, n_heads = BATCH, SEQ, SEQ, 0, 0, HEADS_CTX
    nq = seq_len // tq
    lk = n_cache + seq_len
    width = n_heads * LANES
    hk, hv = OFF_K // width, OFF_V // width

    in_specs = [
        pl.BlockSpec((4, DA_HEAD_DIM), lambda b, h, t: (0, 0)),
        pl.BlockSpec((1, DA_VDIM), lambda b, h, t: (0, 0)),
        pl.BlockSpec((tq, width), lambda b, h, t: (row0 // tq + b * nq + t, h)),
        pl.BlockSpec((seq_len, width), lambda b, h, t: (row0 // seq_len + b, hk + h)),
        pl.BlockSpec((seq_len, width), lambda b, h, t: (row0 // seq_len + b, hv + h)),
    ]
    args = [da_lambda_l, subln_g, proj, proj, proj]
    blocks = [((tq, width), F32), ((seq_len, width), F32), ((seq_len, width), F32), ((tq, width), BF16)]
    if latent:
        cache_k, cache_v = cache
        cache_block = (n_cache * DA_HEADS, LANES)
        in_specs += [pl.BlockSpec((None, None) + cache_block, lambda b, h, t: (b, layer, 0, 0))] * 2
        args += [cache_k, cache_v]
        in_specs += [pl.BlockSpec((seq_len, LANES), lambda b, h, t: (0, 0))] * 3
        args += list(rope)
        blocks += [(cache_block, F32)] * 2 + [((seq_len, LANES), F32)] * 3

    n_hgroups = DA_HEADS // n_heads
    n_steps = n_seq * n_hgroups * nq
    out_specs = [pl.BlockSpec((tq, width), lambda b, h, t: (b * nq + t, h))]
    out_shape = [jax.ShapeDtypeStruct((n_seq * seq_len, DA_WIDTH), BF16)]
    def step(b, h, t):
        return (b * n_hgroups + h) * nq + t

    for w, w_slab in cast_weights:
        _, rows, cols = w.shape
        slab = rows // n_steps
        assert slab * n_steps == rows and slab % 16 == 0, (w.shape, n_steps)
        in_specs.append(pl.BlockSpec((None, slab, cols), lambda b, h, t, w_slab=w_slab: (w_slab, step(b, h, t), 0)))
        args.append(w)
        out_specs.append(pl.BlockSpec((slab, cols), lambda b, h, t: (step(b, h, t), 0)))
        out_shape.append(jax.ShapeDtypeStruct((rows, cols), BF16))
        blocks += [((slab, cols), F32), ((slab, cols), BF16)]

    tsub = min(tq, TQ_SUB)
    scratch = [((n_heads, lk, LANES), BF16), ((n_heads, lk, 2 * DA_VDIM), BF16)]
    return pl.pallas_call(
        _make_attn_kernel(seq_len, tq, n_cache, n_heads, latent, lam_init, len(cast_weights)),
        grid=(n_seq, n_hgroups, nq),
        in_specs=in_specs,
        out_specs=out_specs,
        out_shape=out_shape,
        scratch_shapes=[pltpu.VMEM(s, d) for s, d in scratch],
        compiler_params=pltpu.CompilerParams(
            dimension_semantics=("parallel", "parallel", "arbitrary"),
            vmem_limit_bytes=_vmem_limit(
                blocks, scratch=scratch,
                temps=[((tsub, lk), F32)] * 6 * 4 + [((tq, 2 * DA_VDIM), F32)] * 4)),
        name="diff_attention_latent" if latent else "diff_attention_context",
    )(*args)


def _fill_shifted(scr, value, pad, seq_len, shifts):
    width = value.shape[-1]
    n = seq_len + 2 * pad - SUBLANES
    scr[0, 0:pad, :] = jnp.zeros((pad, width), F32)
    scr[0, pad + seq_len:, :] = jnp.zeros((pad, width), F32)
    scr[0, pad:pad + seq_len, :] = value
    for s, shift in enumerate(shifts, start=1):
        scr[s, 0:n, :] = scr[0, shift:shift + n, :]


def _tap(scr, shifts, r0, offset):
    aligned, rem = offset - offset % SUBLANES, offset % SUBLANES
    slot = 0 if rem == 0 else 1 + shifts.index(rem)
    return scr[slot, pl.ds(r0 + aligned, CONV_ROWS), :]


def _make_conv_kernel(seq_len):
    pad3, pad31 = SUBLANES, 2 * SUBLANES
    shifts3 = (1, SUBLANES - 1)
    shifts31 = tuple(range(1, SUBLANES))

    def kernel(gb_ref, gc_ref, sx_ref, ca_ref, cb_ref, w3_ref, w31_ref, b31_ref, lng_ref, lnb_ref,
               bo_ref, co_ref, t_scr, u_scr, y_scr):
        _fill_shifted(t_scr, gc_ref[...] * sx_ref[...], pad3, seq_len, shifts3)
        _fill_shifted(u_scr, ca_ref[...] * _sigmoid(cb_ref[...]), pad31, seq_len, shifts31)

        def chunk(c, carry):
            r0 = pl.multiple_of(c * CONV_ROWS, CONV_ROWS)
            rows = pl.ds(r0, CONV_ROWS)
            y = _tap(t_scr, shifts3, r0, pad3 - SC_K // 2) * w3_ref[0:1, :]
            for k in range(1, SC_K):
                y = y + _tap(t_scr, shifts3, r0, pad3 - SC_K // 2 + k) * w3_ref[k:k + 1, :]
            bo_ref[rows, :] = (gb_ref[rows, :] * y).astype(BF16)

            acc = _tap(u_scr, shifts31, r0, pad31 - CF_K // 2) * w31_ref[0:1, :]
            for k in range(1, CF_K):
                acc = acc + _tap(u_scr, shifts31, r0, pad31 - CF_K // 2 + k) * w31_ref[k:k + 1, :]
            y_scr[rows, :] = acc + b31_ref[...]
            return carry

        lax.fori_loop(0, seq_len // CONV_ROWS, chunk, 0)

        y = y_scr[...]
        xc = y - jnp.mean(y, axis=-1, keepdims=True)
        yn = xc * lax.rsqrt(jnp.mean(xc * xc, axis=-1, keepdims=True) + EPS) * lng_ref[...] + lnb_ref[...]
        co_ref[...] = (yn * _sigmoid(yn)).astype(BF16)

    return kernel


def _conv_branches(proj, sc_conv_l, cf_conv_l, cf_b, ln_g, ln_b, *, latent):
    if latent:
        n_seq, seq_len, row0 = DEC_BATCH, DEC_SEQ, N_CTX
    else:
        n_seq, seq_len, row0 = BATCH, SEQ, 0
    rb = row0 // seq_len
    c0 = OFF_SC // SC_WIDTH

    def col(cidx):
        return pl.BlockSpec((seq_len, SC_WIDTH), lambda b: (rb + b, cidx))

    def whole(shape):
        return pl.BlockSpec(shape, lambda b: (0, 0))

    in_specs = [col(c0), col(c0 + 1), col(c0 + 2), col(c0 + 3), col(c0 + 4),
                whole((SC_K, SC_WIDTH)), whole((CF_K, CF_WIDTH)),
                whole((1, CF_WIDTH)), whole((1, CF_WIDTH)), whole((1, CF_WIDTH))]
    args = [proj, proj, proj, proj, proj, sc_conv_l, cf_conv_l, cf_b, ln_g, ln_b]
    out_spec = pl.BlockSpec((seq_len, SC_WIDTH), lambda b: (b, 0))
    out_sds = jax.ShapeDtypeStruct((n_seq * seq_len, SC_WIDTH), BF16)
    scratch = [((3, seq_len + 2 * SUBLANES, SC_WIDTH), F32),
               ((SUBLANES, seq_len + 4 * SUBLANES, CF_WIDTH), F32),
               ((seq_len, CF_WIDTH), F32)]
    return pl.pallas_call(
        _make_conv_kernel(seq_len),
        grid=(n_seq,),
        in_specs=in_specs,
        out_specs=[out_spec, out_spec],
        out_shape=[out_sds, out_sds],
        scratch_shapes=[pltpu.VMEM(s, d) for s, d in scratch],
        compiler_params=pltpu.CompilerParams(
            dimension_semantics=("parallel",),
            vmem_limit_bytes=_vmem_limit(
                [((seq_len, SC_WIDTH), F32)] * 5 + [((seq_len, SC_WIDTH), BF16)] * 2,
                scratch=scratch,
                temps=[((seq_len, SC_WIDTH), F32)] * 4)),
        name="conv_branches_latent" if latent else "conv_branches_context",
    )(*args)


def _merge_kernel(ac_ref, al_ref, bc_ref, bl_ref, cc_ref, cl_ref, ga_ref, gb_ref, gc_ref,
                  bga_ref, bgb_ref, bgc_ref, wa_ref, wb_ref, wc_ref, o_ref):
    def merge(a_ref, b_ref, c_ref):
        br_a = jnp.dot(a_ref[...], wa_ref[...], preferred_element_type=F32)
        br_b = jnp.dot(b_ref[...], wb_ref[...], preferred_element_type=F32)
        br_c = jnp.dot(c_ref[...], wc_ref[...], preferred_element_type=F32)
        merged = (_sigmoid(ga_ref[...].astype(F32) + bga_ref[...]) * br_a
                  + _sigmoid(gb_ref[...].astype(F32) + bgb_ref[...]) * br_b
                  + _sigmoid(gc_ref[...].astype(F32) + bgc_ref[...]) * br_c)
        o_ref[...] = merged.astype(BF16)

    is_ctx = pl.program_id(0) < N_CTX // TM_MERGE
    pl.when(is_ctx)(lambda: merge(ac_ref, bc_ref, cc_ref))
    pl.when(jnp.logical_not(is_ctx))(lambda: merge(al_ref, bl_ref, cl_ref))


def _gated_merge(attn_pair, b_pair, c_pair, gates, b_gate_l, w_da, w_sc, w_cf):
    tm = TM_MERGE

    def gate(k):
        return pl.BlockSpec((tm, D_MODEL), lambda i: (i, k))

    def gbias(k):
        return pl.BlockSpec((1, D_MODEL), lambda i: (0, k))

    def resident(rows):
        return pl.BlockSpec((rows, D_MODEL), lambda i: (0, 0))

    return pl.pallas_call(
        _merge_kernel,
        grid=(N_TOK // tm,),
        in_specs=[
            *_group_specs(tm, DA_WIDTH), *_group_specs(tm, SC_WIDTH), *_group_specs(tm, CF_WIDTH),
            gate(0), gate(1), gate(2), gbias(0), gbias(1), gbias(2),
            resident(DA_WIDTH), resident(SC_WIDTH), resident(CF_WIDTH),
        ],
        out_specs=pl.BlockSpec((tm, D_MODEL), lambda i: (i, 0)),
        out_shape=jax.ShapeDtypeStruct((N_TOK, D_MODEL), BF16),
        compiler_params=pltpu.CompilerParams(
            dimension_semantics=("arbitrary",),
            vmem_limit_bytes=_vmem_limit(
                [((tm, DA_WIDTH), BF16), ((tm, SC_WIDTH), BF16), ((tm, CF_WIDTH), BF16)] * 2
                + [((tm, D_MODEL), BF16)] * 4
                + [((DA_WIDTH, D_MODEL), BF16), ((SC_WIDTH, D_MODEL), BF16), ((CF_WIDTH, D_MODEL), BF16)],
                temps=[((tm, D_MODEL), F32)] * 8)),
        name="gated_merge",
    )(*attn_pair, *b_pair, *c_pair, gates, gates, gates, b_gate_l, b_gate_l, b_gate_l, w_da, w_sc, w_cf)


def _outproj_kernel(m_ref, w_ref, xc_ref, xl_ref, gate_ref, o_ref):
    x = _group_tile(xc_ref, xl_ref, TM_OUT)
    o_ref[...] = x + gate_ref[...] * jnp.dot(m_ref[...], w_ref[...], preferred_element_type=F32)


def _out_projection(merged, w_out, x_pair, mod4, layer):
    return pl.pallas_call(
        _outproj_kernel,
        grid=(N_TOK // TM_OUT,),
        in_specs=[
            pl.BlockSpec((TM_OUT, D_MODEL), lambda i: (i, 0)),
            pl.BlockSpec((D_MODEL, D_MODEL), lambda i: (0, 0)),
            *_group_specs(TM_OUT, D_MODEL),
            _mod_spec(layer, 2, TM_OUT),
        ],
        out_specs=pl.BlockSpec((TM_OUT, D_MODEL), lambda i: (i, 0)),
        out_shape=jax.ShapeDtypeStruct((N_TOK, D_MODEL), F32),
        compiler_params=pltpu.CompilerParams(
            dimension_semantics=("arbitrary",),
            vmem_limit_bytes=_vmem_limit(
                [((TM_OUT, D_MODEL), BF16), ((D_MODEL, D_MODEL), BF16)] + [((TM_OUT, D_MODEL), F32)] * 3,
                temps=[((TM_OUT, D_MODEL), F32)] * 2)),
        name="out_projection",
    )(merged, w_out, *x_pair, mod4)


def _make_ffn_kernel(final_norm):
    def kernel(*refs):
        x_ref, g_ref, shift_ref, scale_ref, gate_ref, wu_ref, ww_ref, wo_ref = refs[:8]
        refs = refs[8:]
        if final_norm:
            gf_ref = refs[0]
            refs = refs[1:]
        y_ctx_ref, y_lat_ref, h_scr, acc_scr = refs
        i, j = pl.program_id(0), pl.program_id(1)

        @pl.when(j == 0)
        def _():
            _mod_norm_rows(h_scr, (x_ref,), TM_FFN, g_ref[...], shift_ref[...], scale_ref[...])
            acc_scr[...] = jnp.zeros_like(acc_scr)

        h = h_scr[...]
        u = jnp.dot(h, wu_ref[...], preferred_element_type=F32)
        w = jnp.dot(h, ww_ref[...], preferred_element_type=F32)
        act = (u * _sigmoid(u) * w).astype(BF16)
        acc_scr[...] += jnp.dot(act, wo_ref[...], preferred_element_type=F32)

        def write_result(o_ref):
            for c in range(TM_FFN // NORM_ROWS):
                rows = slice(c * NORM_ROWS, (c + 1) * NORM_ROWS)
                y = x_ref[rows, :] + gate_ref[...] * acc_scr[rows, :]
                o_ref[rows, :] = _rms(y, gf_ref[...]) if final_norm else y

        last = j == pl.num_programs(1) - 1

        @pl.when(last & (i < N_CTX // TM_FFN))
        def _():
            write_result(y_ctx_ref)

        @pl.when(last & (i >= N_CTX // TM_FFN))
        def _():
            write_result(y_lat_ref)

    return kernel


def _swiglu(x, g, mod4, layer, w_ffn_in, w_ffn_out, g_final=None):
    final_norm = g_final is not None
    nh = FF_HIDDEN // TH_FFN
    n_ctx_tiles = N_CTX // TM_FFN
    row = pl.BlockSpec((1, D_MODEL), lambda i, j: (0, 0))
    in_specs = [
        pl.BlockSpec((TM_FFN, D_MODEL), lambda i, j: (i, 0)),
        row,
        _mod_spec(layer, 3, TM_FFN), _mod_spec(layer, 4, TM_FFN), _mod_spec(layer, 5, TM_FFN),
        pl.BlockSpec((D_MODEL, TH_FFN), lambda i, j: (0, j)),
        pl.BlockSpec((D_MODEL, TH_FFN), lambda i, j: (0, nh + j)),
        pl.BlockSpec((TH_FFN, D_MODEL), lambda i, j: (j, 0)),
    ]
    args = [x, g, mod4, mod4, mod4, w_ffn_in, w_ffn_in, w_ffn_out]
    if final_norm:
        in_specs.append(row)
        args.append(g_final)
    out_specs = [pl.BlockSpec((TM_FFN, D_MODEL), lambda i, j: (jnp.minimum(i, n_ctx_tiles - 1), 0)),
                 pl.BlockSpec((TM_FFN, D_MODEL), lambda i, j: (jnp.maximum(i - n_ctx_tiles, 0), 0))]
    out_shape = [jax.ShapeDtypeStruct((N_CTX, D_MODEL), F32), jax.ShapeDtypeStruct((N_LAT, D_MODEL), F32)]
    scratch = [((TM_FFN, D_MODEL), BF16), ((TM_FFN, D_MODEL), F32)]
    return pl.pallas_call(
        _make_ffn_kernel(final_norm),
        grid=(N_TOK // TM_FFN, nh),
        in_specs=in_specs,
        out_specs=out_specs,
        out_shape=out_shape,
        scratch_shapes=[pltpu.VMEM(s, d) for s, d in scratch],
        compiler_params=pltpu.CompilerParams(
            dimension_semantics=("arbitrary", "arbitrary"),
            vmem_limit_bytes=_vmem_limit(
                [((TM_FFN, D_MODEL), F32)] * 3 + [((D_MODEL, TH_FFN), BF16)] * 2
                + [((TH_FFN, D_MODEL), BF16)],
                scratch=scratch,
                temps=[((TM_FFN, TH_FFN), F32)] * 4 + [((TM_FFN, D_MODEL), F32)] * 2)),
        name="swiglu_final" if final_norm else "swiglu",
    )(*args)


def _rope_tables():
    rows = DEC_SEQ // GRID_W
    row_ids = jnp.repeat(jnp.arange(rows), GRID_W).astype(F32)
    col_ids = jnp.tile(jnp.arange(GRID_W), rows).astype(F32)
    n_freq = DA_HEAD_DIM // 4
    inv = ROPE_THETA ** (-jnp.arange(n_freq, dtype=F32) / n_freq)
    ang_r, ang_c = row_ids[:, None] * inv, col_ids[:, None] * inv
    zero = jnp.zeros_like(ang_r)
    cos = jnp.concatenate([jnp.cos(ang_r)] * 2 + [jnp.cos(ang_c)] * 2, axis=-1)
    sin_lo = jnp.concatenate([-jnp.sin(ang_r), zero, -jnp.sin(ang_c), zero], axis=-1)
    sin_hi = jnp.concatenate([zero, jnp.sin(ang_r), zero, jnp.sin(ang_c)], axis=-1)
    return tuple(jnp.tile(t, (1, LANES // DA_HEAD_DIM)) for t in (cos, sin_lo, sin_hi))


def kernel(x_prompt, x_sample, cache_k, cache_v, c, c_ctx, w_mod, b_mod, g_norm1, w_in, da_lambda, da_subln,
           w_da_out, sc_conv, w_sc_out, cf_conv, cf_conv_b, cf_ln_g, cf_ln_b, w_cf_out, b_gate, w_out,
           g_norm2, w_ffn_in, w_ffn_out, g_final):
    x_pair = (x_prompt.reshape(N_CTX, D_MODEL), x_sample.reshape(N_LAT, D_MODEL))
    cvec =jnp.concatenate([c_ctx[None, :], c, jnp.zeros((MOD_ROWS - 1 - DEC_BATCH, D_MODEL), F32)], axis=0)
    mod4 = _modulation(cvec, w_mod, b_mod).reshape(DEPTH, MOD_ROWS, 1, 6 * D_MODEL)
    rope = _rope_tables()
    cache_k = cache_k.reshape(DEC_BATCH, DEPTH, PAST_LEN * DA_HEADS, 2 * DA_HEAD_DIM)
    cache_v = cache_v.reshape(DEC_BATCH, DEPTH, PAST_LEN * DA_HEADS, DA_VDIM)

    new_kv = tuple(jnp.zeros((BATCH, DEPTH, SEQ, DA_WIDTH), F32) for _ in range(2))
    for l in range(DEPTH):
        h = _pre_norm(x_pair, g_norm1[l][None, :], mod4, l)
        proj, gates, new_kv = _in_projection(h, l, w_in, new_kv)

        subln = da_subln[l][None, :]
        attn_ctx, w_out_bf, w_da_bf, w_sc_bf, w_cf_bf = _diff_attention(
            proj, da_lambda[l], subln, l, latent=False,
            cast_weights=[(w, l) for w in (w_out, w_da_out, w_sc_out, w_cf_out)])
        attn_lat, w_ffn_in_bf, w_ffn_out_bf = _diff_attention(
            proj, da_lambda[l], subln, l, latent=True, cache=(cache_k, cache_v), rope=rope,
            cast_weights=[(w_ffn_in, l), (w_ffn_out, l)])
        attn_pair = (attn_ctx, attn_lat)
        conv_args = (sc_conv[l], cf_conv[l], cf_conv_b[l][None, :], cf_ln_g[l][None, :], cf_ln_b[l][None, :])
        b_ctx, c_ctx_pre = _conv_branches(proj, *conv_args, latent=False)
        b_lat, c_lat = _conv_branches(proj, *conv_args, latent=True)

        merged = _gated_merge(attn_pair, (b_ctx, b_lat), (c_ctx_pre, c_lat), gates, b_gate[l][None, :],
                              w_da_bf, w_sc_bf, w_cf_bf)
        x_mid = _out_projection(merged, w_out_bf, x_pair, mod4, l)
        x_pair = _swiglu(x_mid, g_norm2[l][None, :], mod4, l, w_ffn_in_bf, w_ffn_out_bf,
                         g_final[None, :] if l == DEPTH - 1 else None)

    y_prompt, y_sample = x_pair
    new_k, new_v = new_kv
    return (y_prompt.reshape(BATCH, SEQ, D_MODEL), y_sample.reshape(DEC_BATCH, DEC_SEQ, D_MODEL),
            new_k.reshape(BATCH, DEPTH, SEQ, DA_HEADS, 2 * DA_HEAD_DIM),
            new_v.reshape(BATCH, DEPTH, SEQ, DA_HEADS, DA_VDIM))
```
